```python
import jax, jax.numpy as jnp
from jax import lax
import numpy as np


D_MODEL = 2048
BATCH = 4
SEQ = 2048
DEPTH = 1
DEC_BATCH = 128
DEC_SEQ = 4
PAST_LEN = 16384
PAGE_SIZE = 128

GLA_HEADS = 4
GLA_DK = D_MODEL // (2 * GLA_HEADS)
GLA_DV = D_MODEL // GLA_HEADS
GLA_GATE_RANK = 16
GLA_GATE_TAU = 16.0
RET_HEADS = 8
RET_DK = D_MODEL // RET_HEADS
RET_DV = 2 * D_MODEL // RET_HEADS
ROPE_BASE = 10000.0
D_FF = ((8 * D_MODEL // 3 + 255) // 256) * 256
CHUNK = 64
EPS = 1e-6

GLA_QK = GLA_HEADS * GLA_DK
GLA_V = GLA_HEADS * GLA_DV
RET_QK = RET_HEADS * RET_DK
RET_V = RET_HEADS * RET_DV
IN_WIDTHS = (GLA_QK, GLA_QK, GLA_V, GLA_V, GLA_GATE_RANK,
             RET_QK, RET_QK, RET_V, RET_V, D_MODEL, D_MODEL)
IN_TOTAL = sum(IN_WIDTHS)

kernel_name = 'gla_retnet_parallel_macaron_decode_step'


def _rmsnorm(x, g):
    xf = x.astype(jnp.float32)
    y = xf * lax.rsqrt(jnp.mean(xf * xf, axis=-1, keepdims=True) + EPS)
    return (y * g.astype(jnp.float32)).astype(x.dtype)


def _swiglu(x, w1, w3, w2):
    return (jax.nn.silu(x @ w1) * (x @ w3)) @ w2


def _rope(x, pos):
    d = x.shape[-1]
    half = d // 2
    freqs = ROPE_BASE ** (-jnp.arange(half, dtype=jnp.float32) / half)
    ang = pos[:, None] * freqs[None, :]
    cos = jnp.cos(ang)[None, :, None, :]
    sin = jnp.sin(ang)[None, :, None, :]
    xf = x.astype(jnp.float32)
    x1, x2 = xf[..., :half], xf[..., half:]
    out = jnp.concatenate([x1 * cos - x2 * sin, x2 * cos + x1 * sin], axis=-1)
    return out.astype(x.dtype)


def _chunk_len(L):
    return CHUNK if L % CHUNK == 0 else L


def _to_chunks(t, C):
    B, L, H, d = t.shape
    return t.astype(jnp.float32).reshape(B, L // C, C, H, d).transpose(1, 0, 3, 2, 4)


def _from_chunks(t):
    n, B, H, C, d = t.shape
    return t.transpose(1, 0, 3, 2, 4).reshape(B, n * C, H, d)


def _gla_recurrence(q, k, v, log_a, s0):
    C = _chunk_len(q.shape[1])
    xs = (_to_chunks(q, C), _to_chunks(k, C), _to_chunks(v, C), _to_chunks(log_a, C))
    mask = jnp.tril(jnp.ones((C, C), dtype=bool))
    mid = C // 2

    def step(S, inp):
        qi, ki, vi, ai = inp
        b = jnp.cumsum(ai, axis=2)
        b_mid = b[:, :, mid:mid + 1, :]
        b_last = b[:, :, -1:, :]
        q_rel = qi * jnp.exp(b - b_mid)
        k_rel = ki * jnp.exp(b_mid - b)
        att = jnp.where(mask, jnp.einsum('bhtk,bhsk->bhts', q_rel, k_rel), 0.0)
        q_dec = qi * jnp.exp(b)
        o = jnp.einsum('bhts,bhsv->bhtv', att, vi) + jnp.einsum('bhtk,bhkv->bhtv', q_dec, S)
        k_end = ki * jnp.exp(b_last - b)
        S = jnp.exp(b_last)[:, :, 0, :, None] * S + jnp.einsum('bhsk,bhsv->bhkv', k_end, vi)
        return S, o

    S, o = lax.scan(step, s0.astype(jnp.float32), xs)
    return _from_chunks(o), S


def _retention_recurrence(q, k, v, log_gamma, s0):
    C = _chunk_len(q.shape[1])
    xs = (_to_chunks(q, C), _to_chunks(k, C), _to_chunks(v, C))
    idx = jnp.arange(C, dtype=jnp.float32)
    diff = idx[:, None] - idx[None, :]
    lg = log_gamma[:, None, None]
    decay = jnp.where(diff >= 0, jnp.exp(lg * jnp.maximum(diff, 0.0)), 0.0)
    q_decay = jnp.exp(log_gamma[:, None] * (idx + 1.0)[None, :])[None, :, :, None]
    k_decay = jnp.exp(log_gamma[:, None] * (C - 1.0 - idx)[None, :])[None, :, :, None]
    chunk_decay = jnp.exp(log_gamma * C)[None, :, None, None]

    def step(S, inp):
        qi, ki, vi = inp
        att = jnp.einsum('bhtk,bhsk->bhts', qi, ki) * decay[None]
        o = jnp.einsum('bhts,bhsv->bhtv', att, vi) + jnp.einsum('bhtk,bhkv->bhtv', qi * q_decay, S)
        S = chunk_decay * S + jnp.einsum('bhsk,bhsv->bhkv', ki * k_decay, vi)
        return S, o

    S, o = lax.scan(step, s0.astype(jnp.float32), xs)
    return _from_chunks(o), S


def _mixer(u, pos, s_gla, s_ret, w_in, w_gate_up, b_gate, gla_norm, w_gla_o,
           ret_norm, w_ret_o, w_out):
    B, L, _ = u.shape
    proj = u @ w_in
    points = []
    acc = 0
    for w in IN_WIDTHS[:-1]:
        acc += w
        points.append(acc)
    gq, gk, gv, gg, ga, rq, rk, rv, rg, za, zb = jnp.split(proj, points, axis=-1)

    logit = (ga @ w_gate_up + b_gate).astype(jnp.float32)
    log_a = jax.nn.log_sigmoid(logit) / GLA_GATE_TAU
    o_g, s_gla_new = _gla_recurrence(
        gq.reshape(B, L, GLA_HEADS, GLA_DK) * (GLA_DK ** -0.5),
        gk.reshape(B, L, GLA_HEADS, GLA_DK),
        gv.reshape(B, L, GLA_HEADS, GLA_DV),
        log_a.reshape(B, L, GLA_HEADS, GLA_DK), s_gla)
    o_g = _rmsnorm(o_g, gla_norm).reshape(B, L, GLA_V).astype(u.dtype) * jax.nn.silu(gg)
    branch_a = o_g @ w_gla_o

    log_gamma = jnp.log1p(-jnp.exp2(-5.0 - jnp.arange(RET_HEADS, dtype=jnp.float32)))
    q_r = _rope(rq.reshape(B, L, RET_HEADS, RET_DK), pos)
    k_r = _rope(rk.reshape(B, L, RET_HEADS, RET_DK), pos) * (RET_DK ** -0.5)
    o_r, s_ret_new = _retention_recurrence(q_r, k_r, rv.reshape(B, L, RET_HEADS, RET_DV),
                                           log_gamma, s_ret)
    o_r = _rmsnorm(o_r, ret_norm).reshape(B, L, RET_V).astype(u.dtype) * jax.nn.silu(rg)
    branch_b = o_r @ w_ret_o

    merged = jax.nn.sigmoid(za) * branch_a + jax.nn.sigmoid(zb) * branch_b
    return merged @ w_out, s_gla_new, s_ret_new


def _trunk(x, pos, s_gla0, s_ret0, ws):
    (ffn1_norm, ffn1_w1, ffn1_w3, ffn1_w2, mix_norm, w_in, w_gate_up, b_gate,
     gla_norm, w_gla_o, ret_norm, w_ret_o, w_out,
     ffn2_norm, ffn2_w1, ffn2_w3, ffn2_w2, final_norm) = ws
    h = x
    sg_list = []
    sr_list = []
    for l in range(DEPTH):
        h = h + 0.5 * _swiglu(_rmsnorm(h, ffn1_norm[l]), ffn1_w1[l], ffn1_w3[l], ffn1_w2[l])
        m, sg, sr = _mixer(_rmsnorm(h, mix_norm[l]), pos, s_gla0[l], s_ret0[l],
                           w_in[l], w_gate_up[l], b_gate[l], gla_norm[l], w_gla_o[l],
                           ret_norm[l], w_ret_o[l], w_out[l])
        h = h + m
        h = h + 0.5 * _swiglu(_rmsnorm(h, ffn2_norm[l]), ffn2_w1[l], ffn2_w3[l], ffn2_w2[l])
        sg_list.append(sg)
        sr_list.append(sr)
    return _rmsnorm(h, final_norm), jnp.stack(sg_list), jnp.stack(sr_list)


def setup_inputs(seed: int = 0) -> dict:
    key = jax.random.key(seed)
    ks = jax.random.split(key, 24)

    def nrm(k, shape, scale):
        return jax.random.normal(k, shape, jnp.float32) * scale

    def gain(k, shape):
        return 1.0 + 0.02 * jax.random.normal(k, shape, jnp.float32)

    D = D_MODEL
    return {
        'x_prompt': nrm(ks[0], (BATCH, SEQ, D), 1.0),
        'x_sample': nrm(ks[1], (DEC_BATCH, DEC_SEQ, D), 1.0),
        'state_gla': nrm(ks[2], (DEPTH, DEC_BATCH, GLA_HEADS, GLA_DK, GLA_DV), 0.1),
        'state_ret': nrm(ks[3], (DEPTH, DEC_BATCH, RET_HEADS, RET_DK, RET_DV), 0.1),
        'ffn1_norm': gain(ks[4], (DEPTH, D)),
        'ffn1_w1': nrm(ks[5], (DEPTH, D, D_FF), D ** -0.5),
        'ffn1_w3': nrm(ks[6], (DEPTH, D, D_FF), D ** -0.5),
        'ffn1_w2': nrm(ks[7], (DEPTH, D_FF, D), D_FF ** -0.5),
        'mix_norm': gain(ks[8], (DEPTH, D)),
        'w_in': nrm(ks[9], (DEPTH, D, IN_TOTAL), D ** -0.5),
        'w_gate_up': nrm(ks[10], (DEPTH, GLA_GATE_RANK, GLA_QK), GLA_GATE_RANK ** -0.5),
        'b_gate': nrm(ks[11], (DEPTH, GLA_QK), 0.02),
        'gla_norm': gain(ks[12], (DEPTH, GLA_HEADS, GLA_DV)),
        'w_gla_o': nrm(ks[13], (DEPTH, GLA_V, D), GLA_V ** -0.5),
        'ret_norm': gain(ks[14], (DEPTH, RET_HEADS, RET_DV)),
        'w_ret_o': nrm(ks[15], (DEPTH, RET_V, D), RET_V ** -0.5),
        'w_out': nrm(ks[16], (DEPTH, D, D), D ** -0.5),
        'ffn2_norm': gain(ks[17], (DEPTH, D)),
        'ffn2_w1': nrm(ks[18], (DEPTH, D, D_FF), D ** -0.5),
        'ffn2_w3': nrm(ks[19], (DEPTH, D, D_FF), D ** -0.5),
        'ffn2_w2': nrm(ks[20], (DEPTH, D_FF, D), D_FF ** -0.5),
        'final_norm': gain(ks[21], (D,)),
    }


def reference(x_prompt, x_sample, state_gla, state_ret,
              ffn1_norm, ffn1_w1, ffn1_w3, ffn1_w2, mix_norm, w_in, w_gate_up, b_gate,
              gla_norm, w_gla_o, ret_norm, w_ret_o, w_out,
              ffn2_norm, ffn2_w1, ffn2_w3, ffn2_w2, final_norm):
    ws = (ffn1_norm, ffn1_w1, ffn1_w3, ffn1_w2, mix_norm, w_in, w_gate_up, b_gate,
          gla_norm, w_gla_o, ret_norm, w_ret_o, w_out,
          ffn2_norm, ffn2_w1, ffn2_w3, ffn2_w2, final_norm)
    pos_prompt = jnp.arange(SEQ, dtype=jnp.float32)
    zg = jnp.zeros((DEPTH, x_prompt.shape[0], GLA_HEADS, GLA_DK, GLA_DV), jnp.float32)
    zr = jnp.zeros((DEPTH, x_prompt.shape[0], RET_HEADS, RET_DK, RET_DV), jnp.float32)
    y_prompt, gla_prompt, ret_prompt = _trunk(x_prompt, pos_prompt, zg, zr, ws)
    pos_sample = PAST_LEN + jnp.arange(x_sample.shape[1], dtype=jnp.float32)
    y_sample, gla_sample, ret_sample = _trunk(x_sample, pos_sample, state_gla, state_ret, ws)
    return (y_prompt, y_sample, gla_prompt, ret_prompt, gla_sample, ret_sample)
```

```python
import functools

import jax
import jax.numpy as jnp
from jax import lax
from jax.experimental import pallas as pl
from jax.experimental.pallas import tpu as pltpu

F32, BF16 = jnp.float32, jnp.bfloat16

D_MODEL = 2048
PAST_LEN = 16384
GLA_HEADS = 4
GLA_DK = D_MODEL // (2 * GLA_HEADS)
GLA_DV = D_MODEL // GLA_HEADS
GLA_GATE_RANK = 16
GLA_GATE_TAU = 16.0
RET_HEADS = 8
RET_DK = D_MODEL // RET_HEADS
RET_DV = 2 * D_MODEL // RET_HEADS
ROPE_BASE = 10000.0
EPS = 1e-6
GLA_QK = GLA_HEADS * GLA_DK
GLA_V = GLA_HEADS * GLA_DV
RET_QK = RET_HEADS * RET_DK
RET_V = RET_HEADS * RET_DV

LANES = 128
VMEM_LIMIT = 58 * 2**20

GLA_CHUNK = 64
RET_CHUNK = 256
SEQ_TILE = 512
DEC_BT = 8


def _cparams(*sem):
    return pltpu.CompilerParams(dimension_semantics=sem, vmem_limit_bytes=VMEM_LIMIT)


def _dot(a, b):
    return jnp.dot(a, b, preferred_element_type=F32)


def _dot_tb(a, b):
    return lax.dot_general(a, b, (((1,), (1,)), ((), ())), preferred_element_type=F32)


def _dot_ta(a, b):
    return lax.dot_general(a, b, (((0,), (0,)), ((), ())), preferred_element_type=F32)


def _rms(x, g):
    return x * lax.rsqrt(jnp.mean(x * x, axis=-1, keepdims=True) + EPS) * g


def _sigmoid(x):
    return 1.0 / (1.0 + jnp.exp(-x))


def _silu(x):
    return x * _sigmoid(x)


def _log_sigmoid(x):
    return jnp.minimum(x, 0.0) - jnp.log1p(jnp.exp(-jnp.abs(x)))


def _lane_replicated_column(row):
    return jnp.broadcast_to(row, (LANES, row.shape[-1])).T


def _ffn_body(x_ref, g_ref, w1_ref, w3_ref, w2_ref, g2_ref, *refs, tm, rows, final):
    if final:
        y_ref, u_scr, acc_scr = refs
    else:
        h_ref, un_ref, u_scr, acc_scr = refs
    j = pl.program_id(1)

    @pl.when(j == 0)
    def _():
        for r in range(0, tm, rows):
            u_scr[r:r + rows, :] = _rms(x_ref[r:r + rows, :], g_ref[...]).astype(BF16)

    u = u_scr[...]
    a = _dot(u, w1_ref[...])
    b = _dot(u, w3_ref[...])
    part = _dot((_silu(a) * b).astype(BF16), w2_ref[...])

    @pl.when(j == 0)
    def _():
        acc_scr[...] = part

    @pl.when(j > 0)
    def _():
        acc_scr[...] += part

    @pl.when(j == pl.num_programs(1) - 1)
    def _():
        for r in range(0, tm, rows):
            h = x_ref[r:r + rows, :] + 0.5 * acc_scr[r:r + rows, :]
            if final:
                y_ref[r:r + rows, :] = _rms(h, g2_ref[...])
            else:
                h_ref[r:r + rows, :] = h
                un_ref[r:r + rows, :] = _rms(h, g2_ref[...]).astype(BF16)


def _ffn(x, g, w1, w3, w2, g2, *, final, tm=512, tf=512):
    t, d = x.shape
    dff = w1.shape[1]
    tm = min(tm, t)
    grid = (t // tm, dff // tf)
    row = pl.BlockSpec((tm, d), lambda i, j: (i, 0))
    vec = pl.BlockSpec((1, d), lambda i, j: (0, 0))
    if final:
        out_shape = jax.ShapeDtypeStruct((t, d), F32)
        out_specs = row
    else:
        out_shape = (jax.ShapeDtypeStruct((t, d), F32), jax.ShapeDtypeStruct((t, d), BF16))
        out_specs = (row, row)
    return pl.pallas_call(
        functools.partial(_ffn_body, tm=tm, rows=min(256, tm), final=final),
        grid=grid,
        in_specs=[row, vec,
                  pl.BlockSpec((d, tf), lambda i, j: (0, j)),
                  pl.BlockSpec((d, tf), lambda i, j: (0, j)),
                  pl.BlockSpec((tf, d), lambda i, j: (j, 0)),
                  vec],
        out_specs=out_specs,
        out_shape=out_shape,
        scratch_shapes=[pltpu.VMEM((tm, d), BF16), pltpu.VMEM((tm, d), F32)],
        compiler_params=_cparams("parallel", "arbitrary"),
        name="ffn_final" if final else "ffn_mid",
    )(x, g.reshape(1, d), w1, w3, w2, g2.reshape(1, d))


def _merge_body(h_ref, u_ref, og_ref, or_ref, wza_ref, wzb_ref, wgo_ref, wro_ref, wout_ref, o_ref, acc_scr):
    j = pl.program_id(1)
    u = u_ref[...]
    za = _dot(u, wza_ref[...])
    zb = _dot(u, wzb_ref[...])
    branch_a = _dot(og_ref[...].astype(BF16), wgo_ref[...])
    branch_b = _dot(or_ref[...].astype(BF16), wro_ref[...])
    merged = _sigmoid(za) * branch_a + _sigmoid(zb) * branch_b
    part = _dot(merged.astype(BF16), wout_ref[...])

    @pl.when(j == 0)
    def _():
        acc_scr[...] = part

    @pl.when(j > 0)
    def _():
        acc_scr[...] += part

    @pl.when(j == pl.num_programs(1) - 1)
    def _():
        o_ref[...] = h_ref[...] + acc_scr[...]


def _merge(h, u, og, orr, wza, wzb, wgo, wro, wout, *, tm=512, tn=256):
    t, d = h.shape
    tm = min(tm, t)
    return pl.pallas_call(
        _merge_body,
        grid=(t // tm, d // tn),
        in_specs=[pl.BlockSpec((tm, d), lambda i, j: (i, 0)),
                  pl.BlockSpec((tm, d), lambda i, j: (i, 0)),
                  pl.BlockSpec((tm, GLA_V), lambda i, j: (i, 0)),
                  pl.BlockSpec((tm, RET_V), lambda i, j: (i, 0)),
                  pl.BlockSpec((d, tn), lambda i, j: (0, j)),
                  pl.BlockSpec((d, tn), lambda i, j: (0, j)),
                  pl.BlockSpec((GLA_V, tn), lambda i, j: (0, j)),
                  pl.BlockSpec((RET_V, tn), lambda i, j: (0, j)),
                  pl.BlockSpec((tn, d), lambda i, j: (j, 0))],
        out_specs=pl.BlockSpec((tm, d), lambda i, j: (i, 0)),
        out_shape=jax.ShapeDtypeStruct((t, d), F32),
        scratch_shapes=[pltpu.VMEM((tm, d), F32)],
        compiler_params=_cparams("parallel", "arbitrary"),
        name="merge_out",
    )(h, u, og, orr, wza, wzb, wgo, wro, wout)


def _gla_prompt_body(u_ref, wq_ref, wk_ref, wv_ref, wg_ref, wga_ref, wgu_ref, bg_ref, ng_ref,
                     og_ref, st_ref, s_scr, q_scr, k_scr, v_scr, la_scr, o_scr, *, ts, c):
    s = pl.program_id(2)

    @pl.when(s == 0)
    def _():
        s_scr[...] = jnp.zeros_like(s_scr)

    u = u_ref[...]
    q_scr[...] = _dot(u, wq_ref[...]) * (GLA_DK ** -0.5)
    k_scr[...] = _dot(u, wk_ref[...])
    v_scr[...] = _dot(u, wv_ref[...])
    ga = _dot(u, wga_ref[...])
    logit = _dot(ga.astype(BF16), wgu_ref[...]) + bg_ref[...]
    la_scr[...] = _log_sigmoid(logit) / GLA_GATE_TAU

    rr = lax.broadcasted_iota(jnp.int32, (c, c), 0)
    cc = lax.broadcasted_iota(jnp.int32, (c, c), 1)
    causal = rr >= cc
    tril = causal.astype(BF16)
    mid = c // 2

    def chunk(ci, carry):
        r0 = pl.multiple_of(ci * c, c)
        q = q_scr[pl.ds(r0, c), :]
        k = k_scr[pl.ds(r0, c), :]
        v = v_scr[pl.ds(r0, c), :].astype(BF16)
        la = la_scr[pl.ds(r0, c), :]
        la_hi = la.astype(BF16)
        la_lo = (la - la_hi.astype(F32)).astype(BF16)
        b = _dot(tril, la_hi) + _dot(tril, la_lo)
        b_mid = b[mid:mid + 1, :]
        b_last = b[c - 1:c, :]
        q_rel = (q * jnp.exp(b - b_mid)).astype(BF16)
        k_rel = (k * jnp.exp(b_mid - b)).astype(BF16)
        att = jnp.where(causal, _dot_tb(q_rel, k_rel), 0.0)
        q_dec = (q * jnp.exp(b)).astype(BF16)
        st = s_scr[...]
        o_scr[pl.ds(r0, c), :] = _dot(att.astype(BF16), v) + _dot(q_dec, st.astype(BF16))
        k_end = (k * jnp.exp(b_last - b)).astype(BF16)
        upd = _dot_ta(k_end, v)
        dcol = _lane_replicated_column(jnp.exp(b_last))
        for jj in range(GLA_DV // LANES):
            sl = slice(jj * LANES, (jj + 1) * LANES)
            s_scr[:, sl] = dcol * st[:, sl] + upd[:, sl]
        return carry

    lax.fori_loop(0, ts // c, chunk, 0)

    g = _dot(u, wg_ref[...])
    og_ref[...] = (_rms(o_scr[...], ng_ref[...]) * _silu(g)).astype(BF16)

    @pl.when(s == pl.num_programs(2) - 1)
    def _():
        st_ref[...] = s_scr[...]


def _gla_prompt(u, batch, seq, wq, wk, wv, wg, wga, wgu, bg, ng):
    ts, c = SEQ_TILE, GLA_CHUNK
    ns = seq // ts
    d = u.shape[1]
    return pl.pallas_call(
        functools.partial(_gla_prompt_body, ts=ts, c=c),
        grid=(batch, GLA_HEADS, ns),
        in_specs=[pl.BlockSpec((ts, d), lambda b, h, s: (b * ns + s, 0)),
                  pl.BlockSpec((d, GLA_DK), lambda b, h, s: (0, h)),
                  pl.BlockSpec((d, GLA_DK), lambda b, h, s: (0, h)),
                  pl.BlockSpec((d, GLA_DV), lambda b, h, s: (0, h)),
                  pl.BlockSpec((d, GLA_DV), lambda b, h, s: (0, h)),
                  pl.BlockSpec((d, GLA_GATE_RANK), lambda b, h, s: (0, 0)),
                  pl.BlockSpec((GLA_GATE_RANK, GLA_DK), lambda b, h, s: (0, h)),
                  pl.BlockSpec((1, GLA_DK), lambda b, h, s: (0, h)),
                  pl.BlockSpec((None, 1, GLA_DV), lambda b, h, s: (h, 0, 0))],
        out_specs=(pl.BlockSpec((ts, GLA_DV), lambda b, h, s: (b * ns + s, h)),
                   pl.BlockSpec((None, None, GLA_DK, GLA_DV), lambda b, h, s: (b, h, 0, 0))),
        out_shape=(jax.ShapeDtypeStruct((batch * seq, GLA_V), BF16),
                   jax.ShapeDtypeStruct((batch, GLA_HEADS, GLA_DK, GLA_DV), F32)),
        scratch_shapes=[pltpu.VMEM((GLA_DK, GLA_DV), F32),
                        pltpu.VMEM((ts, GLA_DK), F32), pltpu.VMEM((ts, GLA_DK), F32),
                        pltpu.VMEM((ts, GLA_DV), F32), pltpu.VMEM((ts, GLA_DK), F32),
                        pltpu.VMEM((ts, GLA_DV), F32)],
        compiler_params=_cparams("parallel", "parallel", "arbitrary"),
        name="gla_prompt",
    )(u, wq, wk, wv, wg, wga, wgu, bg, ng)


def _rope(x, cos, sin):
    half = x.shape[-1] // 2
    x1, x2 = x[:, :half], x[:, half:]
    return jnp.concatenate([x1 * cos - x2 * sin, x2 * cos + x1 * sin], axis=-1)


def _ret_prompt_body(u_ref, wq_ref, wk_ref, wv_ref, wg_ref, cos_ref, sin_ref, dm_ref, qd_ref, kd_ref, cd_ref,
                     ng_ref, or_ref, st_ref, s_scr, q_scr, k_scr, v_scr, o_scr, *, ts, c):
    s = pl.program_id(2)

    @pl.when(s == 0)
    def _():
        s_scr[...] = jnp.zeros_like(s_scr)

    u = u_ref[...]
    cos, sin = cos_ref[...], sin_ref[...]
    q_scr[...] = _rope(_dot(u, wq_ref[...]), cos, sin)
    k_scr[...] = _rope(_dot(u, wk_ref[...]), cos, sin) * (RET_DK ** -0.5)
    v_scr[...] = _dot(u, wv_ref[...])

    def chunk(ci, carry):
        r0 = pl.multiple_of(ci * c, c)
        q = q_scr[pl.ds(r0, c), :]
        k = k_scr[pl.ds(r0, c), :]
        v = v_scr[pl.ds(r0, c), :].astype(BF16)
        att = _dot_tb(q.astype(BF16), k.astype(BF16)) * dm_ref[...]
        st = s_scr[...]
        o_scr[pl.ds(r0, c), :] = (_dot(att.astype(BF16), v)
                                  + _dot((q * qd_ref[...]).astype(BF16), st.astype(BF16)))
        s_scr[...] = cd_ref[...] * st + _dot_ta((k * kd_ref[...]).astype(BF16), v)
        return carry

    lax.fori_loop(0, ts // c, chunk, 0)

    g = _dot(u, wg_ref[...])
    or_ref[...] = (_rms(o_scr[...], ng_ref[...]) * _silu(g)).astype(BF16)

    @pl.when(s == pl.num_programs(2) - 1)
    def _():
        st_ref[...] = s_scr[...]


def _ret_log_gamma():
    return jnp.log1p(-jnp.exp2(-5.0 - jnp.arange(RET_HEADS, dtype=F32)))


def _rope_tables(pos):
    half = RET_DK // 2
    freqs = ROPE_BASE ** (-jnp.arange(half, dtype=F32) / half)
    ang = pos[:, None] * freqs[None, :]
    return jnp.cos(ang), jnp.sin(ang)


def _ret_prompt(u, batch, seq, wq, wk, wv, wg, ng):
    ts, c = SEQ_TILE, RET_CHUNK
    ns = seq // ts
    d = u.shape[1]
    cos, sin = _rope_tables(jnp.arange(seq, dtype=F32))
    lg = _ret_log_gamma()
    idx = jnp.arange(c, dtype=F32)
    diff = idx[:, None] - idx[None, :]
    dmat = jnp.where(diff >= 0, jnp.exp(lg[:, None, None] * jnp.maximum(diff, 0.0)), 0.0)
    qdec = jnp.broadcast_to(jnp.exp(lg[:, None] * (idx + 1.0)[None, :])[:, :, None], (RET_HEADS, c, RET_DK))
    kdec = jnp.broadcast_to(jnp.exp(lg[:, None] * (c - 1.0 - idx)[None, :])[:, :, None], (RET_HEADS, c, RET_DK))
    cdec = jnp.broadcast_to(jnp.exp(lg * c)[:, None, None], (RET_HEADS, 1, RET_DV))
    return pl.pallas_call(
        functools.partial(_ret_prompt_body, ts=ts, c=c),
        grid=(batch, RET_HEADS, ns),
        in_specs=[pl.BlockSpec((ts, d), lambda b, h, s: (b * ns + s, 0)),
                  pl.BlockSpec((d, RET_DK), lambda b, h, s: (0, h)),
                  pl.BlockSpec((d, RET_DK), lambda b, h, s: (0, h)),
                  pl.BlockSpec((d, RET_DV), lambda b, h, s: (0, h)),
                  pl.BlockSpec((d, RET_DV), lambda b, h, s: (0, h)),
                  pl.BlockSpec((ts, RET_DK // 2), lambda b, h, s: (s, 0)),
                  pl.BlockSpec((ts, RET_DK // 2), lambda b, h, s: (s, 0)),
                  pl.BlockSpec((None, c, c), lambda b, h, s: (h, 0, 0)),
                  pl.BlockSpec((None, c, RET_DK), lambda b, h, s: (h, 0, 0)),
                  pl.BlockSpec((None, c, RET_DK), lambda b, h, s: (h, 0, 0)),
                  pl.BlockSpec((None, 1, RET_DV), lambda b, h, s: (h, 0, 0)),
                  pl.BlockSpec((None, 1, RET_DV), lambda b, h, s: (h, 0, 0))],
        out_specs=(pl.BlockSpec((ts, RET_DV), lambda b, h, s: (b * ns + s, h)),
                   pl.BlockSpec((None, None, RET_DK, RET_DV), lambda b, h, s: (b, h, 0, 0))),
        out_shape=(jax.ShapeDtypeStruct((batch * seq, RET_V), BF16),
                   jax.ShapeDtypeStruct((batch, RET_HEADS, RET_DK, RET_DV), F32)),
        scratch_shapes=[pltpu.VMEM((RET_DK, RET_DV), F32),
                        pltpu.VMEM((ts, RET_DK), F32), pltpu.VMEM((ts, RET_DK), F32),
                        pltpu.VMEM((ts, RET_DV), F32), pltpu.VMEM((ts, RET_DV), F32)],
        compiler_params=_cparams("parallel", "parallel", "arbitrary"),
        name="ret_prompt",
    )(u, wq, wk, wv, wg, cos, sin, dmat, qdec, kdec, cdec, ng)


def _proj_body(u_ref, w_ref, o_ref):
    o_ref[...] = _dot(u_ref[...], w_ref[...])


def _proj(u, w, *, tn=512):
    t, d = u.shape
    n = w.shape[1]
    return pl.pallas_call(
        _proj_body,
        grid=(n // tn,),
        in_specs=[pl.BlockSpec((t, d), lambda j: (0, 0)), pl.BlockSpec((d, tn), lambda j: (0, j))],
        out_specs=pl.BlockSpec((t, tn), lambda j: (0, j)),
        out_shape=jax.ShapeDtypeStruct((t, n), F32),
        compiler_params=_cparams("parallel"),
        name="sample_proj",
    )(u, w)


def _gate_body(u_ref, wga_ref, wgu_ref, bg_ref, o_ref):
    ga = _dot(u_ref[...], wga_ref[...])
    logit = _dot(ga.astype(BF16), wgu_ref[...]) + bg_ref[...]
    o_ref[...] = _log_sigmoid(logit) / GLA_GATE_TAU


def _gate(u, wga, wgu, bg):
    t = u.shape[0]
    return pl.pallas_call(
        _gate_body,
        out_shape=jax.ShapeDtypeStruct((t, GLA_QK), F32),
        compiler_params=pltpu.CompilerParams(vmem_limit_bytes=VMEM_LIMIT),
        name="sample_gate",
    )(u, wga, wgu, bg)


def _row_group_mask(rows, cols, bt, i):
    return (lax.broadcasted_iota(jnp.int32, (rows, cols), 0) % bt) == i


def _gla_decode_body(q_ref, k_ref, v_ref, g_ref, la_ref, ng_ref, s_ref, og_ref, so_ref, *, nt, bt):
    q = [q_ref[t] * (GLA_DK ** -0.5) for t in range(nt)]
    k = [k_ref[t] for t in range(nt)]
    v = [v_ref[t] for t in range(nt)]
    b = [la_ref[0]]
    for t in range(1, nt):
        b.append(b[-1] + la_ref[t])
    b_mid, b_last = b[nt // 2], b[nt - 1]
    q_rel = [q[t] * jnp.exp(b[t] - b_mid) for t in range(nt)]
    k_rel = [k[t] * jnp.exp(b_mid - b[t]) for t in range(nt)]
    intra = []
    for t in range(nt):
        acc = None
        for s in range(t + 1):
            term = jnp.sum(q_rel[t] * k_rel[s], axis=-1, keepdims=True) * v[s]
            acc = term if acc is None else acc + term
        intra.append(acc)
    q_dec = jnp.concatenate([q[t] * jnp.exp(b[t]) for t in range(nt)], axis=0).astype(BF16)
    k_end = jnp.concatenate([k[t] * jnp.exp(b_last - b[t]) for t in range(nt)], axis=0)
    v_all = jnp.concatenate(v, axis=0).astype(BF16)
    decay = jnp.exp(b_last)
    inter = jnp.zeros((nt * bt, GLA_DV), F32)
    for i in range(bt):
        st = s_ref[i]
        inter = inter + jnp.where(_row_group_mask(nt * bt, GLA_DV, bt, i), _dot(q_dec, st.astype(BF16)), 0.0)
        k_i = jnp.where(_row_group_mask(nt * bt, GLA_DK, bt, i), k_end, 0.0).astype(BF16)
        upd = _dot_ta(k_i, v_all)
        dcol = _lane_replicated_column(decay[i:i + 1, :])
        for jj in range(GLA_DV // LANES):
            sl = slice(jj * LANES, (jj + 1) * LANES)
            so_ref[i, :, sl] = dcol * st[:, sl] + upd[:, sl]
    for t in range(nt):
        o = intra[t] + inter[t * bt:(t + 1) * bt, :]
        og_ref[t] = _rms(o, ng_ref[...]) * _silu(g_ref[t])


def _gla_decode(pg, la, state, ng, nt, nb):
    bt = DEC_BT
    kq, kv = GLA_QK // GLA_DK, GLA_QK // GLA_DV
    return pl.pallas_call(
        functools.partial(_gla_decode_body, nt=nt, bt=bt),
        grid=(GLA_HEADS, nb // bt),
        in_specs=[pl.BlockSpec((nt, bt, GLA_DK), lambda h, i: (0, i, h)),
                  pl.BlockSpec((nt, bt, GLA_DK), lambda h, i: (0, i, kq + h)),
                  pl.BlockSpec((nt, bt, GLA_DV), lambda h, i: (0, i, 2 * kv + h)),
                  pl.BlockSpec((nt, bt, GLA_DV), lambda h, i: (0, i, 2 * kv + GLA_HEADS + h)),
                  pl.BlockSpec((nt, bt, GLA_DK), lambda h, i: (0, i, h)),
                  pl.BlockSpec((None, 1, GLA_DV), lambda h, i: (h, 0, 0)),
                  pl.BlockSpec((bt, None, GLA_DK, GLA_DV), lambda h, i: (i, h, 0, 0))],
        out_specs=(pl.BlockSpec((nt, bt, GLA_DV), lambda h, i: (0, i, h)),
                   pl.BlockSpec((bt, None, GLA_DK, GLA_DV), lambda h, i: (i, h, 0, 0))),
        out_shape=(jax.ShapeDtypeStruct((nt, nb, GLA_V), F32),
                   jax.ShapeDtypeStruct(state.shape, F32)),
        compiler_params=_cparams("parallel", "parallel"),
        name="gla_decode",
    )(pg, pg, pg, pg, la, ng, state)


def _ret_decode_body(pw_ref, q_ref, k_ref, v_ref, g_ref, cos_ref, sin_ref, ng_ref, s_ref, or_ref, so_ref, *, nt, bt):
    h = pl.program_id(0)
    pw = [pw_ref[h, n] for n in range(nt + 1)]
    q = [_rope(q_ref[t], cos_ref[t:t + 1, :], sin_ref[t:t + 1, :]) for t in range(nt)]
    k = [_rope(k_ref[t], cos_ref[t:t + 1, :], sin_ref[t:t + 1, :]) * (RET_DK ** -0.5) for t in range(nt)]
    v = [v_ref[t] for t in range(nt)]
    intra = []
    for t in range(nt):
        acc = None
        for s in range(t + 1):
            term = (jnp.sum(q[t] * k[s], axis=-1, keepdims=True) * pw[t - s]) * v[s]
            acc = term if acc is None else acc + term
        intra.append(acc)
    q_dec = jnp.concatenate([q[t] * pw[t + 1] for t in range(nt)], axis=0).astype(BF16)
    k_end = jnp.concatenate([k[t] * pw[nt - 1 - t] for t in range(nt)], axis=0)
    v_all = jnp.concatenate(v, axis=0).astype(BF16)
    inter = jnp.zeros((nt * bt, RET_DV), F32)
    for i in range(bt):
        st = s_ref[i]
        inter = inter + jnp.where(_row_group_mask(nt * bt, RET_DV, bt, i), _dot(q_dec, st.astype(BF16)), 0.0)
        k_i = jnp.where(_row_group_mask(nt * bt, RET_DK, bt, i), k_end, 0.0).astype(BF16)
        so_ref[i] = pw[nt] * st + _dot_ta(k_i, v_all)
    for t in range(nt):
        o = intra[t] + inter[t * bt:(t + 1) * bt, :]
        or_ref[t] = _rms(o, ng_ref[...]) * _silu(g_ref[t])


def _ret_decode(pr, state, ng, nt, nb):
    bt = DEC_BT
    kq, kv = RET_QK // RET_DK, RET_QK // RET_DV
    cos, sin = _rope_tables(PAST_LEN + jnp.arange(nt, dtype=F32))
    pw = jnp.exp(_ret_log_gamma()[:, None] * jnp.arange(nt + 1, dtype=F32)[None, :])
    return pl.pallas_call(
        functools.partial(_ret_decode_body, nt=nt, bt=bt),
        grid=(RET_HEADS, nb // bt),
        in_specs=[pl.BlockSpec(memory_space=pltpu.SMEM),
                  pl.BlockSpec((nt, bt, RET_DK), lambda h, i: (0, i, h)),
                  pl.BlockSpec((nt, bt, RET_DK), lambda h, i: (0, i, kq + h)),
                  pl.BlockSpec((nt, bt, RET_DV), lambda h, i: (0, i, 2 * kv + h)),
                  pl.BlockSpec((nt, bt, RET_DV), lambda h, i: (0, i, 2 * kv + RET_HEADS + h)),
                  pl.BlockSpec((nt, RET_DK // 2), lambda h, i: (0, 0)),
                  pl.BlockSpec((nt, RET_DK // 2), lambda h, i: (0, 0)),
                  pl.BlockSpec((None, 1, RET_DV), lambda h, i: (h, 0, 0)),
                  pl.BlockSpec((bt, None, RET_DK, RET_DV), lambda h, i: (i, h, 0, 0))],
        out_specs=(pl.BlockSpec((nt, bt, RET_DV), lambda h, i: (0, i, h)),
                   pl.BlockSpec((bt, None, RET_DK, RET_DV), lambda h, i: (i, h, 0, 0))),
        out_shape=(jax.ShapeDtypeStruct((nt, nb, RET_V), F32),
                   jax.ShapeDtypeStruct(state.shape, F32)),
        compiler_params=_cparams("parallel", "parallel"),
        name="ret_decode",
    )(pw, pr, pr, pr, pr, cos, sin, ng, state)


def _split_w_in(w_in):
    o = 0
    cuts = {}
    for name, width in (("gq", GLA_QK), ("gk", GLA_QK), ("gv", GLA_V), ("gg", GLA_V), ("ga", GLA_GATE_RANK),
                        ("rq", RET_QK), ("rk", RET_QK), ("rv", RET_V), ("rg", RET_V),
                        ("za", D_MODEL), ("zb", D_MODEL)):
        cuts[name] = (o, o + width)
        o += width
    piece = lambda a, b: w_in[:, cuts[a][0]:cuts[b][1]].astype(BF16)
    return {"gla": piece("gq", "gg"), "ga": piece("ga", "ga"), "ret": piece("rq", "rg"),
            "za": piece("za", "za"), "zb": piece("zb", "zb"),
            "gq": piece("gq", "gq"), "gk": piece("gk", "gk"), "gv": piece("gv", "gv"), "gg": piece("gg", "gg"),
            "rq": piece("rq", "rq"), "rk": piece("rk", "rk"), "rv": piece("rv", "rv"), "rg": piece("rg", "rg")}


def kernel(x_prompt, x_sample, state_gla, state_ret, ffn1_norm, ffn1_w1, ffn1_w3, ffn1_w2, mix_norm, w_in, w_gate_up, b_gate, gla_norm, w_gla_o, ret_norm, w_ret_o, w_out, ffn2_norm, ffn2_w1, ffn2_w3, ffn2_w2, final_norm):
    depth = w_in.shape[0]
    batch, seq, d = x_prompt.shape
    nb, nt, _ = x_sample.shape

    hp = x_prompt.reshape(batch * seq, d)
    hs = x_sample.transpose(1, 0, 2).reshape(nt * nb, d)
    gla_p, ret_p, gla_s, ret_s = [], [], [], []
    for l in range(depth):
        last = l == depth - 1
        w = _split_w_in(w_in[l])
        f1 = (ffn1_norm[l], ffn1_w1[l].astype(BF16), ffn1_w3[l].astype(BF16), ffn1_w2[l].astype(BF16), mix_norm[l])
        f2 = (ffn2_norm[l], ffn2_w1[l].astype(BF16), ffn2_w3[l].astype(BF16), ffn2_w2[l].astype(BF16))
        wgu = w_gate_up[l].astype(BF16)
        bg = b_gate[l].reshape(1, GLA_QK)
        ng = gla_norm[l].reshape(GLA_HEADS, 1, GLA_DV)
        nr = ret_norm[l].reshape(RET_HEADS, 1, RET_DV)
        wgo, wro, wout = w_gla_o[l].astype(BF16), w_ret_o[l].astype(BF16), w_out[l].astype(BF16)

        hp, up = _ffn(hp, *f1, final=False)
        og, sg = _gla_prompt(up, batch, seq, w["gq"], w["gk"], w["gv"], w["gg"], w["ga"], wgu, bg, ng)
        orr, sr = _ret_prompt(up, batch, seq, w["rq"], w["rk"], w["rv"], w["rg"], nr)
        hp = _merge(hp, up, og, orr, w["za"], w["zb"], wgo, wro, wout)
        gla_p.append(sg)
        ret_p.append(sr)

        hs, us = _ffn(hs, *f1, final=False)
        pg = _proj(us, w["gla"]).reshape(nt, nb, -1)
        pr = _proj(us, w["ret"]).reshape(nt, nb, -1)
        la = _gate(us, w["ga"], wgu, bg).reshape(nt, nb, GLA_QK)
        og_s, sg_s = _gla_decode(pg, la, state_gla[l], ng, nt, nb)
        or_s, sr_s = _ret_decode(pr, state_ret[l], nr, nt, nb)
        hs = _merge(hs, us, og_s.reshape(nt * nb, GLA_V), or_s.reshape(nt * nb, RET_V),
                    w["za"], w["zb"], wgo, wro, wout)
        gla_s.append(sg_s)
        ret_s.append(sr_s)

        if last:
            hp = _ffn(hp, *f2, final_norm, final=True)
            hs = _ffn(hs, *f2, final_norm, final=True)
        else:
            hp = _ffn(hp, *f2, ffn1_norm[l + 1], final=False)[0]
            hs = _ffn(hs, *f2, ffn1_norm[l + 1], final=False)[0]

    y_prompt = hp.reshape(batch, seq, d)
    y_sample = hs.reshape(nt, nb, d).transpose(1, 0, 2)
    return (y_prompt, y_sample, jnp.stack(gla_p), jnp.stack(ret_p), jnp.stack(gla_s), jnp.stack(ret_s))
```

```python
import functools

import jax
import jax.numpy as jnp
from jax import lax
from jax.experimental import pallas as pl
from jax.experimental.pallas import tpu as pltpu

F32, BF16 = jnp.float32, jnp.bfloat16

D_MODEL = 2048
PAST_LEN = 16384
GLA_HEADS = 4
GLA_DK = D_MODEL // (2 * GLA_HEADS)
GLA_DV = D_MODEL // GLA_HEADS
GLA_GATE_RANK = 16
GLA_GATE_TAU = 16.0
RET_HEADS = 8
RET_DK = D_MODEL // RET_HEADS
RET_DV = 2 * D_MODEL // RET_HEADS
ROPE_BASE = 10000.0
EPS = 1e-6
GLA_QK = GLA_HEADS * GLA_DK
GLA_V = GLA_HEADS * GLA_DV
RET_QK = RET_HEADS * RET_DK
RET_V = RET_HEADS * RET_DV

WA_Q, WA_K, WA_V, WA_G = 0, GLA_QK, 2 * GLA_QK, 2 * GLA_QK + GLA_V
WA_WIDTH = 2 * GLA_QK + 2 * GLA_V
WB_Q, WB_K, WB_V, WB_G = 0, RET_QK, 2 * RET_QK, 2 * RET_QK + RET_V
WB_ZA = 2 * RET_QK + 2 * RET_V
WB_ZB = WB_ZA + D_MODEL
WB_START = WA_WIDTH + GLA_GATE_RANK

LANES = 128
VMEM_LIMIT = 58 * 2**20

GLA_CHUNK = 64
RET_CHUNK = 256
SEQ_TILE = 512
DEC_BT = 16


def _cparams(*sem):
    return pltpu.CompilerParams(dimension_semantics=sem, vmem_limit_bytes=VMEM_LIMIT)


def _dot(a, b):
    return jnp.dot(a, b, preferred_element_type=F32)


def _dot_tb(a, b):
    return lax.dot_general(a, b, (((1,), (1,)), ((), ())), preferred_element_type=F32)


def _dot_ta(a, b):
    return lax.dot_general(a, b, (((0,), (0,)), ((), ())), preferred_element_type=F32)


def _rms(x, g):
    return x * lax.rsqrt(jnp.mean(x * x, axis=-1, keepdims=True) + EPS) * g


def _sigmoid(x):
    return 1.0 / (1.0 + jnp.exp(-x))


def _silu(x):
    return x * _sigmoid(x)


def _log_sigmoid(x):
    return jnp.minimum(x, 0.0) - jnp.log1p(jnp.exp(-jnp.abs(x)))


def _lane_replicated_column(row):
    return jnp.broadcast_to(row, (LANES, row.shape[-1])).T


def _ffn_body(x_ref, g_ref, w1_ref, w3_ref, w2_ref, g2_ref, *refs, tm, rows, final):
    if final:
        y_ref, u_scr = refs
        acc_ref = y_ref
    else:
        h_ref, un_ref, u_scr = refs
        acc_ref = h_ref
    j = pl.program_id(1)

    @pl.when(j == 0)
    def _():
        for r in range(0, tm, rows):
            u_scr[r:r + rows, :] = _rms(x_ref[r:r + rows, :], g_ref[...]).astype(BF16)
            acc_ref[r:r + rows, :] = jnp.zeros((rows, acc_ref.shape[1]), F32)

    u = u_scr[...]
    a = _dot(u, w1_ref[...])
    b = _dot(u, w3_ref[...])
    acc_ref[...] += _dot((_silu(a) * b).astype(BF16), w2_ref[...])

    @pl.when(j == pl.num_programs(1) - 1)
    def _():
        for r in range(0, tm, rows):
            h = x_ref[r:r + rows, :] + 0.5 * acc_ref[r:r + rows, :]
            if final:
                y_ref[r:r + rows, :] = _rms(h, g2_ref[...])
            else:
                h_ref[r:r + rows, :] = h
                un_ref[r:r + rows, :] = _rms(h, g2_ref[...]).astype(BF16)


def _ffn(x, g, w1, w3, w2, g2, *, final, tm=512, tf=512):
    t, d = x.shape
    dff = w1.shape[1]
    tm = min(tm, t)
    grid = (t // tm, dff // tf)
    row = pl.BlockSpec((tm, d), lambda i, j: (i, 0))
    vec = pl.BlockSpec((1, d), lambda i, j: (0, 0))
    if final:
        out_shape = jax.ShapeDtypeStruct((t, d), F32)
        out_specs = row
    else:
        out_shape = (jax.ShapeDtypeStruct((t, d), F32), jax.ShapeDtypeStruct((t, d), BF16))
        out_specs = (row, row)
    return pl.pallas_call(
        functools.partial(_ffn_body, tm=tm, rows=min(256, tm), final=final),
        grid=grid,
        in_specs=[row, vec,
                  pl.BlockSpec((d, tf), lambda i, j: (0, j)),
                  pl.BlockSpec((d, tf), lambda i, j: (0, j)),
                  pl.BlockSpec((tf, d), lambda i, j: (j, 0)),
                  vec],
        out_specs=out_specs,
        out_shape=out_shape,
        scratch_shapes=[pltpu.VMEM((tm, d), BF16)],
        compiler_params=_cparams("parallel", "arbitrary"),
        name="ffn_final" if final else "ffn_mid",
    )(x, g.reshape(1, d), w1, w3, w2, g2.reshape(1, d))


def _merge_body(u_ref, og_ref, or_ref, wza_ref, wzb_ref, wgo_ref, wro_ref, m_ref):
    u = u_ref[...]
    za = _dot(u, wza_ref[...])
    zb = _dot(u, wzb_ref[...])
    branch_a = _dot(og_ref[...].astype(BF16), wgo_ref[...])
    branch_b = _dot(or_ref[...].astype(BF16), wro_ref[...])
    m_ref[...] = (_sigmoid(za) * branch_a + _sigmoid(zb) * branch_b).astype(BF16)


def _out_body(h_ref, m_ref, wout_ref, o_ref):
    o_ref[...] = h_ref[...] + _dot(m_ref[...], wout_ref[...])


def _merge(h, u, og, orr, w_b, wgo, wro, wout, *, tm=512, tn=512):
    t, d = h.shape
    tm = min(tm, t)
    za0, zb0 = WB_ZA // tn, WB_ZB // tn
    merged = pl.pallas_call(
        _merge_body,
        grid=(t // tm, d // tn),
        in_specs=[pl.BlockSpec((tm, d), lambda i, j: (i, 0)),
                  pl.BlockSpec((tm, GLA_V), lambda i, j: (i, 0)),
                  pl.BlockSpec((tm, RET_V), lambda i, j: (i, 0)),
                  pl.BlockSpec((d, tn), lambda i, j: (0, za0 + j)),
                  pl.BlockSpec((d, tn), lambda i, j: (0, zb0 + j)),
                  pl.BlockSpec((GLA_V, tn), lambda i, j: (0, j)),
                  pl.BlockSpec((RET_V, tn), lambda i, j: (0, j))],
        out_specs=pl.BlockSpec((tm, tn), lambda i, j: (i, j)),
        out_shape=jax.ShapeDtypeStruct((t, d), BF16),
        compiler_params=_cparams("parallel", "parallel"),
        name="merge_gate",
    )(u, og, orr, w_b, w_b, wgo, wro)
    return pl.pallas_call(
        _out_body,
        grid=(t // tm,),
        in_specs=[pl.BlockSpec((tm, d), lambda i: (i, 0)),
                  pl.BlockSpec((tm, d), lambda i: (i, 0)),
                  pl.BlockSpec((d, d), lambda i: (0, 0))],
        out_specs=pl.BlockSpec((tm, d), lambda i: (i, 0)),
        out_shape=jax.ShapeDtypeStruct((t, d), F32),
        compiler_params=_cparams("parallel"),
        name="out_proj",
    )(h, merged, wout)


def _gla_prompt_body(u_ref, wq_ref, wk_ref, wv_ref, wg_ref, wga_ref, wgu_ref, bg_ref, ng_ref,
                     og_ref, st_ref, s_scr, *, ts, c):
    s = pl.program_id(2)

    @pl.when(s == 0)
    def _():
        s_scr[...] = jnp.zeros_like(s_scr)

    u = u_ref[...]
    ga = _dot(u, wga_ref[...])
    logit = _dot(ga.astype(BF16), wgu_ref[...]) + bg_ref[...]
    la_all = _log_sigmoid(logit) / GLA_GATE_TAU
    la_hi = la_all.astype(BF16)
    la_lo = (la_all - la_hi.astype(F32)).astype(BF16)
    q_all = _dot(u, wq_ref[...]) * (GLA_DK ** -0.5)
    k_all = _dot(u, wk_ref[...])
    v_all = _dot(u, wv_ref[...]).astype(BF16)
    g_all = _dot(u, wg_ref[...])

    rr = lax.broadcasted_iota(jnp.int32, (c, c), 0)
    cc = lax.broadcasted_iota(jnp.int32, (c, c), 1)
    causal = rr >= cc
    tril = causal.astype(BF16)
    mid = c // 2
    rows = [slice(ci * c, (ci + 1) * c) for ci in range(ts // c)]

    b = [_dot(tril, la_hi[r]) + _dot(tril, la_lo[r]) for r in rows]
    q_rel, k_rel, q_dec, k_end, decay = [], [], [], [], []
    for r, bc in zip(rows, b):
        q, k = q_all[r], k_all[r]
        b_mid, b_last = bc[mid:mid + 1, :], bc[c - 1:c, :]
        q_rel.append((q * jnp.exp(bc - b_mid)).astype(BF16))
        k_rel.append((k * jnp.exp(b_mid - bc)).astype(BF16))
        q_dec.append((q * jnp.exp(bc)).astype(BF16))
        k_end.append((k * jnp.exp(b_last - bc)).astype(BF16))
        dcol = _lane_replicated_column(jnp.exp(b_last))
        decay.append(jnp.concatenate([dcol] * (GLA_DV // LANES), axis=1))
    att = [jnp.where(causal, _dot_tb(qr, kr), 0.0).astype(BF16) for qr, kr in zip(q_rel, k_rel)]
    o_intra = [_dot(a, v_all[r]) for a, r in zip(att, rows)]
    upd = [_dot_ta(ke, v_all[r]) for ke, r in zip(k_end, rows)]
    st = s_scr[...]
    for ci, r in enumerate(rows):
        o = o_intra[ci] + _dot(q_dec[ci], st.astype(BF16))
        og_ref[r, :] = (_rms(o, ng_ref[...]) * _silu(g_all[r])).astype(BF16)
        st = decay[ci] * st + upd[ci]
    s_scr[...] = st

    @pl.when(s == pl.num_programs(2) - 1)
    def _():
        st_ref[...] = st


def _gla_prompt(u, batch, seq, w_a, wga, wgu, bg, ng):
    ts, c = SEQ_TILE, GLA_CHUNK
    ns = seq // ts
    d = u.shape[1]
    kq, kk, kv, kg = WA_Q // GLA_DK, WA_K // GLA_DK, WA_V // GLA_DV, WA_G // GLA_DV
    return pl.pallas_call(
        functools.partial(_gla_prompt_body, ts=ts, c=c),
        grid=(batch, GLA_HEADS, ns),
        in_specs=[pl.BlockSpec((ts, d), lambda b, h, s: (b * ns + s, 0)),
                  pl.BlockSpec((d, GLA_DK), lambda b, h, s: (0, kq + h)),
                  pl.BlockSpec((d, GLA_DK), lambda b, h, s: (0, kk + h)),
                  pl.BlockSpec((d, GLA_DV), lambda b, h, s: (0, kv + h)),
                  pl.BlockSpec((d, GLA_DV), lambda b, h, s: (0, kg + h)),
                  pl.BlockSpec((d, GLA_GATE_RANK), lambda b, h, s: (0, 0)),
                  pl.BlockSpec((GLA_GATE_RANK, GLA_DK), lambda b, h, s: (0, h)),
                  pl.BlockSpec((1, GLA_DK), lambda b, h, s: (0, h)),
                  pl.BlockSpec((None, 1, GLA_DV), lambda b, h, s: (h, 0, 0))],
        out_specs=(pl.BlockSpec((ts, GLA_DV), lambda b, h, s: (b * ns + s, h)),
                   pl.BlockSpec((None, None, GLA_DK, GLA_DV), lambda b, h, s: (b, h, 0, 0))),
        out_shape=(jax.ShapeDtypeStruct((batch * seq, GLA_V), BF16),
                   jax.ShapeDtypeStruct((batch, GLA_HEADS, GLA_DK, GLA_DV), F32)),
        scratch_shapes=[pltpu.VMEM((GLA_DK, GLA_DV), F32)],
        compiler_params=_cparams("parallel", "parallel", "arbitrary"),
        name="gla_prompt",
    )(u, w_a, w_a, w_a, w_a, wga, wgu, bg, ng)


def _rope(x, cos, sin):
    half = x.shape[-1] // 2
    x1, x2 = x[:, :half], x[:, half:]
    return jnp.concatenate([x1 * cos - x2 * sin, x2 * cos + x1 * sin], axis=-1)


def _ret_prompt_body(u_ref, wq_ref, wk_ref, wv_ref, wg_ref, cos_ref, sin_ref, dm_ref, qd_ref, kd_ref, cd_ref,
                     ng_ref, or_ref, st_ref, s_scr, *, ts, c):
    s = pl.program_id(2)

    @pl.when(s == 0)
    def _():
        s_scr[...] = jnp.zeros_like(s_scr)

    u = u_ref[...]
    cos, sin = cos_ref[...], sin_ref[...]
    q_all = _rope(_dot(u, wq_ref[...]), cos, sin)
    k_all = _rope(_dot(u, wk_ref[...]), cos, sin) * (RET_DK ** -0.5)
    v_all = _dot(u, wv_ref[...]).astype(BF16)
    g_all = _dot(u, wg_ref[...])

    rows = [slice(ci * c, (ci + 1) * c) for ci in range(ts // c)]
    att = [(_dot_tb(q_all[r].astype(BF16), k_all[r].astype(BF16)) * dm_ref[...]).astype(BF16) for r in rows]
    o_intra = [_dot(a, v_all[r]) for a, r in zip(att, rows)]
    upd = [_dot_ta((k_all[r] * kd_ref[...]).astype(BF16), v_all[r]) for r in rows]
    q_dec = [(q_all[r] * qd_ref[...]).astype(BF16) for r in rows]
    st = s_scr[...]
    for ci, r in enumerate(rows):
        o = o_intra[ci] + _dot(q_dec[ci], st.astype(BF16))
        or_ref[r, :] = (_rms(o, ng_ref[...]) * _silu(g_all[r])).astype(BF16)
        st = cd_ref[...] * st + upd[ci]
    s_scr[...] = st

    @pl.when(s == pl.num_programs(2) - 1)
    def _():
        st_ref[...] = st


def _ret_log_gamma():
    return jnp.log1p(-jnp.exp2(-5.0 - jnp.arange(RET_HEADS, dtype=F32)))


def _rope_tables(pos):
    half = RET_DK // 2
    freqs = ROPE_BASE ** (-jnp.arange(half, dtype=F32) / half)
    ang = pos[:, None] * freqs[None, :]
    return jnp.cos(ang), jnp.sin(ang)


def _ret_prompt(u, batch, seq, w_b, ng):
    ts, c = SEQ_TILE, RET_CHUNK
    ns = seq // ts
    d = u.shape[1]
    kq, kk, kv, kg = WB_Q // RET_DK, WB_K // RET_DK, WB_V // RET_DV, WB_G // RET_DV
    cos, sin = _rope_tables(jnp.arange(seq, dtype=F32))
    lg = _ret_log_gamma()
    idx = jnp.arange(c, dtype=F32)
    diff = idx[:, None] - idx[None, :]
    dmat = jnp.where(diff >= 0, jnp.exp(lg[:, None, None] * jnp.maximum(diff, 0.0)), 0.0)
    qdec = jnp.broadcast_to(jnp.exp(lg[:, None] * (idx + 1.0)[None, :])[:, :, None], (RET_HEADS, c, RET_DK))
    kdec = jnp.broadcast_to(jnp.exp(lg[:, None] * (c - 1.0 - idx)[None, :])[:, :, None], (RET_HEADS, c, RET_DK))
    cdec = jnp.broadcast_to(jnp.exp(lg * c)[:, None, None], (RET_HEADS, 1, RET_DV))
    return pl.pallas_call(
        functools.partial(_ret_prompt_body, ts=ts, c=c),
        grid=(batch, RET_HEADS, ns),
        in_specs=[pl.BlockSpec((ts, d), lambda b, h, s: (b * ns + s, 0)),
                  pl.BlockSpec((d, RET_DK), lambda b, h, s: (0, kq + h)),
                  pl.BlockSpec((d, RET_DK), lambda b, h, s: (0, kk + h)),
                  pl.BlockSpec((d, RET_DV), lambda b, h, s: (0, kv + h)),
                  pl.BlockSpec((d, RET_DV), lambda b, h, s: (0, kg + h)),
                  pl.BlockSpec((ts, RET_DK // 2), lambda b, h, s: (s, 0)),
                  pl.BlockSpec((ts, RET_DK // 2), lambda b, h, s: (s, 0)),
                  pl.BlockSpec((None, c, c), lambda b, h, s: (h, 0, 0)),
                  pl.BlockSpec((None, c, RET_DK), lambda b, h, s: (h, 0, 0)),
                  pl.BlockSpec((None, c, RET_DK), lambda b, h, s: (h, 0, 0)),
                  pl.BlockSpec((None, 1, RET_DV), lambda b, h, s: (h, 0, 0)),
                  pl.BlockSpec((None, 1, RET_DV), lambda b, h, s: (h, 0, 0))],
        out_specs=(pl.BlockSpec((ts, RET_DV), lambda b, h, s: (b * ns + s, h)),
                   pl.BlockSpec((None, None, RET_DK, RET_DV), lambda b, h, s: (b, h, 0, 0))),
        out_shape=(jax.ShapeDtypeStruct((batch * seq, RET_V), BF16),
                   jax.ShapeDtypeStruct((batch, RET_HEADS, RET_DK, RET_DV), F32)),
        scratch_shapes=[pltpu.VMEM((RET_DK, RET_DV), F32)],
        compiler_params=_cparams("parallel", "parallel", "arbitrary"),
        name="ret_prompt",
    )(u, w_b, w_b, w_b, w_b, cos, sin, dmat, qdec, kdec, cdec, ng)


def _proj_body(u_ref, w_ref, o_ref):
    o_ref[...] = _dot(u_ref[...], w_ref[...])


def _proj(u, w, n, *, tn=512):
    t, d = u.shape
    return pl.pallas_call(
        _proj_body,
        grid=(n // tn,),
        in_specs=[pl.BlockSpec((t, d), lambda j: (0, 0)), pl.BlockSpec((d, tn), lambda j: (0, j))],
        out_specs=pl.BlockSpec((t, tn), lambda j: (0, j)),
        out_shape=jax.ShapeDtypeStruct((t, n), F32),
        compiler_params=_cparams("parallel"),
        name="sample_proj",
    )(u, w)


def _gate_body(u_ref, wga_ref, wgu_ref, bg_ref, o_ref):
    ga = _dot(u_ref[...], wga_ref[...])
    logit = _dot(ga.astype(BF16), wgu_ref[...]) + bg_ref[...]
    o_ref[...] = _log_sigmoid(logit) / GLA_GATE_TAU


def _gate(u, wga, wgu, bg):
    t = u.shape[0]
    return pl.pallas_call(
        _gate_body,
        out_shape=jax.ShapeDtypeStruct((t, GLA_QK), F32),
        compiler_params=pltpu.CompilerParams(vmem_limit_bytes=VMEM_LIMIT),
        name="sample_gate",
    )(u, wga, wgu, bg)


def _row_group_mask(rows, cols, bt, i):
    return (lax.broadcasted_iota(jnp.int32, (rows, cols), 0) % bt) == i


def _gla_decode_body(q_ref, k_ref, v_ref, g_ref, la_ref, ng_ref, s_ref, og_ref, so_ref, *, nt, bt):
    q = [q_ref[t] * (GLA_DK ** -0.5) for t in range(nt)]
    k = [k_ref[t] for t in range(nt)]
    v = [v_ref[t] for t in range(nt)]
    b = [la_ref[0]]
    for t in range(1, nt):
        b.append(b[-1] + la_ref[t])
    b_mid, b_last = b[nt // 2], b[nt - 1]
    q_rel = [q[t] * jnp.exp(b[t] - b_mid) for t in range(nt)]
    k_rel = [k[t] * jnp.exp(b_mid - b[t]) for t in range(nt)]
    intra = []
    for t in range(nt):
        acc = None
        for s in range(t + 1):
            term = jnp.sum(q_rel[t] * k_rel[s], axis=-1, keepdims=True) * v[s]
            acc = term if acc is None else acc + term
        intra.append(acc)
    q_dec = jnp.concatenate([q[t] * jnp.exp(b[t]) for t in range(nt)], axis=0).astype(BF16)
    k_end = jnp.concatenate([k[t] * jnp.exp(b_last - b[t]) for t in range(nt)], axis=0)
    v_all = jnp.concatenate(v, axis=0).astype(BF16)
    decay = jnp.exp(b_last)
    inter = jnp.zeros((nt * bt, GLA_DV), F32)
    for i in range(bt):
        st = s_ref[i]
        inter = inter + jnp.where(_row_group_mask(nt * bt, GLA_DV, bt, i), _dot(q_dec, st.astype(BF16)), 0.0)
        k_i = jnp.where(_row_group_mask(nt * bt, GLA_DK, bt, i), k_end, 0.0).astype(BF16)
        upd = _dot_ta(k_i, v_all)
        dcol = _lane_replicated_column(decay[i:i + 1, :])
        for jj in range(GLA_DV // LANES):
            sl = slice(jj * LANES, (jj + 1) * LANES)
            so_ref[i, :, sl] = dcol * st[:, sl] + upd[:, sl]
    for t in range(nt):
        o = intra[t] + inter[t * bt:(t + 1) * bt, :]
        og_ref[t] = _rms(o, ng_ref[...]) * _silu(g_ref[t])


def _gla_decode(pg, la, state, ng, nt, nb):
    bt = DEC_BT
    kq, kk, kv, kg = WA_Q // GLA_DK, WA_K // GLA_DK, WA_V // GLA_DV, WA_G // GLA_DV
    return pl.pallas_call(
        functools.partial(_gla_decode_body, nt=nt, bt=bt),
        grid=(GLA_HEADS, nb // bt),
        in_specs=[pl.BlockSpec((nt, bt, GLA_DK), lambda h, i: (0, i, kq + h)),
                  pl.BlockSpec((nt, bt, GLA_DK), lambda h, i: (0, i, kk + h)),
                  pl.BlockSpec((nt, bt, GLA_DV), lambda h, i: (0, i, kv + h)),
                  pl.BlockSpec((nt, bt, GLA_DV), lambda h, i: (0, i, kg + h)),
                  pl.BlockSpec((nt, bt, GLA_DK), lambda h, i: (0, i, h)),
                  pl.BlockSpec((None, 1, GLA_DV), lambda h, i: (h, 0, 0)),
                  pl.BlockSpec((bt, None, GLA_DK, GLA_DV), lambda h, i: (i, h, 0, 0))],
        out_specs=(pl.BlockSpec((nt, bt, GLA_DV), lambda h, i: (0, i, h)),
                   pl.BlockSpec((bt, None, GLA_DK, GLA_DV), lambda h, i: (i, h, 0, 0))),
        out_shape=(jax.ShapeDtypeStruct((nt, nb, GLA_V), F32),
                   jax.ShapeDtypeStruct(state.shape, F32)),
        compiler_params=_cparams("parallel", "parallel"),
        name="gla_decode",
    )(pg, pg, pg, pg, la, ng, state)


def _ret_decode_body(pw_ref, q_ref, k_ref, v_ref, g_ref, cos_ref, sin_ref, ng_ref, s_ref, or_ref, so_ref, *, nt, bt):
    h = pl.program_id(0)
    pw = [pw_ref[h, n] for n in range(nt + 1)]
    q = [_rope(q_ref[t], cos_ref[t:t + 1, :], sin_ref[t:t + 1, :]) for t in range(nt)]
    k = [_rope(k_ref[t], cos_ref[t:t + 1, :], sin_ref[t:t + 1, :]) * (RET_DK ** -0.5) for t in range(nt)]
    v = [v_ref[t] for t in range(nt)]
    intra = []
    for t in range(nt):
        acc = None
        for s in range(t + 1):
            term = (jnp.sum(q[t] * k[s], axis=-1, keepdims=True) * pw[t - s]) * v[s]
            acc = term if acc is None else acc + term
        intra.append(acc)
    q_dec = jnp.concatenate([q[t] * pw[t + 1] for t in range(nt)], axis=0).astype(BF16)
    k_end = jnp.concatenate([k[t] * pw[nt - 1 - t] for t in range(nt)], axis=0)
    v_all = jnp.concatenate(v, axis=0).astype(BF16)
    inter = jnp.zeros((nt * bt, RET_DV), F32)
    for i in range(bt):
        st = s_ref[i]
        inter = inter + jnp.where(_row_group_mask(nt * bt, RET_DV, bt, i), _dot(q_dec, st.astype(BF16)), 0.0)
        k_i = jnp.where(_row_group_mask(nt * bt, RET_DK, bt, i), k_end, 0.0).astype(BF16)
        so_ref[i] = pw[nt] * st + _dot_ta(k_i, v_all)
    for t in range(nt):
        o = intra[t] + inter[t * bt:(t + 1) * bt, :]
        or_ref[t] = _rms(o, ng_ref[...]) * _silu(g_ref[t])


def _ret_decode(pr, state, ng, nt, nb):
    bt = DEC_BT
    kq, kk, kv, kg = WB_Q // RET_DK, WB_K // RET_DK, WB_V // RET_DV, WB_G // RET_DV
    cos, sin = _rope_tables(PAST_LEN + jnp.arange(nt, dtype=F32))
    pw = jnp.exp(_ret_log_gamma()[:, None] * jnp.arange(nt + 1, dtype=F32)[None, :])
    return pl.pallas_call(
        functools.partial(_ret_decode_body, nt=nt, bt=bt),
        grid=(RET_HEADS, nb // bt),
        in_specs=[pl.BlockSpec(memory_space=pltpu.SMEM),
                  pl.BlockSpec((nt, bt, RET_DK), lambda h, i: (0, i, kq + h)),
                  pl.BlockSpec((nt, bt, RET_DK), lambda h, i: (0, i, kk + h)),
                  pl.BlockSpec((nt, bt, RET_DV), lambda h, i: (0, i, kv + h)),
                  pl.BlockSpec((nt, bt, RET_DV), lambda h, i: (0, i, kg + h)),
                  pl.BlockSpec((nt, RET_DK // 2), lambda h, i: (0, 0)),
                  pl.BlockSpec((nt, RET_DK // 2), lambda h, i: (0, 0)),
                  pl.BlockSpec((None, 1, RET_DV), lambda h, i: (h, 0, 0)),
                  pl.BlockSpec((bt, None, RET_DK, RET_DV), lambda h, i: (i, h, 0, 0))],
        out_specs=(pl.BlockSpec((nt, bt, RET_DV), lambda h, i: (0, i, h)),
                   pl.BlockSpec((bt, None, RET_DK, RET_DV), lambda h, i: (i, h, 0, 0))),
        out_shape=(jax.ShapeDtypeStruct((nt, nb, RET_V), F32),
                   jax.ShapeDtypeStruct(state.shape, F32)),
        compiler_params=_cparams("parallel", "parallel"),
        name="ret_decode",
    )(pw, pr, pr, pr, pr, cos, sin, ng, state)


def _split_w_in(w_in):
    return (w_in[:, :WA_WIDTH].astype(BF16), w_in[:, WA_WIDTH:WB_START].astype(BF16), w_in[:, WB_START:].astype(BF16))


def kernel(x_prompt, x_sample, state_gla, state_ret, ffn1_norm, ffn1_w1, ffn1_w3, ffn1_w2, mix_norm, w_in, w_gate_up, b_gate, gla_norm, w_gla_o, ret_norm, w_ret_o, w_out, ffn2_norm, ffn2_w1, ffn2_w3, ffn2_w2, final_norm):
    depth = w_in.shape[0]
    batch, seq, d = x_prompt.shape
    nb, nt, _ = x_sample.shape

    hp = x_prompt.reshape(batch * seq, d)
    hs = x_sample.transpose(1, 0, 2).reshape(nt * nb, d)
    gla_p, ret_p, gla_s, ret_s = [], [], [], []
    for l in range(depth):
        last = l == depth - 1
        w_a, w_ga, w_b = _split_w_in(w_in[l])
        f1 = (ffn1_norm[l], ffn1_w1[l].astype(BF16), ffn1_w3[l].astype(BF16), ffn1_w2[l].astype(BF16), mix_norm[l])
        f2 = (ffn2_norm[l], ffn2_w1[l].astype(BF16), ffn2_w3[l].astype(BF16), ffn2_w2[l].astype(BF16))
        wgu = w_gate_up[l].astype(BF16)
        bg = b_gate[l].reshape(1, GLA_QK)
        ng = gla_norm[l].reshape(GLA_HEADS, 1, GLA_DV)
        nr = ret_norm[l].reshape(RET_HEADS, 1, RET_DV)
        wgo, wro, wout = w_gla_o[l].astype(BF16), w_ret_o[l].astype(BF16), w_out[l].astype(BF16)

        hp, up = _ffn(hp, *f1, final=False)
        og, sg = _gla_prompt(up, batch, seq, w_a, w_ga, wgu, bg, ng)
        orr, sr = _ret_prompt(up, batch, seq, w_b, nr)
        hp = _merge(hp, up, og, orr, w_b, wgo, wro, wout)
        gla_p.append(sg)
        ret_p.append(sr)

        hs, us = _ffn(hs, *f1, final=False)
        pg = _proj(us, w_a, WA_WIDTH).reshape(nt, nb, -1)
        pr = _proj(us, w_b, WB_ZA).reshape(nt, nb, -1)
        la = _gate(us, w_ga, wgu, bg).reshape(nt, nb, GLA_QK)
        og_s, sg_s = _gla_decode(pg, la, state_gla[l], ng, nt, nb)
        or_s, sr_s = _ret_decode(pr, state_ret[l], nr, nt, nb)
        hs = _merge(hs, us, og_s.reshape(nt * nb, GLA_V), or_s.reshape(nt * nb, RET_V), w_b, wgo, wro, wout)
        gla_s.append(sg_s)
        ret_s.append(sr_s)

        if last:
            hp = _ffn(hp, *f2, final_norm, final=True)
            hs = _ffn(hs, *f2, final_norm, final=True)
        else:
            hp = _ffn(hp, *f2, ffn1_norm[l + 1], final=False)[0]
            hs = _ffn(hs, *f2, ffn1_norm[l + 1], final=False)[0]

    y_prompt = hp.reshape(batch, seq, d)
    y_sample = hs.reshape(nt, nb, d).transpose(1, 0, 2)
    return (y_prompt, y_sample, jnp.stack(gla_p), jnp.stack(ret_p), jnp.stack(gla_s), jnp.stack(ret_s))
```

```python
import functools

import jax
import jax.numpy as jnp
from jax import lax
from jax.experimental import pallas as pl
from jax.experimental.pallas import tpu as pltpu

F32, BF16 = jnp.float32, jnp.bfloat16

D_MODEL = 2048
PAST_LEN = 16384
GLA_HEADS = 4
GLA_DK = D_MODEL // (2 * GLA_HEADS)
GLA_DV = D_MODEL // GLA_HEADS
GLA_GATE_RANK = 16
GLA_GATE_TAU = 16.0
RET_HEADS = 8
RET_DK = D_MODEL // RET_HEADS
RET_DV = 2 * D_MODEL // RET_HEADS
ROPE_BASE = 10000.0
EPS = 1e-6
GLA_QK = GLA_HEADS * GLA_DK
GLA_V = GLA_HEADS * GLA_DV
RET_QK = RET_HEADS * RET_DK
RET_V = RET_HEADS * RET_DV

WA_Q, WA_K, WA_V, WA_G = 0, GLA_QK, 2 * GLA_QK, 2 * GLA_QK + GLA_V
WA_WIDTH = 2 * GLA_QK + 2 * GLA_V
WB_Q, WB_K, WB_V, WB_G = 0, RET_QK, 2 * RET_QK, 2 * RET_QK + RET_V
WB_ZA = 2 * RET_QK + 2 * RET_V
WB_ZB = WB_ZA + D_MODEL
WB_START = WA_WIDTH + GLA_GATE_RANK

LANES = 128
VMEM_LIMIT = 58 * 2**20

GLA_CHUNK = 64
RET_CHUNK = 256
SEQ_TILE = 512
DEC_BT = 16


def _cparams(*sem):
    return pltpu.CompilerParams(dimension_semantics=sem, vmem_limit_bytes=VMEM_LIMIT)


def _dot(a, b):
    return jnp.dot(a, b, preferred_element_type=F32)


def _dot_tb(a, b):
    return lax.dot_general(a, b, (((1,), (1,)), ((), ())), preferred_element_type=F32)


def _dot_ta(a, b):
    return lax.dot_general(a, b, (((0,), (0,)), ((), ())), preferred_element_type=F32)


def _rms(x, g):
    return x * lax.rsqrt(jnp.mean(x * x, axis=-1, keepdims=True) + EPS) * g


def _sigmoid(x):
    return 1.0 / (1.0 + jnp.exp(-x))


def _silu(x):
    return x * _sigmoid(x)


def _log_sigmoid(x):
    return jnp.minimum(x, 0.0) - jnp.log1p(jnp.exp(-jnp.abs(x)))


def _lane_replicated_column(row):
    return jnp.broadcast_to(row, (LANES, row.shape[-1])).T


def _ffn_body(x_ref, g_ref, w1_ref, w3_ref, w2_ref, g2_ref, *refs, tm, rows, final, emit_w):
    refs = list(refs)
    u_scr = refs.pop()
    wb_refs = [refs.pop() for _ in range(3)][::-1] if emit_w else None
    if final:
        y_ref, = refs
        acc_ref = y_ref
    else:
        h_ref, un_ref = refs
        acc_ref = h_ref
    j = pl.program_id(1)
    w1, w3, w2 = w1_ref[...].astype(BF16), w3_ref[...].astype(BF16), w2_ref[...].astype(BF16)
    if emit_w:
        for ref, w in zip(wb_refs, (w1, w3, w2)):
            ref[...] = w

    @pl.when(j == 0)
    def _():
        for r in range(0, tm, rows):
            u_scr[r:r + rows, :] = _rms(x_ref[r:r + rows, :], g_ref[...]).astype(BF16)
            acc_ref[r:r + rows, :] = jnp.zeros((rows, acc_ref.shape[1]), F32)

    u = u_scr[...]
    a = _dot(u, w1)
    b = _dot(u, w3)
    acc_ref[...] += _dot((_silu(a) * b).astype(BF16), w2)

    @pl.when(j == pl.num_programs(1) - 1)
    def _():
        for r in range(0, tm, rows):
            h = x_ref[r:r + rows, :] + 0.5 * acc_ref[r:r + rows, :]
            if final:
                y_ref[r:r + rows, :] = _rms(h, g2_ref[...])
            else:
                h_ref[r:r + rows, :] = h
                un_ref[r:r + rows, :] = _rms(h, g2_ref[...]).astype(BF16)


def _ffn(x, g, w1, w3, w2, g2, *, final, tm=512):
    t, d = x.shape
    dff = w1.shape[1]
    tm = min(tm, t)
    emit_w = w1.dtype == F32
    assert not emit_w or t == tm, "weights are emitted by a single-token-tile call"
    tf = 256 if emit_w else 512
    grid = (t // tm, dff // tf)
    row = pl.BlockSpec((tm, d), lambda i, j: (i, 0))
    vec = pl.BlockSpec((1, d), lambda i, j: (0, 0))
    w_up = pl.BlockSpec((d, tf), lambda i, j: (0, j))
    w_dn = pl.BlockSpec((tf, d), lambda i, j: (j, 0))
    out_shape = [jax.ShapeDtypeStruct((t, d), F32)] + ([] if final else [jax.ShapeDtypeStruct((t, d), BF16)])
    out_specs = [row] * len(out_shape)
    if emit_w:
        out_shape += [jax.ShapeDtypeStruct(w.shape, BF16) for w in (w1, w3, w2)]
        out_specs += [w_up, w_up, w_dn]
    return pl.pallas_call(
        functools.partial(_ffn_body, tm=tm, rows=min(256, tm), final=final, emit_w=emit_w),
        grid=grid,
        in_specs=[row, vec, w_up, w_up, w_dn, vec],
        out_specs=out_specs,
        out_shape=out_shape,
        scratch_shapes=[pltpu.VMEM((tm, d), BF16)],
        compiler_params=_cparams("parallel", "arbitrary"),
        name=("ffn_final" if final else "ffn_mid") + ("_castw" if emit_w else ""),
    )(x, g.reshape(1, d), w1, w3, w2, g2.reshape(1, d))


def _shifted_cols(a_ref, b_ref, shift):
    width = a_ref.shape[1]
    return jnp.concatenate([a_ref[...], b_ref[...]], axis=1)[:, shift:shift + width]


def _window_specs(rows, tn, start, col_of):
    shift = start % LANES
    base = start - shift
    assert shift and base % tn == 0
    return (pl.BlockSpec((rows, tn), lambda *g: (0, base // tn + col_of(*g))),
            pl.BlockSpec((rows, LANES), lambda *g: (0, (base + tn * (col_of(*g) + 1)) // LANES))), shift


def _merge_body(u_ref, og_ref, or_ref, *refs, shift):
    u = u_ref[...]
    if shift is None:
        wza_ref, wzb_ref, wgo_ref, wro_ref, m_ref = refs
        wza, wzb, wgo, wro = wza_ref[...], wzb_ref[...], wgo_ref[...], wro_ref[...]
    else:
        za_a, za_b, zb_a, zb_b, wgo_ref, wro_ref, m_ref, wza_o, wzb_o, wgo_o, wro_o = refs
        wza = _shifted_cols(za_a, za_b, shift).astype(BF16)
        wzb = _shifted_cols(zb_a, zb_b, shift).astype(BF16)
        wgo, wro = wgo_ref[...].astype(BF16), wro_ref[...].astype(BF16)
        wza_o[...], wzb_o[...], wgo_o[...], wro_o[...] = wza, wzb, wgo, wro
    za = _dot(u, wza)
    zb = _dot(u, wzb)
    branch_a = _dot(og_ref[...], wgo)
    branch_b = _dot(or_ref[...], wro)
    m_ref[...] = (_sigmoid(za) * branch_a + _sigmoid(zb) * branch_b).astype(BF16)


def _merge_gate(u, og, orr, wz, wgo, wro, *, tm=512):
    t, d = u.shape
    tm = min(tm, t)
    emit_w = wgo.dtype == F32
    assert not emit_w or t == tm, "weights are emitted by a single-token-tile call"
    tn = 256 if emit_w else 512
    col = lambda i, j: j
    acts = [pl.BlockSpec((tm, d), lambda i, j: (i, 0)),
            pl.BlockSpec((tm, GLA_V), lambda i, j: (i, 0)),
            pl.BlockSpec((tm, RET_V), lambda i, j: (i, 0))]
    w_col = lambda rows: pl.BlockSpec((rows, tn), lambda i, j: (0, j))
    out_shape = [jax.ShapeDtypeStruct((t, d), BF16)]
    out_specs = [pl.BlockSpec((tm, tn), lambda i, j: (i, j))]
    if emit_w:
        (za_a, za_b), shift = _window_specs(d, tn, WB_START + WB_ZA, col)
        (zb_a, zb_b), _ = _window_specs(d, tn, WB_START + WB_ZB, col)
        w_specs, w_args = [za_a, za_b, zb_a, zb_b, w_col(GLA_V), w_col(RET_V)], (wz, wz, wz, wz, wgo, wro)
        out_shape += [jax.ShapeDtypeStruct((d, d), BF16), jax.ShapeDtypeStruct((d, d), BF16),
                      jax.ShapeDtypeStruct(wgo.shape, BF16), jax.ShapeDtypeStruct(wro.shape, BF16)]
        out_specs += [w_col(d), w_col(d), w_col(GLA_V), w_col(RET_V)]
    else:
        shift = None
        w_specs, w_args = [w_col(d), w_col(d), w_col(GLA_V), w_col(RET_V)], (*wz, wgo, wro)
    return pl.pallas_call(
        functools.partial(_merge_body, shift=shift),
        grid=(t // tm, d // tn),
        in_specs=acts + w_specs,
        out_specs=out_specs,
        out_shape=out_shape,
        compiler_params=_cparams("parallel", "parallel"),
        name="merge_gate" + ("_castw" if emit_w else ""),
    )(u, og, orr, *w_args)


def _out_body(h_ref, m_ref, wout_ref, o_ref, *wb_ref):
    wout = wout_ref[...].astype(BF16)
    if wb_ref:
        wb_ref[0][...] = wout
    o_ref[...] = h_ref[...] + _dot(m_ref[...], wout)


def _out_proj(h, merged, wout, *, tm=512):
    t, d = h.shape
    tm = min(tm, t)
    emit_w = wout.dtype == F32
    assert not emit_w or t == tm, "weights are emitted by a single-token-tile call"
    tn = 512 if emit_w else d
    out_shape = [jax.ShapeDtypeStruct((t, d), F32)] + ([jax.ShapeDtypeStruct((d, d), BF16)] if emit_w else [])
    out_specs = [pl.BlockSpec((tm, tn), lambda i, j: (i, j))] + ([pl.BlockSpec((d, tn), lambda i, j: (0, j))] if emit_w else [])
    return pl.pallas_call(
        _out_body,
        grid=(t // tm, d // tn),
        in_specs=[pl.BlockSpec((tm, tn), lambda i, j: (i, j)),
                  pl.BlockSpec((tm, d), lambda i, j: (i, 0)),
                  pl.BlockSpec((d, tn), lambda i, j: (0, j))],
        out_specs=out_specs,
        out_shape=out_shape,
        compiler_params=_cparams("parallel", "parallel"),
        name="out_proj" + ("_castw" if emit_w else ""),
    )(h, merged, wout)


def _gla_prompt_body(u_ref, wq_ref, wk_ref, wv_ref, wg_ref, wga_ref, wgu_ref, bg_ref, ng_ref,
                     og_ref, st_ref, s_scr, *, ts, c):
    s = pl.program_id(2)

    @pl.when(s == 0)
    def _():
        s_scr[...] = jnp.zeros_like(s_scr)

    u = u_ref[...]
    ga = _dot(u, wga_ref[...])
    logit = _dot(ga.astype(BF16), wgu_ref[...]) + bg_ref[...]
    la_all = _log_sigmoid(logit) / GLA_GATE_TAU
    la_hi = la_all.astype(BF16)
    la_lo = (la_all - la_hi.astype(F32)).astype(BF16)
    q_all = _dot(u, wq_ref[...]) * (GLA_DK ** -0.5)
    k_all = _dot(u, wk_ref[...])
    v_all = _dot(u, wv_ref[...]).astype(BF16)
    g_all = _dot(u, wg_ref[...])

    rr = lax.broadcasted_iota(jnp.int32, (c, c), 0)
    cc = lax.broadcasted_iota(jnp.int32, (c, c), 1)
    causal = rr >= cc
    tril = causal.astype(BF16)
    mid = c // 2
    rows = [slice(ci * c, (ci + 1) * c) for ci in range(ts // c)]

    b = [_dot(tril, la_hi[r]) + _dot(tril, la_lo[r]) for r in rows]
    q_rel, k_rel, q_dec, k_end, decay = [], [], [], [], []
    for r, bc in zip(rows, b):
        q, k = q_all[r], k_all[r]
        b_mid, b_last = bc[mid:mid + 1, :], bc[c - 1:c, :]
        q_rel.append((q * jnp.exp(bc - b_mid)).astype(BF16))
        k_rel.append((k * jnp.exp(b_mid - bc)).astype(BF16))
        q_dec.append((q * jnp.exp(bc)).astype(BF16))
        k_end.append((k * jnp.exp(b_last - bc)).astype(BF16))
        dcol = _lane_replicated_column(jnp.exp(b_last))
        decay.append(jnp.concatenate([dcol] * (GLA_DV // LANES), axis=1))
    att = [jnp.where(causal, _dot_tb(qr, kr), 0.0).astype(BF16) for qr, kr in zip(q_rel, k_rel)]
    o_intra = [_dot(a, v_all[r]) for a, r in zip(att, rows)]
    upd = [_dot_ta(ke, v_all[r]) for ke, r in zip(k_end, rows)]
    st = s_scr[...]
    for ci, r in enumerate(rows):
        o = o_intra[ci] + _dot(q_dec[ci], st.astype(BF16))
        og_ref[r, :] = (_rms(o, ng_ref[...]) * _silu(g_all[r])).astype(BF16)
        st = decay[ci] * st + upd[ci]
    s_scr[...] = st

    @pl.when(s == pl.num_programs(2) - 1)
    def _():
        st_ref[...] = st


def _gla_prompt(u, batch, seq, w_a, wga, wgu, bg, ng):
    ts, c = SEQ_TILE, GLA_CHUNK
    ns = seq // ts
    d = u.shape[1]
    kq, kk, kv, kg = WA_Q // GLA_DK, WA_K // GLA_DK, WA_V // GLA_DV, WA_G // GLA_DV
    return pl.pallas_call(
        functools.partial(_gla_prompt_body, ts=ts, c=c),
        grid=(batch, GLA_HEADS, ns),
        in_specs=[pl.BlockSpec((ts, d), lambda b, h, s: (b * ns + s, 0)),
                  pl.BlockSpec((d, GLA_DK), lambda b, h, s: (0, kq + h)),
                  pl.BlockSpec((d, GLA_DK), lambda b, h, s: (0, kk + h)),
                  pl.BlockSpec((d, GLA_DV), lambda b, h, s: (0, kv + h)),
                  pl.BlockSpec((d, GLA_DV), lambda b, h, s: (0, kg + h)),
                  pl.BlockSpec((d, GLA_GATE_RANK), lambda b, h, s: (0, 0)),
                  pl.BlockSpec((GLA_GATE_RANK, GLA_DK), lambda b, h, s: (0, h)),
                  pl.BlockSpec((1, GLA_DK), lambda b, h, s: (0, h)),
                  pl.BlockSpec((None, 1, GLA_DV), lambda b, h, s: (h, 0, 0))],
        out_specs=(pl.BlockSpec((ts, GLA_DV), lambda b, h, s: (b * ns + s, h)),
                   pl.BlockSpec((None, None, GLA_DK, GLA_DV), lambda b, h, s: (b, h, 0, 0))),
        out_shape=(jax.ShapeDtypeStruct((batch * seq, GLA_V), BF16),
                   jax.ShapeDtypeStruct((batch, GLA_HEADS, GLA_DK, GLA_DV), F32)),
        scratch_shapes=[pltpu.VMEM((GLA_DK, GLA_DV), F32)],
        compiler_params=_cparams("parallel", "parallel", "arbitrary"),
        name="gla_prompt",
    )(u, w_a, w_a, w_a, w_a, wga, wgu, bg, ng)


def _rope(x, cos, sin):
    half = x.shape[-1] // 2
    x1, x2 = x[:, :half], x[:, half:]
    return jnp.concatenate([x1 * cos - x2 * sin, x2 * cos + x1 * sin], axis=-1)


def _ret_prompt_body(u_ref, wq_ref, wk_ref, wv_ref, wg_ref, cos_ref, sin_ref, dm_ref, qd_ref, kd_ref, cd_ref,
                     ng_ref, or_ref, st_ref, s_scr, *, ts, c):
    s = pl.program_id(2)

    @pl.when(s == 0)
    def _():
        s_scr[...] = jnp.zeros_like(s_scr)

    u = u_ref[...]
    cos, sin = cos_ref[...], sin_ref[...]
    q_all = _rope(_dot(u, wq_ref[...]), cos, sin)
    k_all = _rope(_dot(u, wk_ref[...]), cos, sin) * (RET_DK ** -0.5)
    v_all = _dot(u, wv_ref[...]).astype(BF16)
    g_all = _dot(u, wg_ref[...])

    rows = [slice(ci * c, (ci + 1) * c) for ci in range(ts // c)]
    att = [(_dot_tb(q_all[r].astype(BF16), k_all[r].astype(BF16)) * dm_ref[...]).astype(BF16) for r in rows]
    o_intra = [_dot(a, v_all[r]) for a, r in zip(att, rows)]
    upd = [_dot_ta((k_all[r] * kd_ref[...]).astype(BF16), v_all[r]) for r in rows]
    q_dec = [(q_all[r] * qd_ref[...]).astype(BF16) for r in rows]
    st = s_scr[...]
    for ci, r in enumerate(rows):
        o = o_intra[ci] + _dot(q_dec[ci], st.astype(BF16))
        or_ref[r, :] = (_rms(o, ng_ref[...]) * _silu(g_all[r])).astype(BF16)
        st = cd_ref[...] * st + upd[ci]
    s_scr[...] = st

    @pl.when(s == pl.num_programs(2) - 1)
    def _():
        st_ref[...] = st


def _ret_log_gamma():
    return jnp.log1p(-jnp.exp2(-5.0 - jnp.arange(RET_HEADS, dtype=F32)))


def _rope_tables(pos):
    half = RET_DK // 2
    freqs = ROPE_BASE ** (-jnp.arange(half, dtype=F32) / half)
    ang = pos[:, None] * freqs[None, :]
    return jnp.cos(ang), jnp.sin(ang)


def _ret_prompt(u, batch, seq, w_b, ng):
    ts, c = SEQ_TILE, RET_CHUNK
    ns = seq // ts
    d = u.shape[1]
    kq, kk, kv, kg = WB_Q // RET_DK, WB_K // RET_DK, WB_V // RET_DV, WB_G // RET_DV
    cos, sin = _rope_tables(jnp.arange(seq, dtype=F32))
    lg = _ret_log_gamma()
    idx = jnp.arange(c, dtype=F32)
    diff = idx[:, None] - idx[None, :]
    dmat = jnp.where(diff >= 0, jnp.exp(lg[:, None, None] * jnp.maximum(diff, 0.0)), 0.0)
    qdec = jnp.broadcast_to(jnp.exp(lg[:, None] * (idx + 1.0)[None, :])[:, :, None], (RET_HEADS, c, RET_DK))
    kdec = jnp.broadcast_to(jnp.exp(lg[:, None] * (c - 1.0 - idx)[None, :])[:, :, None], (RET_HEADS, c, RET_DK))
    cdec = jnp.broadcast_to(jnp.exp(lg * c)[:, None, None], (RET_HEADS, 1, RET_DV))
    return pl.pallas_call(
        functools.partial(_ret_prompt_body, ts=ts, c=c),
        grid=(batch, RET_HEADS, ns),
        in_specs=[pl.BlockSpec((ts, d), lambda b, h, s: (b * ns + s, 0)),
                  pl.BlockSpec((d, RET_DK), lambda b, h, s: (0, kq + h)),
                  pl.BlockSpec((d, RET_DK), lambda b, h, s: (0, kk + h)),
                  pl.BlockSpec((d, RET_DV), lambda b, h, s: (0, kv + h)),
                  pl.BlockSpec((d, RET_DV), lambda b, h, s: (0, kg + h)),
                  pl.BlockSpec((ts, RET_DK // 2), lambda b, h, s: (s, 0)),
                  pl.BlockSpec((ts, RET_DK // 2), lambda b, h, s: (s, 0)),
                  pl.BlockSpec((None, c, c), lambda b, h, s: (h, 0, 0)),
                  pl.BlockSpec((None, c, RET_DK), lambda b, h, s: (h, 0, 0)),
                  pl.BlockSpec((None, c, RET_DK), lambda b, h, s: (h, 0, 0)),
                  pl.BlockSpec((None, 1, RET_DV), lambda b, h, s: (h, 0, 0)),
                  pl.BlockSpec((None, 1, RET_DV), lambda b, h, s: (h, 0, 0))],
        out_specs=(pl.BlockSpec((ts, RET_DV), lambda b, h, s: (b * ns + s, h)),
                   pl.BlockSpec((None, None, RET_DK, RET_DV), lambda b, h, s: (b, h, 0, 0))),
        out_shape=(jax.ShapeDtypeStruct((batch * seq, RET_V), BF16),
                   jax.ShapeDtypeStruct((batch, RET_HEADS, RET_DK, RET_DV), F32)),
        scratch_shapes=[pltpu.VMEM((RET_DK, RET_DV), F32)],
        compiler_params=_cparams("parallel", "parallel", "arbitrary"),
        name="ret_prompt",
    )(u, w_b, w_b, w_b, w_b, cos, sin, dmat, qdec, kdec, cdec, ng)


def _proj_body(u_ref, *refs, shift):
    if shift:
        a_ref, b_ref, o_ref, wb_ref = refs
        w = _shifted_cols(a_ref, b_ref, shift).astype(BF16)
    else:
        a_ref, o_ref, wb_ref = refs
        w = a_ref[...].astype(BF16)
    wb_ref[...] = w
    o_ref[...] = _dot(u_ref[...], w)


def _proj(u, w, start, n, *, tn=512):
    t, d = u.shape
    if start % LANES:
        (a_spec, b_spec), shift = _window_specs(d, tn, start, lambda j: j)
        w_specs, w_args = [a_spec, b_spec], (w, w)
    else:
        shift = 0
        w_specs, w_args = [pl.BlockSpec((d, tn), lambda j: (0, start // tn + j))], (w,)
    return pl.pallas_call(
        functools.partial(_proj_body, shift=shift),
        grid=(n // tn,),
        in_specs=[pl.BlockSpec((t, d), lambda j: (0, 0))] + w_specs,
        out_specs=[pl.BlockSpec((t, tn), lambda j: (0, j)), pl.BlockSpec((d, tn), lambda j: (0, j))],
        out_shape=[jax.ShapeDtypeStruct((t, n), F32), jax.ShapeDtypeStruct((d, n), BF16)],
        compiler_params=_cparams("parallel"),
        name="sample_proj_castw",
    )(u, *w_args)


def _gate_body(u_ref, wga_ref, wgu_ref, bg_ref, o_ref):
    ga = _dot(u_ref[...], wga_ref[...])
    logit = _dot(ga.astype(BF16), wgu_ref[...]) + bg_ref[...]
    o_ref[...] = _log_sigmoid(logit) / GLA_GATE_TAU


def _gate(u, wga, wgu, bg):
    t = u.shape[0]
    return pl.pallas_call(
        _gate_body,
        out_shape=jax.ShapeDtypeStruct((t, GLA_QK), F32),
        compiler_params=pltpu.CompilerParams(vmem_limit_bytes=VMEM_LIMIT),
        name="sample_gate",
    )(u, wga, wgu, bg)


def _row_group_mask(rows, cols, bt, i):
    return (lax.broadcasted_iota(jnp.int32, (rows, cols), 0) % bt) == i


def _gla_decode_body(q_ref, k_ref, v_ref, g_ref, la_ref, ng_ref, s_ref, og_ref, so_ref, *, nt, bt):
    q = [q_ref[t] * (GLA_DK ** -0.5) for t in range(nt)]
    k = [k_ref[t] for t in range(nt)]
    v = [v_ref[t] for t in range(nt)]
    b = [la_ref[0]]
    for t in range(1, nt):
        b.append(b[-1] + la_ref[t])
    b_mid, b_last = b[nt // 2], b[nt - 1]
    q_rel = [q[t] * jnp.exp(b[t] - b_mid) for t in range(nt)]
    k_rel = [k[t] * jnp.exp(b_mid - b[t]) for t in range(nt)]
    intra = []
    for t in range(nt):
        acc = None
        for s in range(t + 1):
            term = jnp.sum(q_rel[t] * k_rel[s], axis=-1, keepdims=True) * v[s]
            acc = term if acc is None else acc + term
        intra.append(acc)
    q_dec = jnp.concatenate([q[t] * jnp.exp(b[t]) for t in range(nt)], axis=0).astype(BF16)
    k_end = jnp.concatenate([k[t] * jnp.exp(b_last - b[t]) for t in range(nt)], axis=0)
    v_all = jnp.concatenate(v, axis=0).astype(BF16)
    decay = jnp.exp(b_last)
    inter = jnp.zeros((nt * bt, GLA_DV), F32)
    for i in range(bt):
        st = s_ref[i]
        inter = inter + jnp.where(_row_group_mask(nt * bt, GLA_DV, bt, i), _dot(q_dec, st.astype(BF16)), 0.0)
        k_i = jnp.where(_row_group_mask(nt * bt, GLA_DK, bt, i), k_end, 0.0).astype(BF16)
        upd = _dot_ta(k_i, v_all)
        dcol = _lane_replicated_column(decay[i:i + 1, :])
        for jj in range(GLA_DV // LANES):
            sl = slice(jj * LANES, (jj + 1) * LANES)
            so_ref[i, :, sl] = dcol * st[:, sl] + upd[:, sl]
    for t in range(nt):
        o = intra[t] + inter[t * bt:(t + 1) * bt, :]
        og_ref[t] = (_rms(o, ng_ref[...]) * _silu(g_ref[t])).astype(BF16)


def _gla_decode(pg, la, state, ng, nt, nb):
    bt = DEC_BT
    kq, kk, kv, kg = WA_Q // GLA_DK, WA_K // GLA_DK, WA_V // GLA_DV, WA_G // GLA_DV
    return pl.pallas_call(
        functools.partial(_gla_decode_body, nt=nt, bt=bt),
        grid=(GLA_HEADS, nb // bt),
        in_specs=[pl.BlockSpec((nt, bt, GLA_DK), lambda h, i: (0, i, kq + h)),
                  pl.BlockSpec((nt, bt, GLA_DK), lambda h, i: (0, i, kk + h)),
                  pl.BlockSpec((nt, bt, GLA_DV), lambda h, i: (0, i, kv + h)),
                  pl.BlockSpec((nt, bt, GLA_DV), lambda h, i: (0, i, kg + h)),
                  pl.BlockSpec((nt, bt, GLA_DK), lambda h, i: (0, i, h)),
                  pl.BlockSpec((None, 1, GLA_DV), lambda h, i: (h, 0, 0)),
                  pl.BlockSpec((bt, None, GLA_DK, GLA_DV), lambda h, i: (i, h, 0, 0))],
        out_specs=(pl.BlockSpec((nt, bt, GLA_DV), lambda h, i: (0, i, h)),
                   pl.BlockSpec((bt, None, GLA_DK, GLA_DV), lambda h, i: (i, h, 0, 0))),
        out_shape=(jax.ShapeDtypeStruct((nt, nb, GLA_V), BF16),
                   jax.ShapeDtypeStruct(state.shape, F32)),
        compiler_params=_cparams("parallel", "parallel"),
        name="gla_decode",
    )(pg, pg, pg, pg, la, ng, state)


def _ret_decode_body(pw_ref, q_ref, k_ref, v_ref, g_ref, cos_ref, sin_ref, ng_ref, s_ref, or_ref, so_ref, *, nt, bt):
    h = pl.program_id(0)
    pw = [pw_ref[h, n] for n in range(nt + 1)]
    q = [_rope(q_ref[t], cos_ref[t:t + 1, :], sin_ref[t:t + 1, :]) for t in range(nt)]
    k = [_rope(k_ref[t], cos_ref[t:t + 1, :], sin_ref[t:t + 1, :]) * (RET_DK ** -0.5) for t in range(nt)]
    v = [v_ref[t] for t in range(nt)]
    intra = []
    for t in range(nt):
        acc = None
        for s in range(t + 1):
            term = (jnp.sum(q[t] * k[s], axis=-1, keepdims=True) * pw[t - s]) * v[s]
            acc = term if acc is None else acc + term
        intra.append(acc)
    q_dec = jnp.concatenate([q[t] * pw[t + 1] for t in range(nt)], axis=0).astype(BF16)
    k_end = jnp.concatenate([k[t] * pw[nt - 1 - t] for t in range(nt)], axis=0)
    v_all = jnp.concatenate(v, axis=0).astype(BF16)
    inter = jnp.zeros((nt * bt, RET_DV), F32)
    for i in range(bt):
        st = s_ref[i]
        inter = inter + jnp.where(_row_group_mask(nt * bt, RET_DV, bt, i), _dot(q_dec, st.astype(BF16)), 0.0)
        k_i = jnp.where(_row_group_mask(nt * bt, RET_DK, bt, i), k_end, 0.0).astype(BF16)
        so_ref[i] = pw[nt] * st + _dot_ta(k_i, v_all)
    for t in range(nt):
        o = intra[t] + inter[t * bt:(t + 1) * bt, :]
        or_ref[t] = (_rms(o, ng_ref[...]) * _silu(g_ref[t])).astype(BF16)


def _ret_decode(pr, state, ng, nt, nb):
    bt = DEC_BT
    kq, kk, kv, kg = WB_Q // RET_DK, WB_K // RET_DK, WB_V // RET_DV, WB_G // RET_DV
    cos, sin = _rope_tables(PAST_LEN + jnp.arange(nt, dtype=F32))
    pw = jnp.exp(_ret_log_gamma()[:, None] * jnp.arange(nt + 1, dtype=F32)[None, :])
    return pl.pallas_call(
        functools.partial(_ret_decode_body, nt=nt, bt=bt),
        grid=(RET_HEADS, nb // bt),
        in_specs=[pl.BlockSpec(memory_space=pltpu.SMEM),
                  pl.BlockSpec((nt, bt, RET_DK), lambda h, i: (0, i, kq + h)),
                  pl.BlockSpec((nt, bt, RET_DK), lambda h, i: (0, i, kk + h)),
                  pl.BlockSpec((nt, bt, RET_DV), lambda h, i: (0, i, kv + h)),
                  pl.BlockSpec((nt, bt, RET_DV), lambda h, i: (0, i, kg + h)),
                  pl.BlockSpec((nt, RET_DK // 2), lambda h, i: (0, 0)),
                  pl.BlockSpec((nt, RET_DK // 2), lambda h, i: (0, 0)),
                  pl.BlockSpec((None, 1, RET_DV), lambda h, i: (h, 0, 0)),
                  pl.BlockSpec((bt, None, RET_DK, RET_DV), lambda h, i: (i, h, 0, 0))],
        out_specs=(pl.BlockSpec((nt, bt, RET_DV), lambda h, i: (0, i, h)),
                   pl.BlockSpec((bt, None, RET_DK, RET_DV), lambda h, i: (i, h, 0, 0))),
        out_shape=(jax.ShapeDtypeStruct((nt, nb, RET_V), BF16),
                   jax.ShapeDtypeStruct(state.shape, F32)),
        compiler_params=_cparams("parallel", "parallel"),
        name="ret_decode",
    )(pw, pr, pr, pr, pr, cos, sin, ng, state)


def kernel(x_prompt, x_sample, state_gla, state_ret, ffn1_norm, ffn1_w1, ffn1_w3, ffn1_w2, mix_norm, w_in, w_gate_up, b_gate, gla_norm, w_gla_o, ret_norm, w_ret_o, w_out, ffn2_norm, ffn2_w1, ffn2_w3, ffn2_w2, final_norm):
    depth = w_in.shape[0]
    batch, seq, d = x_prompt.shape
    nb, nt, _ = x_sample.shape

    hp = x_prompt.reshape(batch * seq, d)
    hs = x_sample.transpose(1, 0, 2).reshape(nt * nb, d)
    gla_p, ret_p, gla_s, ret_s = [], [], [], []
    for l in range(depth):
        last = l == depth - 1
        wl = w_in[l]
        w_ga = wl[:, WA_WIDTH:WB_START].astype(BF16)
        wgu = w_gate_up[l].astype(BF16)
        bg = b_gate[l].reshape(1, GLA_QK)
        ng = gla_norm[l].reshape(GLA_HEADS, 1, GLA_DV)
        nr = ret_norm[l].reshape(RET_HEADS, 1, RET_DV)
        g_next = final_norm if last else ffn1_norm[l + 1]

        hs, us, *f1 = _ffn(hs, ffn1_norm[l], ffn1_w1[l], ffn1_w3[l], ffn1_w2[l], mix_norm[l], final=False)
        hp, up = _ffn(hp, ffn1_norm[l], *f1, mix_norm[l], final=False)

        pg, w_a = _proj(us, wl, 0, WA_WIDTH)
        pr, w_b = _proj(us, wl, WB_START, WB_ZA)
        la = _gate(us, w_ga, wgu, bg)
        og_s, sg_s = _gla_decode(pg.reshape(nt, nb, -1), la.reshape(nt, nb, -1), state_gla[l], ng, nt, nb)
        or_s, sr_s = _ret_decode(pr.reshape(nt, nb, -1), state_ret[l], nr, nt, nb)
        og, sg = _gla_prompt(up, batch, seq, w_a, w_ga, wgu, bg, ng)
        orr, sr = _ret_prompt(up, batch, seq, w_b, nr)
        gla_s.append(sg_s)
        ret_s.append(sr_s)
        gla_p.append(sg)
        ret_p.append(sr)

        ms, wza, wzb, wgo, wro = _merge_gate(us, og_s.reshape(nt * nb, GLA_V), or_s.reshape(nt * nb, RET_V),
                                             wl, w_gla_o[l], w_ret_o[l])
        mp, = _merge_gate(up, og, orr, (wza, wzb), wgo, wro)
        hs, wout = _out_proj(hs, ms, w_out[l])
        hp, = _out_proj(hp, mp, wout)

        hs, *f2 = _ffn(hs, ffn2_norm[l], ffn2_w1[l], ffn2_w3[l], ffn2_w2[l], g_next, final=last)
        hp, *_ = _ffn(hp, ffn2_norm[l], *f2[-3:], g_next, final=last)

    y_prompt = hp.reshape(batch, seq, d)
    y_sample = hs.reshape(nt, nb, d).transpose(1, 0, 2)
    return (y_prompt, y_sample, jnp.stack(gla_p), jnp.stack(ret_p), jnp.stack(gla_s), jnp.stack(ret_s))
```

```python
import functools

import jax
import jax.numpy as jnp
from jax import lax
from jax.experimental import pallas as pl
from jax.experimental.pallas import tpu as pltpu

F32, BF16 = jnp.float32, jnp.bfloat16

D_MODEL = 2048
PAST_LEN = 16384
GLA_HEADS = 4
GLA_DK = D_MODEL // (2 * GLA_HEADS)
GLA_DV = D_MODEL // GLA_HEADS
GLA_GATE_RANK = 16
GLA_GATE_TAU = 16.0
RET_HEADS = 8
RET_DK = D_MODEL // RET_HEADS
RET_DV = 2 * D_MODEL // RET_HEADS
ROPE_BASE = 10000.0
EPS = 1e-6
GLA_QK = GLA_HEADS * GLA_DK
GLA_V = GLA_HEADS * GLA_DV
RET_QK = RET_HEADS * RET_DK
RET_V = RET_HEADS * RET_DV

WA_Q, WA_K, WA_V, WA_G = 0, GLA_QK, 2 * GLA_QK, 2 * GLA_QK + GLA_V
WA_WIDTH = 2 * GLA_QK + 2 * GLA_V
WB_Q, WB_K, WB_V, WB_G = 0, RET_QK, 2 * RET_QK, 2 * RET_QK + RET_V
WB_ZA = 2 * RET_QK + 2 * RET_V
WB_ZB = WB_ZA + D_MODEL
WB_START = WA_WIDTH + GLA_GATE_RANK

LANES = 128
SUBLANES = 8
VMEM_LIMIT = 58 * 2**20

GLA_CHUNK = 64
RET_CHUNK = 256
SEQ_TILE = 512
DEC_BT = 16


def _cparams(*sem):
    return pltpu.CompilerParams(dimension_semantics=sem, vmem_limit_bytes=VMEM_LIMIT)


def _dot(a, b):
    return jnp.dot(a, b, preferred_element_type=F32)


def _dot_tb(a, b):
    return lax.dot_general(a, b, (((1,), (1,)), ((), ())), preferred_element_type=F32)


def _dot_ta(a, b):
    return lax.dot_general(a, b, (((0,), (0,)), ((), ())), preferred_element_type=F32)


def _rms(x, g):
    return x * lax.rsqrt(jnp.mean(x * x, axis=-1, keepdims=True) + EPS) * g


def _sigmoid(x):
    return 1.0 / (1.0 + jnp.exp(-x))


def _silu(x):
    return x * _sigmoid(x)


def _log_sigmoid(x):
    return jnp.minimum(x, 0.0) - jnp.log1p(jnp.exp(-jnp.abs(x)))


def _lane_replicated_column(row):
    return jnp.broadcast_to(row, (LANES, row.shape[-1])).T


def _ffn_body(x_ref, g_ref, w1_ref, w3_ref, w2_ref, g2_ref, *refs, tm, rows, final, emit_w):
    refs = list(refs)
    u_scr = refs.pop()
    wb_refs = [refs.pop() for _ in range(3)][::-1] if emit_w else None
    if final:
        y_ref, = refs
        acc_ref = y_ref
    else:
        h_ref, un_ref = refs
        acc_ref = h_ref
    j = pl.program_id(1)

    @pl.when(j == 0)
    def _():
        for r in range(0, tm, rows):
            u_scr[r:r + rows, :] = _rms(x_ref[r:r + rows, :], g_ref[...]).astype(BF16)
            acc_ref[r:r + rows, :] = jnp.zeros((rows, acc_ref.shape[1]), F32)

    w1, w3, w2 = w1_ref[...].astype(BF16), w3_ref[...].astype(BF16), w2_ref[...].astype(BF16)
    if emit_w:
        for ref, w in zip(wb_refs, (w1, w3, w2)):
            ref[...] = w
    u = u_scr[...]
    a = _dot(u, w1)
    b = _dot(u, w3)
    acc_ref[...] += _dot((_silu(a) * b).astype(BF16), w2)

    @pl.when(j == pl.num_programs(1) - 1)
    def _():
        for r in range(0, tm, rows):
            h = x_ref[r:r + rows, :] + 0.5 * acc_ref[r:r + rows, :]
            if final:
                y_ref[r:r + rows, :] = _rms(h, g2_ref[...])
            else:
                h_ref[r:r + rows, :] = h
                un_ref[r:r + rows, :] = _rms(h, g2_ref[...]).astype(BF16)


def _ffn(x, g, w1, w3, w2, g2, *, final, tm=512):
    t, d = x.shape
    dff = w1.shape[1]
    tm = min(tm, t)
    emit_w = w1.dtype == F32
    assert not emit_w or t == tm, "weights are emitted by a single-token-tile call"
    tf = 256 if emit_w else 512
    grid = (t // tm, dff // tf)
    row = pl.BlockSpec((tm, d), lambda i, j: (i, 0))
    vec = pl.BlockSpec((1, d), lambda i, j: (0, 0))
    w_up = pl.BlockSpec((d, tf), lambda i, j: (0, j))
    w_dn = pl.BlockSpec((tf, d), lambda i, j: (j, 0))
    out_shape = [jax.ShapeDtypeStruct((t, d), F32)] + ([] if final else [jax.ShapeDtypeStruct((t, d), BF16)])
    out_specs = [row] * len(out_shape)
    if emit_w:
        out_shape += [jax.ShapeDtypeStruct(w.shape, BF16) for w in (w1, w3, w2)]
        out_specs += [w_up, w_up, w_dn]
    return pl.pallas_call(
        functools.partial(_ffn_body, tm=tm, rows=min(256, tm), final=final, emit_w=emit_w),
        grid=grid,
        in_specs=[row, vec, w_up, w_up, w_dn, vec],
        out_specs=out_specs,
        out_shape=out_shape,
        scratch_shapes=[pltpu.VMEM((tm, d), BF16)],
        compiler_params=_cparams("parallel", "arbitrary"),
        name=("ffn_final" if final else "ffn_mid") + ("_castw" if emit_w else ""),
    )(x, g.reshape(1, d), w1, w3, w2, g2.reshape(1, d))


def _w_cols(a_ref, b_ref):
    rows = a_ref[...] if b_ref is None else jnp.concatenate([a_ref[b_ref.shape[0]:, :], b_ref[...]], axis=0)
    return rows.T.astype(BF16)


def _w_cols_specs(width, tn, start, col_of):
    shift = start % tn
    base = start - shift
    if shift == 0:
        return [pl.BlockSpec((tn, width), lambda *g: (base // tn + col_of(*g), 0))]
    assert shift % SUBLANES == 0 and tn % shift == 0 and base % shift == 0
    return [pl.BlockSpec((tn, width), lambda *g: (base // tn + col_of(*g), 0)),
            pl.BlockSpec((shift, width), lambda *g: ((base + tn * (col_of(*g) + 1)) // shift, 0))]


def _merge_body(u_ref, og_ref, or_ref, *refs, emit_w):
    u = u_ref[...]
    if not emit_w:
        wza_ref, wzb_ref, wgo_ref, wro_ref, m_ref = refs
        wza, wzb, wgo, wro = wza_ref[...], wzb_ref[...], wgo_ref[...], wro_ref[...]
    else:
        za_a, za_b, zb_a, zb_b, wgo_ref, wro_ref, m_ref, wza_o, wzb_o, wgo_o, wro_o = refs
        wza = _w_cols(za_a, za_b)
        wzb = _w_cols(zb_a, zb_b)
        wgo, wro = wgo_ref[...].astype(BF16), wro_ref[...].astype(BF16)
        wza_o[...], wzb_o[...], wgo_o[...], wro_o[...] = wza, wzb, wgo, wro
    za = _dot(u, wza)
    zb = _dot(u, wzb)
    branch_a = _dot(og_ref[...], wgo)
    branch_b = _dot(or_ref[...], wro)
    m_ref[...] = (_sigmoid(za) * branch_a + _sigmoid(zb) * branch_b).astype(BF16)


def _merge_gate(u, og, orr, wz, wgo, wro, *, tm=512):
    t, d = u.shape
    tm = min(tm, t)
    emit_w = wgo.dtype == F32
    assert not emit_w or t == tm, "weights are emitted by a single-token-tile call"
    tn = 256 if emit_w else 512
    col = lambda i, j: j
    acts = [pl.BlockSpec((tm, d), lambda i, j: (i, 0)),
            pl.BlockSpec((tm, GLA_V), lambda i, j: (i, 0)),
            pl.BlockSpec((tm, RET_V), lambda i, j: (i, 0))]
    w_col = lambda rows: pl.BlockSpec((rows, tn), lambda i, j: (0, j))
    out_shape = [jax.ShapeDtypeStruct((t, d), BF16)]
    out_specs = [pl.BlockSpec((tm, tn), lambda i, j: (i, j))]
    if emit_w:
        w_specs = (_w_cols_specs(d, tn, WB_START + WB_ZA, col) + _w_cols_specs(d, tn, WB_START + WB_ZB, col)
                   + [w_col(GLA_V), w_col(RET_V)])
        w_args = (wz, wz, wz, wz, wgo, wro)
        out_shape += [jax.ShapeDtypeStruct((d, d), BF16), jax.ShapeDtypeStruct((d, d), BF16),
                      jax.ShapeDtypeStruct(wgo.shape, BF16), jax.ShapeDtypeStruct(wro.shape, BF16)]
        out_specs += [w_col(d), w_col(d), w_col(GLA_V), w_col(RET_V)]
    else:
        w_specs, w_args = [w_col(d), w_col(d), w_col(GLA_V), w_col(RET_V)], (*wz, wgo, wro)
    return pl.pallas_call(
        functools.partial(_merge_body, emit_w=emit_w),
        grid=(t // tm, d // tn),
        in_specs=acts + w_specs,
        out_specs=out_specs,
        out_shape=out_shape,
        compiler_params=_cparams("parallel", "parallel"),
        name="merge_gate" + ("_castw" if emit_w else ""),
    )(u, og, orr, *w_args)


def _out_body(h_ref, m_ref, wout_ref, o_ref, *wb_ref):
    wout = wout_ref[...].astype(BF16)
    if wb_ref:
        wb_ref[0][...] = wout
    o_ref[...] = h_ref[...] + _dot(m_ref[...], wout)


def _out_proj(h, merged, wout, *, tm=512):
    t, d = h.shape
    tm = min(tm, t)
    emit_w = wout.dtype == F32
    assert not emit_w or t == tm, "weights are emitted by a single-token-tile call"
    tn = 512 if emit_w else d
    out_shape = [jax.ShapeDtypeStruct((t, d), F32)] + ([jax.ShapeDtypeStruct((d, d), BF16)] if emit_w else [])
    out_specs = [pl.BlockSpec((tm, tn), lambda i, j: (i, j))] + ([pl.BlockSpec((d, tn), lambda i, j: (0, j))] if emit_w else [])
    return pl.pallas_call(
        _out_body,
        grid=(t // tm, d // tn),
        in_specs=[pl.BlockSpec((tm, tn), lambda i, j: (i, j)),
                  pl.BlockSpec((tm, d), lambda i, j: (i, 0)),
                  pl.BlockSpec((d, tn), lambda i, j: (0, j))],
        out_specs=out_specs,
        out_shape=out_shape,
        compiler_params=_cparams("parallel", "parallel"),
        name="out_proj" + ("_castw" if emit_w else ""),
    )(h, merged, wout)


def _gla_prompt_body(u_ref, wq_ref, wk_ref, wv_ref, wg_ref, wga_ref, wgu_ref, bg_ref, ng_ref,
                     og_ref, st_ref, s_scr, *, ts, c):
    s = pl.program_id(2)

    @pl.when(s == 0)
    def _():
        s_scr[...] = jnp.zeros_like(s_scr)

    u = u_ref[...]
    ga = _dot(u, wga_ref[...])
    logit = _dot(ga.astype(BF16), wgu_ref[...]) + bg_ref[...]
    la_all = _log_sigmoid(logit) / GLA_GATE_TAU
    la_hi = la_all.astype(BF16)
    la_lo = (la_all - la_hi.astype(F32)).astype(BF16)
    q_all = _dot(u, wq_ref[...]) * (GLA_DK ** -0.5)
    k_all = _dot(u, wk_ref[...])
    v_all = _dot(u, wv_ref[...]).astype(BF16)
    g_all = _dot(u, wg_ref[...])

    rr = lax.broadcasted_iota(jnp.int32, (c, c), 0)
    cc = lax.broadcasted_iota(jnp.int32, (c, c), 1)
    causal = rr >= cc
    tril = causal.astype(BF16)
    mid = c // 2
    rows = [slice(ci * c, (ci + 1) * c) for ci in range(ts // c)]

    b = [_dot(tril, la_hi[r]) + _dot(tril, la_lo[r]) for r in rows]
    q_rel, k_rel, q_dec, k_end, decay = [], [], [], [], []
    for r, bc in zip(rows, b):
        q, k = q_all[r], k_all[r]
        b_mid, b_last = bc[mid:mid + 1, :], bc[c - 1:c, :]
        q_rel.append((q * jnp.exp(bc - b_mid)).astype(BF16))
        k_rel.append((k * jnp.exp(b_mid - bc)).astype(BF16))
        q_dec.append((q * jnp.exp(bc)).astype(BF16))
        k_end.append((k * jnp.exp(b_last - bc)).astype(BF16))
        dcol = _lane_replicated_column(jnp.exp(b_last))
        decay.append(jnp.concatenate([dcol] * (GLA_DV // LANES), axis=1))
    att = [jnp.where(causal, _dot_tb(qr, kr), 0.0).astype(BF16) for qr, kr in zip(q_rel, k_rel)]
    o_intra = [_dot(a, v_all[r]) for a, r in zip(att, rows)]
    upd = [_dot_ta(ke, v_all[r]) for ke, r in zip(k_end, rows)]
    st = s_scr[...]
    for ci, r in enumerate(rows):
        o = o_intra[ci] + _dot(q_dec[ci], st.astype(BF16))
        og_ref[r, :] = (_rms(o, ng_ref[...]) * _silu(g_all[r])).astype(BF16)
        st = decay[ci] * st + upd[ci]
    s_scr[...] = st

    @pl.when(s == pl.num_programs(2) - 1)
    def _():
        st_ref[...] = st


def _gla_prompt(u, batch, seq, w_a, wga, wgu, bg, ng):
    ts, c = SEQ_TILE, GLA_CHUNK
    ns = seq // ts
    d = u.shape[1]
    kq, kk, kv, kg = WA_Q // GLA_DK, WA_K // GLA_DK, WA_V // GLA_DV, WA_G // GLA_DV
    return pl.pallas_call(
        functools.partial(_gla_prompt_body, ts=ts, c=c),
        grid=(batch, GLA_HEADS, ns),
        in_specs=[pl.BlockSpec((ts, d), lambda b, h, s: (b * ns + s, 0)),
                  pl.BlockSpec((d, GLA_DK), lambda b, h, s: (0, kq + h)),
                  pl.BlockSpec((d, GLA_DK), lambda b, h, s: (0, kk + h)),
                  pl.BlockSpec((d, GLA_DV), lambda b, h, s: (0, kv + h)),
                  pl.BlockSpec((d, GLA_DV), lambda b, h, s: (0, kg + h)),
                  pl.BlockSpec((d, GLA_GATE_RANK), lambda b, h, s: (0, 0)),
                  pl.BlockSpec((GLA_GATE_RANK, GLA_DK), lambda b, h, s: (0, h)),
                  pl.BlockSpec((1, GLA_DK), lambda b, h, s: (0, h)),
                  pl.BlockSpec((None, 1, GLA_DV), lambda b, h, s: (h, 0, 0))],
        out_specs=(pl.BlockSpec((ts, GLA_DV), lambda b, h, s: (b * ns + s, h)),
                   pl.BlockSpec((None, None, GLA_DK, GLA_DV), lambda b, h, s: (b, h, 0, 0))),
        out_shape=(jax.ShapeDtypeStruct((batch * seq, GLA_V), BF16),
                   jax.ShapeDtypeStruct((batch, GLA_HEADS, GLA_DK, GLA_DV), F32)),
        scratch_shapes=[pltpu.VMEM((GLA_DK, GLA_DV), F32)],
        compiler_params=_cparams("parallel", "parallel", "arbitrary"),
        name="gla_prompt",
    )(u, w_a, w_a, w_a, w_a, wga, wgu, bg, ng)


def _rope(x, cos, sin):
    half = x.shape[-1] // 2
    x1, x2 = x[:, :half], x[:, half:]
    return jnp.concatenate([x1 * cos - x2 * sin, x2 * cos + x1 * sin], axis=-1)


def _ret_prompt_body(u_ref, wq_ref, wk_ref, wv_ref, wg_ref, cos_ref, sin_ref, dm_ref, qd_ref, kd_ref, cd_ref,
                     ng_ref, or_ref, st_ref, s_scr, *, ts, c):
    s = pl.program_id(2)

    @pl.when(s == 0)
    def _():
        s_scr[...] = jnp.zeros_like(s_scr)

    u = u_ref[...]
    cos, sin = cos_ref[...], sin_ref[...]
    q_all = _rope(_dot(u, wq_ref[...]), cos, sin)
    k_all = _rope(_dot(u, wk_ref[...]), cos, sin) * (RET_DK ** -0.5)
    v_all = _dot(u, wv_ref[...]).astype(BF16)
    g_all = _dot(u, wg_ref[...])

    rows = [slice(ci * c, (ci + 1) * c) for ci in range(ts // c)]
    att = [(_dot_tb(q_all[r].astype(BF16), k_all[r].astype(BF16)) * dm_ref[...]).astype(BF16) for r in rows]
    o_intra = [_dot(a, v_all[r]) for a, r in zip(att, rows)]
    upd = [_dot_ta((k_all[r] * kd_ref[...]).astype(BF16), v_all[r]) for r in rows]
    q_dec = [(q_all[r] * qd_ref[...]).astype(BF16) for r in rows]
    st = s_scr[...]
    for ci, r in enumerate(rows):
        o = o_intra[ci] + _dot(q_dec[ci], st.astype(BF16))
        or_ref[r, :] = (_rms(o, ng_ref[...]) * _silu(g_all[r])).astype(BF16)
        st = cd_ref[...] * st + upd[ci]
    s_scr[...] = st

    @pl.when(s == pl.num_programs(2) - 1)
    def _():
        st_ref[...] = st


def _ret_log_gamma():
    return jnp.log1p(-jnp.exp2(-5.0 - jnp.arange(RET_HEADS, dtype=F32)))


def _rope_tables(pos):
    half = RET_DK // 2
    freqs = ROPE_BASE ** (-jnp.arange(half, dtype=F32) / half)
    ang = pos[:, None] * freqs[None, :]
    return jnp.cos(ang), jnp.sin(ang)


def _ret_prompt(u, batch, seq, w_b, ng):
    ts, c = SEQ_TILE, RET_CHUNK
    ns = seq // ts
    d = u.shape[1]
    kq, kk, kv, kg = WB_Q // RET_DK, WB_K // RET_DK, WB_V // RET_DV, WB_G // RET_DV
    cos, sin = _rope_tables(jnp.arange(seq, dtype=F32))
    lg = _ret_log_gamma()
    idx = jnp.arange(c, dtype=F32)
    diff = idx[:, None] - idx[None, :]
    dmat = jnp.where(diff >= 0, jnp.exp(lg[:, None, None] * jnp.maximum(diff, 0.0)), 0.0)
    qdec = jnp.broadcast_to(jnp.exp(lg[:, None] * (idx + 1.0)[None, :])[:, :, None], (RET_HEADS, c, RET_DK))
    kdec = jnp.broadcast_to(jnp.exp(lg[:, None] * (c - 1.0 - idx)[None, :])[:, :, None], (RET_HEADS, c, RET_DK))
    cdec = jnp.broadcast_to(jnp.exp(lg * c)[:, None, None], (RET_HEADS, 1, RET_DV))
    return pl.pallas_call(
        functools.partial(_ret_prompt_body, ts=ts, c=c),
        grid=(batch, RET_HEADS, ns),
        in_specs=[pl.BlockSpec((ts, d), lambda b, h, s: (b * ns + s, 0)),
                  pl.BlockSpec((d, RET_DK), lambda b, h, s: (0, kq + h)),
                  pl.BlockSpec((d, RET_DK), lambda b, h, s: (0, kk + h)),
                  pl.BlockSpec((d, RET_DV), lambda b, h, s: (0, kv + h)),
                  pl.BlockSpec((d, RET_DV), lambda b, h, s: (0, kg + h)),
                  pl.BlockSpec((ts, RET_DK // 2), lambda b, h, s: (s, 0)),
                  pl.BlockSpec((ts, RET_DK // 2), lambda b, h, s: (s, 0)),
                  pl.BlockSpec((None, c, c), lambda b, h, s: (h, 0, 0)),
                  pl.BlockSpec((None, c, RET_DK), lambda b, h, s: (h, 0, 0)),
                  pl.BlockSpec((None, c, RET_DK), lambda b, h, s: (h, 0, 0)),
                  pl.BlockSpec((None, 1, RET_DV), lambda b, h, s: (h, 0, 0)),
                  pl.BlockSpec((None, 1, RET_DV), lambda b, h, s: (h, 0, 0))],
        out_specs=(pl.BlockSpec((ts, RET_DV), lambda b, h, s: (b * ns + s, h)),
                   pl.BlockSpec((None, None, RET_DK, RET_DV), lambda b, h, s: (b, h, 0, 0))),
        out_shape=(jax.ShapeDtypeStruct((batch * seq, RET_V), BF16),
                   jax.ShapeDtypeStruct((batch, RET_HEADS, RET_DK, RET_DV), F32)),
        scratch_shapes=[pltpu.VMEM((RET_DK, RET_DV), F32)],
        compiler_params=_cparams("parallel", "parallel", "arbitrary"),
        name="ret_prompt",
    )(u, w_b, w_b, w_b, w_b, cos, sin, dmat, qdec, kdec, cdec, ng)


def _proj_body(u_ref, *refs):
    *w_refs, o_ref, wb_ref = refs
    w = _w_cols(w_refs[0], w_refs[1] if len(w_refs) > 1 else None)
    wb_ref[...] = w
    o_ref[...] = _dot(u_ref[...], w)


def _proj(u, w_t, start, n, *, tn=512):
    t, d = u.shape
    w_specs = _w_cols_specs(d, tn, start, lambda j: j)
    return pl.pallas_call(
        _proj_body,
        grid=(n // tn,),
        in_specs=[pl.BlockSpec((t, d), lambda j: (0, 0))] + w_specs,
        out_specs=[pl.BlockSpec((t, tn), lambda j: (0, j)), pl.BlockSpec((d, tn), lambda j: (0, j))],
        out_shape=[jax.ShapeDtypeStruct((t, n), F32), jax.ShapeDtypeStruct((d, n), BF16)],
        compiler_params=_cparams("parallel"),
        name="sample_proj_castw",
    )(u, *([w_t] * len(w_specs)))


def _gate_body(u_ref, wga_ref, wgu_ref, bg_ref, o_ref):
    ga = _dot(u_ref[...], wga_ref[...])
    logit = _dot(ga.astype(BF16), wgu_ref[...]) + bg_ref[...]
    o_ref[...] = _log_sigmoid(logit) / GLA_GATE_TAU


def _gate(u, wga, wgu, bg):
    t = u.shape[0]
    return pl.pallas_call(
        _gate_body,
        out_shape=jax.ShapeDtypeStruct((t, GLA_QK), F32),
        compiler_params=pltpu.CompilerParams(vmem_limit_bytes=VMEM_LIMIT),
        name="sample_gate",
    )(u, wga, wgu, bg)


def _row_group_mask(rows, cols, bt, i):
    return (lax.broadcasted_iota(jnp.int32, (rows, cols), 0) % bt) == i


def _gla_decode_body(q_ref, k_ref, v_ref, g_ref, la_ref, ng_ref, s_ref, og_ref, so_ref, *, nt, bt):
    q = [q_ref[t] * (GLA_DK ** -0.5) for t in range(nt)]
    k = [k_ref[t] for t in range(nt)]
    v = [v_ref[t] for t in range(nt)]
    b = [la_ref[0]]
    for t in range(1, nt):
        b.append(b[-1] + la_ref[t])
    b_mid, b_last = b[nt // 2], b[nt - 1]
    q_rel = [q[t] * jnp.exp(b[t] - b_mid) for t in range(nt)]
    k_rel = [k[t] * jnp.exp(b_mid - b[t]) for t in range(nt)]
    intra = []
    for t in range(nt):
        acc = None
        for s in range(t + 1):
            term = jnp.sum(q_rel[t] * k_rel[s], axis=-1, keepdims=True) * v[s]
            acc = term if acc is None else acc + term
        intra.append(acc)
    q_dec = jnp.concatenate([q[t] * jnp.exp(b[t]) for t in range(nt)], axis=0).astype(BF16)
    k_end = jnp.concatenate([k[t] * jnp.exp(b_last - b[t]) for t in range(nt)], axis=0)
    v_all = jnp.concatenate(v, axis=0).astype(BF16)
    decay = jnp.exp(b_last)
    inter = jnp.zeros((nt * bt, GLA_DV), F32)
    for i in range(bt):
        st = s_ref[i]
        inter = inter + jnp.where(_row_group_mask(nt * bt, GLA_DV, bt, i), _dot(q_dec, st.astype(BF16)), 0.0)
        k_i = jnp.where(_row_group_mask(nt * bt, GLA_DK, bt, i), k_end, 0.0).astype(BF16)
        upd = _dot_ta(k_i, v_all)
        dcol = _lane_replicated_column(decay[i:i + 1, :])
        for jj in range(GLA_DV // LANES):
            sl = slice(jj * LANES, (jj + 1) * LANES)
            so_ref[i, :, sl] = dcol * st[:, sl] + upd[:, sl]
    for t in range(nt):
        o = intra[t] + inter[t * bt:(t + 1) * bt, :]
        og_ref[t] = (_rms(o, ng_ref[...]) * _silu(g_ref[t])).astype(BF16)


def _gla_decode(pg, la, state, ng, nt, nb):
    bt = DEC_BT
    kq, kk, kv, kg = WA_Q // GLA_DK, WA_K // GLA_DK, WA_V // GLA_DV, WA_G // GLA_DV
    return pl.pallas_call(
        functools.partial(_gla_decode_body, nt=nt, bt=bt),
        grid=(GLA_HEADS, nb // bt),
        in_specs=[pl.BlockSpec((nt, bt, GLA_DK), lambda h, i: (0, i, kq + h)),
                  pl.BlockSpec((nt, bt, GLA_DK), lambda h, i: (0, i, kk + h)),
                  pl.BlockSpec((nt, bt, GLA_DV), lambda h, i: (0, i, kv + h)),
                  pl.BlockSpec((nt, bt, GLA_DV), lambda h, i: (0, i, kg + h)),
                  pl.BlockSpec((nt, bt, GLA_DK), lambda h, i: (0, i, h)),
                  pl.BlockSpec((None, 1, GLA_DV), lambda h, i: (h, 0, 0)),
                  pl.BlockSpec((bt, None, GLA_DK, GLA_DV), lambda h, i: (i, h, 0, 0))],
        out_specs=(pl.BlockSpec((nt, bt, GLA_DV), lambda h, i: (0, i, h)),
                   pl.BlockSpec((bt, None, GLA_DK, GLA_DV), lambda h, i: (i, h, 0, 0))),
        out_shape=(jax.ShapeDtypeStruct((nt, nb, GLA_V), BF16),
                   jax.ShapeDtypeStruct(state.shape, F32)),
        compiler_params=_cparams("parallel", "parallel"),
        name="gla_decode",
    )(pg, pg, pg, pg, la, ng, state)


def _ret_decode_body(pw_ref, q_ref, k_ref, v_ref, g_ref, cos_ref, sin_ref, ng_ref, s_ref, or_ref, so_ref, *, nt, bt):
    h = pl.program_id(0)
    pw = [pw_ref[h, n] for n in range(nt + 1)]
    q = [_rope(q_ref[t], cos_ref[t:t + 1, :], sin_ref[t:t + 1, :]) for t in range(nt)]
    k = [_rope(k_ref[t], cos_ref[t:t + 1, :], sin_ref[t:t + 1, :]) * (RET_DK ** -0.5) for t in range(nt)]
    v = [v_ref[t] for t in range(nt)]
    intra = []
    for t in range(nt):
        acc = None
        for s in range(t + 1):
            term = (jnp.sum(q[t] * k[s], axis=-1, keepdims=True) * pw[t - s]) * v[s]
            acc = term if acc is None else acc + term
        intra.append(acc)
    q_dec = jnp.concatenate([q[t] * pw[t + 1] for t in range(nt)], axis=0).astype(BF16)
    k_end = jnp.concatenate([k[t] * pw[nt - 1 - t] for t in range(nt)], axis=0)
    v_all = jnp.concatenate(v, axis=0).astype(BF16)
    inter = jnp.zeros((nt * bt, RET_DV), F32)
    for i in range(bt):
        st = s_ref[i]
        inter = inter + jnp.where(_row_group_mask(nt * bt, RET_DV, bt, i), _dot(q_dec, st.astype(BF16)), 0.0)
        k_i = jnp.where(_row_group_mask(nt * bt, RET_DK, bt, i), k_end, 0.0).astype(BF16)
        so_ref[i] = pw[nt] * st + _dot_ta(k_i, v_all)
    for t in range(nt):
        o = intra[t] + inter[t * bt:(t + 1) * bt, :]
        or_ref[t] = (_rms(o, ng_ref[...]) * _silu(g_ref[t])).astype(BF16)


def _ret_decode(pr, state, ng, nt, nb):
    bt = DEC_BT
    kq, kk, kv, kg = WB_Q // RET_DK, WB_K // RET_DK, WB_V // RET_DV, WB_G // RET_DV
    cos, sin = _rope_tables(PAST_LEN + jnp.arange(nt, dtype=F32))
    pw = jnp.exp(_ret_log_gamma()[:, None] * jnp.arange(nt + 1, dtype=F32)[None, :])
    return pl.pallas_call(
        functools.partial(_ret_decode_body, nt=nt, bt=bt),
        grid=(RET_HEADS, nb // bt),
        in_specs=[pl.BlockSpec(memory_space=pltpu.SMEM),
                  pl.BlockSpec((nt, bt, RET_DK), lambda h, i: (0, i, kq + h)),
                  pl.BlockSpec((nt, bt, RET_DK), lambda h, i: (0, i, kk + h)),
                  pl.BlockSpec((nt, bt, RET_DV), lambda h, i: (0, i, kv + h)),
                  pl.BlockSpec((nt, bt, RET_DV), lambda h, i: (0, i, kg + h)),
                  pl.BlockSpec((nt, RET_DK // 2), lambda h, i: (0, 0)),
                  pl.BlockSpec((nt, RET_DK // 2), lambda h, i: (0, 0)),
                  pl.BlockSpec((None, 1, RET_DV), lambda h, i: (h, 0, 0)),
                  pl.BlockSpec((bt, None, RET_DK, RET_DV), lambda h, i: (i, h, 0, 0))],
        out_specs=(pl.BlockSpec((nt, bt, RET_DV), lambda h, i: (0, i, h)),
                   pl.BlockSpec((bt, None, RET_DK, RET_DV), lambda h, i: (i, h, 0, 0))),
        out_shape=(jax.ShapeDtypeStruct((nt, nb, RET_V), BF16),
                   jax.ShapeDtypeStruct(state.shape, F32)),
        compiler_params=_cparams("parallel", "parallel"),
        name="ret_decode",
    )(pw, pr, pr, pr, pr, cos, sin, ng, state)


def kernel(x_prompt, x_sample, state_gla, state_ret, ffn1_norm, ffn1_w1, ffn1_w3, ffn1_w2, mix_norm, w_in, w_gate_up, b_gate, gla_norm, w_gla_o, ret_norm, w_ret_o, w_out, ffn2_norm, ffn2_w1, ffn2_w3, ffn2_w2, final_norm):
    depth = w_in.shape[0]
    batch, seq, d = x_prompt.shape
    nb, nt, _ = x_sample.shape

    hp = x_prompt.reshape(batch * seq, d)
    hs = x_sample.transpose(1, 0, 2).reshape(nt * nb, d)
    gla_p, ret_p, gla_s, ret_s = [], [], [], []
    for l in range(depth):
        last = l == depth - 1
        wl = w_in[l].T
        w_ga = w_in[l][:, WA_WIDTH:WB_START].astype(BF16)
        wgu = w_gate_up[l].astype(BF16)
        bg = b_gate[l].reshape(1, GLA_QK)
        ng = gla_norm[l].reshape(GLA_HEADS, 1, GLA_DV)
        nr = ret_norm[l].reshape(RET_HEADS, 1, RET_DV)
        g_next = final_norm if last else ffn1_norm[l + 1]

        hs, us, *f1 = _ffn(hs, ffn1_norm[l], ffn1_w1[l], ffn1_w3[l], ffn1_w2[l], mix_norm[l], final=False)
        hp, up = _ffn(hp, ffn1_norm[l], *f1, mix_norm[l], final=False)

        pg, w_a = _proj(us, wl, 0, WA_WIDTH)
        pr, w_b = _proj(us, wl, WB_START, WB_ZA)
        la = _gate(us, w_ga, wgu, bg)
        og_s, sg_s = _gla_decode(pg.reshape(nt, nb, -1), la.reshape(nt, nb, -1), state_gla[l], ng, nt, nb)
        or_s, sr_s = _ret_decode(pr.reshape(nt, nb, -1), state_ret[l], nr, nt, nb)
        og, sg = _gla_prompt(up, batch, seq, w_a, w_ga, wgu, bg, ng)
        orr, sr = _ret_prompt(up, batch, seq, w_b, nr)
        gla_s.append(sg_s)
        ret_s.append(sr_s)
        gla_p.append(sg)
        ret_p.append(sr)

        ms, wza, wzb, wgo, wro = _merge_gate(us, og_s.reshape(nt * nb, GLA_V), or_s.reshape(nt * nb, RET_V),
                                             wl, w_gla_o[l], w_ret_o[l])
        mp, = _merge_gate(up, og, orr, (wza, wzb), wgo, wro)
        hs, wout = _out_proj(hs, ms, w_out[l])
        hp, = _out_proj(hp, mp, wout)

        hs, *f2 = _ffn(hs, ffn2_norm[l], ffn2_w1[l], ffn2_w3[l], ffn2_w2[l], g_next, final=last)
        hp, *_ = _ffn(hp, ffn2_norm[l], *f2[-3:], g_next, final=last)

    y_prompt = hp.reshape(batch, seq, d)
    y_sample = hs.reshape(nt, nb, d).transpose(1, 0, 2)
    return (y_prompt, y_sample, jnp.stack(gla_p), jnp.stack(ret_p), jnp.stack(gla_s), jnp.stack(ret_s))
```

```python
import functools

import jax
import jax.numpy as jnp
from jax import lax
from jax.experimental import pallas as pl
from jax.experimental.pallas import tpu as pltpu

F32, BF16 = jnp.float32, jnp.bfloat16

D_MODEL = 2048
PAST_LEN = 16384
GLA_HEADS = 4
GLA_DK = D_MODEL // (2 * GLA_HEADS)
GLA_DV = D_MODEL // GLA_HEADS
GLA_GATE_RANK = 16
GLA_GATE_TAU = 16.0
RET_HEADS = 8
RET_DK = D_MODEL // RET_HEADS
RET_DV = 2 * D_MODEL // RET_HEADS
ROPE_BASE = 10000.0
EPS = 1e-6
GLA_QK = GLA_HEADS * GLA_DK
GLA_V = GLA_HEADS * GLA_DV
RET_QK = RET_HEADS * RET_DK
RET_V = RET_HEADS * RET_DV

WA_Q, WA_K, WA_V, WA_G = 0, GLA_QK, 2 * GLA_QK, 2 * GLA_QK + GLA_V
WA_WIDTH = 2 * GLA_QK + 2 * GLA_V
WB_Q, WB_K, WB_V, WB_G = 0, RET_QK, 2 * RET_QK, 2 * RET_QK + RET_V
WB_ZA = 2 * RET_QK + 2 * RET_V
WB_ZB = WB_ZA + D_MODEL
WB_START = WA_WIDTH + GLA_GATE_RANK

LANES = 128
SUBLANES = 8
VMEM_LIMIT = 58 * 2**20

GLA_CHUNK = 64
RET_CHUNK = 256
SEQ_TILE = 512


def _cparams(*sem):
    return pltpu.CompilerParams(dimension_semantics=sem, vmem_limit_bytes=VMEM_LIMIT)


def _dot(a, b):
    return jnp.dot(a, b, preferred_element_type=F32)


def _dot_tb(a, b):
    return lax.dot_general(a, b, (((1,), (1,)), ((), ())), preferred_element_type=F32)


def _dot_ta(a, b):
    return lax.dot_general(a, b, (((0,), (0,)), ((), ())), preferred_element_type=F32)


def _rms(x, g):
    return x * lax.rsqrt(jnp.mean(x * x, axis=-1, keepdims=True) + EPS) * g


def _sigmoid(x):
    return 1.0 / (1.0 + jnp.exp(-x))


def _silu(x):
    return x * _sigmoid(x)


def _log_sigmoid(x):
    return jnp.minimum(x, 0.0) - jnp.log1p(jnp.exp(-jnp.abs(x)))


def _lane_replicated_column(row):
    return jnp.broadcast_to(row, (LANES, row.shape[-1])).T


def _ffn_body(x_ref, g_ref, w1_ref, w3_ref, w2_ref, g2_ref, *refs, tm, rows, final, emit_w):
    refs = list(refs)
    u_scr = refs.pop()
    wb_refs = [refs.pop() for _ in range(3)][::-1] if emit_w else None
    if final:
        y_ref, = refs
        acc_ref = y_ref
    else:
        h_ref, un_ref = refs
        acc_ref = h_ref
    j = pl.program_id(1)

    @pl.when(j == 0)
    def _():
        for r in range(0, tm, rows):
            u_scr[r:r + rows, :] = _rms(x_ref[r:r + rows, :], g_ref[...]).astype(BF16)
            acc_ref[r:r + rows, :] = jnp.zeros((rows, acc_ref.shape[1]), F32)

    w1, w3, w2 = w1_ref[...].astype(BF16), w3_ref[...].astype(BF16), w2_ref[...].astype(BF16)
    if emit_w:
        for ref, w in zip(wb_refs, (w1, w3, w2)):
            ref[...] = w
    u = u_scr[...]
    a = _dot(u, w1)
    b = _dot(u, w3)
    acc_ref[...] += _dot((_silu(a) * b).astype(BF16), w2)

    @pl.when(j == pl.num_programs(1) - 1)
    def _():
        for r in range(0, tm, rows):
            h = x_ref[r:r + rows, :] + 0.5 * acc_ref[r:r + rows, :]
            if final:
                y_ref[r:r + rows, :] = _rms(h, g2_ref[...])
            else:
                h_ref[r:r + rows, :] = h
                un_ref[r:r + rows, :] = _rms(h, g2_ref[...]).astype(BF16)


def _ffn(x, g, w1, w3, w2, g2, *, final, tm=512):
    t, d = x.shape
    dff = w1.shape[1]
    tm = min(tm, t)
    emit_w = w1.dtype == F32
    assert not emit_w or t == tm, "weights are emitted by a single-token-tile call"
    tf = 256 if emit_w else 512
    grid = (t // tm, dff // tf)
    row = pl.BlockSpec((tm, d), lambda i, j: (i, 0))
    vec = pl.BlockSpec((1, d), lambda i, j: (0, 0))
    w_up = pl.BlockSpec((d, tf), lambda i, j: (0, j))
    w_dn = pl.BlockSpec((tf, d), lambda i, j: (j, 0))
    out_shape = [jax.ShapeDtypeStruct((t, d), F32)] + ([] if final else [jax.ShapeDtypeStruct((t, d), BF16)])
    out_specs = [row] * len(out_shape)
    if emit_w:
        out_shape += [jax.ShapeDtypeStruct(w.shape, BF16) for w in (w1, w3, w2)]
        out_specs += [w_up, w_up, w_dn]
    return pl.pallas_call(
        functools.partial(_ffn_body, tm=tm, rows=min(256, tm), final=final, emit_w=emit_w),
        grid=grid,
        in_specs=[row, vec, w_up, w_up, w_dn, vec],
        out_specs=out_specs,
        out_shape=out_shape,
        scratch_shapes=[pltpu.VMEM((tm, d), BF16)],
        compiler_params=_cparams("parallel", "arbitrary"),
        name=("ffn_final" if final else "ffn_mid") + ("_castw" if emit_w else ""),
    )(x, g.reshape(1, d), w1, w3, w2, g2.reshape(1, d))


def _w_cols(a_ref, b_ref):
    rows = a_ref[...] if b_ref is None else jnp.concatenate([a_ref[b_ref.shape[0]:, :], b_ref[...]], axis=0)
    return rows.T.astype(BF16)


def _w_cols_specs(width, tn, start, col_of):
    shift = start % tn
    base = start - shift
    if shift == 0:
        return [pl.BlockSpec((tn, width), lambda *g: (base // tn + col_of(*g), 0))]
    assert shift % SUBLANES == 0 and tn % shift == 0 and base % shift == 0
    return [pl.BlockSpec((tn, width), lambda *g: (base // tn + col_of(*g), 0)),
            pl.BlockSpec((shift, width), lambda *g: ((base + tn * (col_of(*g) + 1)) // shift, 0))]


def _merge_body(u_ref, og_ref, or_ref, *refs, emit_w):
    u = u_ref[...]
    if not emit_w:
        wza_ref, wzb_ref, wgo_ref, wro_ref, m_ref = refs
        wza, wzb, wgo, wro = wza_ref[...], wzb_ref[...], wgo_ref[...], wro_ref[...]
    else:
        za_a, za_b, zb_a, zb_b, wgo_ref, wro_ref, m_ref, wza_o, wzb_o, wgo_o, wro_o = refs
        wza = _w_cols(za_a, za_b)
        wzb = _w_cols(zb_a, zb_b)
        wgo, wro = wgo_ref[...].astype(BF16), wro_ref[...].astype(BF16)
        wza_o[...], wzb_o[...], wgo_o[...], wro_o[...] = wza, wzb, wgo, wro
    za = _dot(u, wza)
    zb = _dot(u, wzb)
    branch_a = _dot(og_ref[...].astype(BF16), wgo)
    branch_b = _dot(or_ref[...].astype(BF16), wro)
    m_ref[...] = (_sigmoid(za) * branch_a + _sigmoid(zb) * branch_b).astype(BF16)


def _merge_gate(u, og, orr, wz, wgo, wro, *, tm=512):
    t, d = u.shape
    tm = min(tm, t)
    emit_w = wgo.dtype == F32
    assert not emit_w or t == tm, "weights are emitted by a single-token-tile call"
    tn = 256 if emit_w else 512
    col = lambda i, j: j
    acts = [pl.BlockSpec((tm, d), lambda i, j: (i, 0)),
            pl.BlockSpec((tm, GLA_V), lambda i, j: (i, 0)),
            pl.BlockSpec((tm, RET_V), lambda i, j: (i, 0))]
    w_col = lambda rows: pl.BlockSpec((rows, tn), lambda i, j: (0, j))
    out_shape = [jax.ShapeDtypeStruct((t, d), BF16)]
    out_specs = [pl.BlockSpec((tm, tn), lambda i, j: (i, j))]
    if emit_w:
        w_specs = (_w_cols_specs(d, tn, WB_START + WB_ZA, col) + _w_cols_specs(d, tn, WB_START + WB_ZB, col)
                   + [w_col(GLA_V), w_col(RET_V)])
        w_args = (wz, wz, wz, wz, wgo, wro)
        out_shape += [jax.ShapeDtypeStruct((d, d), BF16), jax.ShapeDtypeStruct((d, d), BF16),
                      jax.ShapeDtypeStruct(wgo.shape, BF16), jax.ShapeDtypeStruct(wro.shape, BF16)]
        out_specs += [w_col(d), w_col(d), w_col(GLA_V), w_col(RET_V)]
    else:
        w_specs, w_args = [w_col(d), w_col(d), w_col(GLA_V), w_col(RET_V)], (*wz, wgo, wro)
    return pl.pallas_call(
        functools.partial(_merge_body, emit_w=emit_w),
        grid=(t // tm, d // tn),
        in_specs=acts + w_specs,
        out_specs=out_specs,
        out_shape=out_shape,
        compiler_params=_cparams("parallel", "parallel"),
        name="merge_gate" + ("_castw" if emit_w else ""),
    )(u, og, orr, *w_args)


def _out_body(h_ref, m_ref, wout_ref, o_ref, *wb_ref):
    wout = wout_ref[...].astype(BF16)
    if wb_ref:
        wb_ref[0][...] = wout
    o_ref[...] = h_ref[...] + _dot(m_ref[...], wout)


def _out_proj(h, merged, wout, *, tm=512):
    t, d = h.shape
    tm = min(tm, t)
    emit_w = wout.dtype == F32
    assert not emit_w or t == tm, "weights are emitted by a single-token-tile call"
    tn = 512 if emit_w else d
    out_shape = [jax.ShapeDtypeStruct((t, d), F32)] + ([jax.ShapeDtypeStruct((d, d), BF16)] if emit_w else [])
    out_specs = [pl.BlockSpec((tm, tn), lambda i, j: (i, j))] + ([pl.BlockSpec((d, tn), lambda i, j: (0, j))] if emit_w else [])
    return pl.pallas_call(
        _out_body,
        grid=(t // tm, d // tn),
        in_specs=[pl.BlockSpec((tm, tn), lambda i, j: (i, j)),
                  pl.BlockSpec((tm, d), lambda i, j: (i, 0)),
                  pl.BlockSpec((d, tn), lambda i, j: (0, j))],
        out_specs=out_specs,
        out_shape=out_shape,
        compiler_params=_cparams("parallel", "parallel"),
        name="out_proj" + ("_castw" if emit_w else ""),
    )(h, merged, wout)


def _gla_body(u_ref, wq_ref, wk_ref, wv_ref, wg_ref, wga_ref, wgu_ref, bg_ref, ng_ref,
              dq_ref, dk_ref, dv_ref, dg_ref, dla_ref, ds_ref,
              og_ref, st_ref, dog_ref, dso_ref, s_scr, *, ts, c, nt, bt):
    s = pl.program_id(2)

    @pl.when(s == 0)
    def _():
        s_scr[...] = jnp.zeros_like(s_scr)

    dec = _gla_decode_prep(dq_ref, dk_ref, dv_ref, dla_ref, ds_ref, dso_ref, nt=nt, bt=bt)
    dec.units(0, bt)
    dec.finish(dg_ref, ng_ref, dog_ref)

    u = u_ref[...]
    ga = _dot(u, wga_ref[...])
    logit = _dot(ga.astype(BF16), wgu_ref[...]) + bg_ref[...]
    la_all = _log_sigmoid(logit) / GLA_GATE_TAU
    la_hi = la_all.astype(BF16)
    la_lo = (la_all - la_hi.astype(F32)).astype(BF16)
    q_all = _dot(u, wq_ref[...]) * (GLA_DK ** -0.5)
    k_all = _dot(u, wk_ref[...])
    v_all = _dot(u, wv_ref[...]).astype(BF16)
    g_all = _dot(u, wg_ref[...])

    rr = lax.broadcasted_iota(jnp.int32, (c, c), 0)
    cc = lax.broadcasted_iota(jnp.int32, (c, c), 1)
    causal = rr >= cc
    tril = causal.astype(BF16)
    mid = c // 2
    rows = [slice(ci * c, (ci + 1) * c) for ci in range(ts // c)]

    b = [_dot(tril, la_hi[r]) + _dot(tril, la_lo[r]) for r in rows]
    q_rel, k_rel, q_dec, k_end, decay = [], [], [], [], []
    for r, bc in zip(rows, b):
        q, k = q_all[r], k_all[r]
        b_mid, b_last = bc[mid:mid + 1, :], bc[c - 1:c, :]
        q_rel.append((q * jnp.exp(bc - b_mid)).astype(BF16))
        k_rel.append((k * jnp.exp(b_mid - bc)).astype(BF16))
        q_dec.append((q * jnp.exp(bc)).astype(BF16))
        k_end.append((k * jnp.exp(b_last - bc)).astype(BF16))
        dcol = _lane_replicated_column(jnp.exp(b_last))
        decay.append(jnp.concatenate([dcol] * (GLA_DV // LANES), axis=1))
    att = [jnp.where(causal, _dot_tb(qr, kr), 0.0).astype(BF16) for qr, kr in zip(q_rel, k_rel)]
    o_intra = [_dot(a, v_all[r]) for a, r in zip(att, rows)]
    upd = [_dot_ta(ke, v_all[r]) for ke, r in zip(k_end, rows)]
    st = s_scr[...]
    for ci, r in enumerate(rows):
        o = o_intra[ci] + _dot(q_dec[ci], st.astype(BF16))
        og_ref[r, :] = (_rms(o, ng_ref[...]) * _silu(g_all[r])).astype(BF16)
        st = decay[ci] * st + upd[ci]
    s_scr[...] = st

    @pl.when(s == pl.num_programs(2) - 1)
    def _():
        st_ref[...] = st


def _decode_tile(batch, ns, nb):
    bt = nb // (batch * ns)
    assert bt * batch * ns == nb and bt % SUBLANES == 0
    return bt


def _gla_mixer(u, batch, seq, w_a, wga, wgu, bg, ng, pg, la, state):
    ts, c = SEQ_TILE, GLA_CHUNK
    ns = seq // ts
    d = u.shape[1]
    nt, nb, _ = pg.shape
    bt = _decode_tile(batch, ns, nb)
    kq, kk, kv, kg = WA_Q // GLA_DK, WA_K // GLA_DK, WA_V // GLA_DV, WA_G // GLA_DV
    dec = lambda width, col0: pl.BlockSpec((nt, bt, width), lambda b, h, s: (0, b * ns + s, col0 + h))
    st_spec = pl.BlockSpec((bt, None, GLA_DK, GLA_DV), lambda b, h, s: (b * ns + s, h, 0, 0))
    return pl.pallas_call(
        functools.partial(_gla_body, ts=ts, c=c, nt=nt, bt=bt),
        grid=(batch, GLA_HEADS, ns),
        in_specs=[pl.BlockSpec((ts, d), lambda b, h, s: (b * ns + s, 0)),
                  pl.BlockSpec((d, GLA_DK), lambda b, h, s: (0, kq + h)),
                  pl.BlockSpec((d, GLA_DK), lambda b, h, s: (0, kk + h)),
                  pl.BlockSpec((d, GLA_DV), lambda b, h, s: (0, kv + h)),
                  pl.BlockSpec((d, GLA_DV), lambda b, h, s: (0, kg + h)),
                  pl.BlockSpec((d, GLA_GATE_RANK), lambda b, h, s: (0, 0)),
                  pl.BlockSpec((GLA_GATE_RANK, GLA_DK), lambda b, h, s: (0, h)),
                  pl.BlockSpec((1, GLA_DK), lambda b, h, s: (0, h)),
                  pl.BlockSpec((None, 1, GLA_DV), lambda b, h, s: (h, 0, 0)),
                  dec(GLA_DK, kq), dec(GLA_DK, kk), dec(GLA_DV, kv), dec(GLA_DV, kg), dec(GLA_DK, 0), st_spec],
        out_specs=(pl.BlockSpec((ts, GLA_DV), lambda b, h, s: (b * ns + s, h)),
                   pl.BlockSpec((None, None, GLA_DK, GLA_DV), lambda b, h, s: (b, h, 0, 0)),
                   dec(GLA_DV, 0), st_spec),
        out_shape=(jax.ShapeDtypeStruct((batch * seq, GLA_V), BF16),
                   jax.ShapeDtypeStruct((batch, GLA_HEADS, GLA_DK, GLA_DV), F32),
                   jax.ShapeDtypeStruct((nt, nb, GLA_V), F32),
                   jax.ShapeDtypeStruct(state.shape, F32)),
        scratch_shapes=[pltpu.VMEM((GLA_DK, GLA_DV), F32)],
        compiler_params=_cparams("parallel", "parallel", "arbitrary"),
        name="gla_mixer",
    )(u, w_a, w_a, w_a, w_a, wga, wgu, bg, ng, pg, pg, pg, pg, la, state)


def _rope(x, cos, sin):
    half = x.shape[-1] // 2
    x1, x2 = x[:, :half], x[:, half:]
    return jnp.concatenate([x1 * cos - x2 * sin, x2 * cos + x1 * sin], axis=-1)


def _ret_body(u_ref, wq_ref, wk_ref, wv_ref, wg_ref, cos_ref, sin_ref, dm_ref, qd_ref, kd_ref, cd_ref, ng_ref,
              pw_ref, dq_ref, dk_ref, dv_ref, dg_ref, dcos_ref, dsin_ref, ds_ref,
              or_ref, st_ref, dor_ref, dso_ref, s_scr, *, ts, c, nt, bt):
    s = pl.program_id(2)

    @pl.when(s == 0)
    def _():
        s_scr[...] = jnp.zeros_like(s_scr)

    dec = _ret_decode_prep(pw_ref, dq_ref, dk_ref, dv_ref, dcos_ref, dsin_ref, ds_ref, dso_ref, nt=nt, bt=bt)
    cut = [bt * n // 4 for n in range(5)]

    u = u_ref[...]
    cos, sin = cos_ref[...], sin_ref[...]
    q_all = _rope(_dot(u, wq_ref[...]), cos, sin)
    dec.units(cut[0], cut[1])
    k_all = _rope(_dot(u, wk_ref[...]), cos, sin) * (RET_DK ** -0.5)
    dec.units(cut[1], cut[2])
    v_all = _dot(u, wv_ref[...]).astype(BF16)
    dec.units(cut[2], cut[3])
    g_all = _dot(u, wg_ref[...])
    dec.units(cut[3], cut[4])
    dec.finish(dg_ref, ng_ref, dor_ref)

    rows = [slice(ci * c, (ci + 1) * c) for ci in range(ts // c)]
    att = [(_dot_tb(q_all[r].astype(BF16), k_all[r].astype(BF16)) * dm_ref[...]).astype(BF16) for r in rows]
    o_intra = [_dot(a, v_all[r]) for a, r in zip(att, rows)]
    upd = [_dot_ta((k_all[r] * kd_ref[...]).astype(BF16), v_all[r]) for r in rows]
    q_dec = [(q_all[r] * qd_ref[...]).astype(BF16) for r in rows]
    st = s_scr[...]
    for ci, r in enumerate(rows):
        o = o_intra[ci] + _dot(q_dec[ci], st.astype(BF16))
        or_ref[r, :] = (_rms(o, ng_ref[...]) * _silu(g_all[r])).astype(BF16)
        st = cd_ref[...] * st + upd[ci]
    s_scr[...] = st

    @pl.when(s == pl.num_programs(2) - 1)
    def _():
        st_ref[...] = st


def _ret_log_gamma():
    return jnp.log1p(-jnp.exp2(-5.0 - jnp.arange(RET_HEADS, dtype=F32)))


def _rope_tables(pos):
    half = RET_DK // 2
    freqs = ROPE_BASE ** (-jnp.arange(half, dtype=F32) / half)
    ang = pos[:, None] * freqs[None, :]
    return jnp.cos(ang), jnp.sin(ang)


def _ret_mixer(u, batch, seq, w_b, ng, pr, state):
    ts, c = SEQ_TILE, RET_CHUNK
    ns = seq // ts
    d = u.shape[1]
    nt, nb, _ = pr.shape
    bt = _decode_tile(batch, ns, nb)
    kq, kk, kv, kg = WB_Q // RET_DK, WB_K // RET_DK, WB_V // RET_DV, WB_G // RET_DV
    cos, sin = _rope_tables(jnp.arange(seq, dtype=F32))
    dcos, dsin = _rope_tables(PAST_LEN + jnp.arange(nt, dtype=F32))
    pw = jnp.exp(_ret_log_gamma()[:, None] * jnp.arange(nt + 1, dtype=F32)[None, :])
    dec = lambda width, col0: pl.BlockSpec((nt, bt, width), lambda b, h, s: (0, b * ns + s, col0 + h))
    rope_spec = pl.BlockSpec((nt, RET_DK // 2), lambda b, h, s: (0, 0))
    st_spec = pl.BlockSpec((bt, None, RET_DK, RET_DV), lambda b, h, s: (b * ns + s, h, 0, 0))
    lg = _ret_log_gamma()
    idx = jnp.arange(c, dtype=F32)
    diff = idx[:, None] - idx[None, :]
    dmat = jnp.where(diff >= 0, jnp.exp(lg[:, None, None] * jnp.maximum(diff, 0.0)), 0.0)
    qdec = jnp.broadcast_to(jnp.exp(lg[:, None] * (idx + 1.0)[None, :])[:, :, None], (RET_HEADS, c, RET_DK))
    kdec = jnp.broadcast_to(jnp.exp(lg[:, None] * (c - 1.0 - idx)[None, :])[:, :, None], (RET_HEADS, c, RET_DK))
    cdec = jnp.broadcast_to(jnp.exp(lg * c)[:, None, None], (RET_HEADS, 1, RET_DV))
    return pl.pallas_call(
        functools.partial(_ret_body, ts=ts, c=c, nt=nt, bt=bt),
        grid=(batch, RET_HEADS, ns),
        in_specs=[pl.BlockSpec((ts, d), lambda b, h, s: (b * ns + s, 0)),
                  pl.BlockSpec((d, RET_DK), lambda b, h, s: (0, kq + h)),
                  pl.BlockSpec((d, RET_DK), lambda b, h, s: (0, kk + h)),
                  pl.BlockSpec((d, RET_DV), lambda b, h, s: (0, kv + h)),
                  pl.BlockSpec((d, RET_DV), lambda b, h, s: (0, kg + h)),
                  pl.BlockSpec((ts, RET_DK // 2), lambda b, h, s: (s, 0)),
                  pl.BlockSpec((ts, RET_DK // 2), lambda b, h, s: (s, 0)),
                  pl.BlockSpec((None, c, c), lambda b, h, s: (h, 0, 0)),
                  pl.BlockSpec((None, c, RET_DK), lambda b, h, s: (h, 0, 0)),
                  pl.BlockSpec((None, c, RET_DK), lambda b, h, s: (h, 0, 0)),
                  pl.BlockSpec((None, 1, RET_DV), lambda b, h, s: (h, 0, 0)),
                  pl.BlockSpec((None, 1, RET_DV), lambda b, h, s: (h, 0, 0)),
                  pl.BlockSpec(memory_space=pltpu.SMEM),
                  dec(RET_DK, kq), dec(RET_DK, kk), dec(RET_DV, kv), dec(RET_DV, kg), rope_spec, rope_spec, st_spec],
        out_specs=(pl.BlockSpec((ts, RET_DV), lambda b, h, s: (b * ns + s, h)),
                   pl.BlockSpec((None, None, RET_DK, RET_DV), lambda b, h, s: (b, h, 0, 0)),
                   dec(RET_DV, 0), st_spec),
        out_shape=(jax.ShapeDtypeStruct((batch * seq, RET_V), BF16),
                   jax.ShapeDtypeStruct((batch, RET_HEADS, RET_DK, RET_DV), F32),
                   jax.ShapeDtypeStruct((nt, nb, RET_V), F32),
                   jax.ShapeDtypeStruct(state.shape, F32)),
        scratch_shapes=[pltpu.VMEM((RET_DK, RET_DV), F32)],
        compiler_params=_cparams("parallel", "parallel", "arbitrary"),
        name="ret_mixer",
    )(u, w_b, w_b, w_b, w_b, cos, sin, dmat, qdec, kdec, cdec, ng, pw, pr, pr, pr, pr, dcos, dsin, state)


def _proj_body(u_ref, *refs):
    *w_refs, o_ref, wb_ref = refs
    w = _w_cols(w_refs[0], w_refs[1] if len(w_refs) > 1 else None)
    wb_ref[...] = w
    o_ref[...] = _dot(u_ref[...], w)


def _proj(u, w_t, start, n, *, tn=512):
    t, d = u.shape
    w_specs = _w_cols_specs(d, tn, start, lambda j: j)
    return pl.pallas_call(
        _proj_body,
        grid=(n // tn,),
        in_specs=[pl.BlockSpec((t, d), lambda j: (0, 0))] + w_specs,
        out_specs=[pl.BlockSpec((t, tn), lambda j: (0, j)), pl.BlockSpec((d, tn), lambda j: (0, j))],
        out_shape=[jax.ShapeDtypeStruct((t, n), F32), jax.ShapeDtypeStruct((d, n), BF16)],
        compiler_params=_cparams("parallel"),
        name="sample_proj_castw",
    )(u, *([w_t] * len(w_specs)))


def _gate_body(u_ref, wga_ref, wgu_ref, bg_ref, o_ref):
    ga = _dot(u_ref[...], wga_ref[...])
    logit = _dot(ga.astype(BF16), wgu_ref[...]) + bg_ref[...]
    o_ref[...] = _log_sigmoid(logit) / GLA_GATE_TAU


def _gate(u, wga, wgu, bg):
    t = u.shape[0]
    return pl.pallas_call(
        _gate_body,
        out_shape=jax.ShapeDtypeStruct((t, GLA_QK), F32),
        compiler_params=pltpu.CompilerParams(vmem_limit_bytes=VMEM_LIMIT),
        name="sample_gate",
    )(u, wga, wgu, bg)


def _row_group_mask(rows, cols, bt, i):
    return (lax.broadcasted_iota(jnp.int32, (rows, cols), 0) % bt) == i


class _Decode:
    def __init__(self, intra, q_dec, k_end, v_all, new_state, s_ref, so_ref, nt, bt):
        self.intra, self.q_dec, self.k_end, self.v_all = intra, q_dec, k_end, v_all
        self.new_state, self.s_ref, self.so_ref, self.nt, self.bt = new_state, s_ref, so_ref, nt, bt
        self.inter = jnp.zeros((nt * bt, v_all.shape[1]), F32)

    def units(self, lo, hi):
        rows, bt = self.nt * self.bt, self.bt
        for i in range(lo, hi):
            st = self.s_ref[i]
            qs = _dot(self.q_dec, st.astype(BF16))
            self.inter = self.inter + jnp.where(_row_group_mask(rows, qs.shape[1], bt, i), qs, 0.0)
            k_i = jnp.where(_row_group_mask(rows, self.k_end.shape[1], bt, i), self.k_end, 0.0).astype(BF16)
            self.so_ref[i] = self.new_state(i, st, _dot_ta(k_i, self.v_all))

    def finish(self, g_ref, ng_ref, o_ref):
        bt = self.bt
        for t in range(self.nt):
            o = self.intra[t] + self.inter[t * bt:(t + 1) * bt, :]
            o_ref[t] = _rms(o, ng_ref[...]) * _silu(g_ref[t])


def _gla_decode_prep(q_ref, k_ref, v_ref, la_ref, s_ref, so_ref, *, nt, bt):
    q = [q_ref[t] * (GLA_DK ** -0.5) for t in range(nt)]
    k = [k_ref[t] for t in range(nt)]
    v = [v_ref[t] for t in range(nt)]
    b = [la_ref[0]]
    for t in range(1, nt):
        b.append(b[-1] + la_ref[t])
    b_mid, b_last = b[nt // 2], b[nt - 1]
    q_rel = [q[t] * jnp.exp(b[t] - b_mid) for t in range(nt)]
    k_rel = [k[t] * jnp.exp(b_mid - b[t]) for t in range(nt)]
    intra = []
    for t in range(nt):
        acc = None
        for s in range(t + 1):
            term = jnp.sum(q_rel[t] * k_rel[s], axis=-1, keepdims=True) * v[s]
            acc = term if acc is None else acc + term
        intra.append(acc)
    q_dec = jnp.concatenate([q[t] * jnp.exp(b[t]) for t in range(nt)], axis=0).astype(BF16)
    k_end = jnp.concatenate([k[t] * jnp.exp(b_last - b[t]) for t in range(nt)], axis=0)
    v_all = jnp.concatenate(v, axis=0).astype(BF16)
    decay = jnp.exp(b_last)

    def new_state(i, st, upd):
        dcol = _lane_replicated_column(decay[i:i + 1, :])
        return jnp.concatenate([dcol] * (GLA_DV // LANES), axis=1) * st + upd

    return _Decode(intra, q_dec, k_end, v_all, new_state, s_ref, so_ref, nt, bt)


def _ret_decode_prep(pw_ref, q_ref, k_ref, v_ref, cos_ref, sin_ref, s_ref, so_ref, *, nt, bt):
    h = pl.program_id(1)
    pw = [pw_ref[h, n] for n in range(nt + 1)]
    q = [_rope(q_ref[t], cos_ref[t:t + 1, :], sin_ref[t:t + 1, :]) for t in range(nt)]
    k = [_rope(k_ref[t], cos_ref[t:t + 1, :], sin_ref[t:t + 1, :]) * (RET_DK ** -0.5) for t in range(nt)]
    v = [v_ref[t] for t in range(nt)]
    intra = []
    for t in range(nt):
        acc = None
        for s in range(t + 1):
            term = (jnp.sum(q[t] * k[s], axis=-1, keepdims=True) * pw[t - s]) * v[s]
            acc = term if acc is None else acc + term
        intra.append(acc)
    q_dec = jnp.concatenate([q[t] * pw[t + 1] for t in range(nt)], axis=0).astype(BF16)
    k_end = jnp.concatenate([k[t] * pw[nt - 1 - t] for t in range(nt)], axis=0)
    v_all = jnp.concatenate(v, axis=0).astype(BF16)
    return _Decode(intra, q_dec, k_end, v_all, lambda i, st, upd: pw[nt] * st + upd, s_ref, so_ref, nt, bt)


def kernel(x_prompt, x_sample, state_gla, state_ret, ffn1_norm, ffn1_w1, ffn1_w3, ffn1_w2, mix_norm, w_in, w_gate_up, b_gate, gla_norm, w_gla_o, ret_norm, w_ret_o, w_out, ffn2_norm, ffn2_w1, ffn2_w3, ffn2_w2, final_norm):
    depth = w_in.shape[0]
    batch, seq, d = x_prompt.shape
    nb, nt, _ = x_sample.shape

    hp = x_prompt.reshape(batch * seq, d)
    hs = x_sample.transpose(1, 0, 2).reshape(nt * nb, d)
    gla_p, ret_p, gla_s, ret_s = [], [], [], []
    for l in range(depth):
        last = l == depth - 1
        wl = w_in[l].T
        w_ga = w_in[l][:, WA_WIDTH:WB_START].astype(BF16)
        wgu = w_gate_up[l].astype(BF16)
        bg = b_gate[l].reshape(1, GLA_QK)
        ng = gla_norm[l].reshape(GLA_HEADS, 1, GLA_DV)
        nr = ret_norm[l].reshape(RET_HEADS, 1, RET_DV)
        g_next = final_norm if last else ffn1_norm[l + 1]

        hs, us, *f1 = _ffn(hs, ffn1_norm[l], ffn1_w1[l], ffn1_w3[l], ffn1_w2[l], mix_norm[l], final=False)
        hp, up = _ffn(hp, ffn1_norm[l], *f1, mix_norm[l], final=False)

        pg, w_a = _proj(us, wl, 0, WA_WIDTH)
        pr, w_b = _proj(us, wl, WB_START, WB_ZA)
        la = _gate(us, w_ga, wgu, bg)
        og, sg, og_s, sg_s = _gla_mixer(up, batch, seq, w_a, w_ga, wgu, bg, ng,
                                        pg.reshape(nt, nb, -1), la.reshape(nt, nb, -1), state_gla[l])
        orr, sr, or_s, sr_s = _ret_mixer(up, batch, seq, w_b, nr, pr.reshape(nt, nb, -1), state_ret[l])
        gla_s.append(sg_s)
        ret_s.append(sr_s)
        gla_p.append(sg)
        ret_p.append(sr)

        ms, wza, wzb, wgo, wro = _merge_gate(us, og_s.reshape(nt * nb, GLA_V), or_s.reshape(nt * nb, RET_V),
                                             wl, w_gla_o[l], w_ret_o[l])
        mp, = _merge_gate(up, og, orr, (wza, wzb), wgo, wro)
        hs, wout = _out_proj(hs, ms, w_out[l])
        hp, = _out_proj(hp, mp, wout)

        hs, *f2 = _ffn(hs, ffn2_norm[l], ffn2_w1[l], ffn2_w3[l], ffn2_w2[l], g_next, final=last)
        hp, *_ = _ffn(hp, ffn2_norm[l], *f2[-3:], g_next, final=last)

    y_prompt = hp.reshape(batch, seq, d)
    y_sample = hs.reshape(nt, nb, d).transpose(1, 0, 2)
    return (y_prompt, y_sample, jnp.stack(gla_p), jnp.stack(ret_p), jnp.stack(gla_s), jnp.stack(ret_s))
```

```python
import functools

import jax
import jax.numpy as jnp
from jax import lax
from jax.experimental import pallas as pl
from jax.experimental.pallas import tpu as pltpu

F32, BF16 = jnp.float32, jnp.bfloat16

D_MODEL = 2048
PAST_LEN = 16384
GLA_HEADS = 4
GLA_DK = D_MODEL // (2 * GLA_HEADS)
GLA_DV = D_MODEL // GLA_HEADS
GLA_GATE_RANK = 16
GLA_GATE_TAU = 16.0
RET_HEADS = 8
RET_DK = D_MODEL // RET_HEADS
RET_DV = 2 * D_MODEL // RET_HEADS
ROPE_BASE = 10000.0
EPS = 1e-6
GLA_QK = GLA_HEADS * GLA_DK
GLA_V = GLA_HEADS * GLA_DV
RET_QK = RET_HEADS * RET_DK
RET_V = RET_HEADS * RET_DV

WA_Q, WA_K, WA_V, WA_G = 0, GLA_QK, 2 * GLA_QK, 2 * GLA_QK + GLA_V
WA_WIDTH = 2 * GLA_QK + 2 * GLA_V
WB_Q, WB_K, WB_V, WB_G = 0, RET_QK, 2 * RET_QK, 2 * RET_QK + RET_V
WB_ZA = 2 * RET_QK + 2 * RET_V
WB_ZB = WB_ZA + D_MODEL
WB_START = WA_WIDTH + GLA_GATE_RANK

LANES = 128
SUBLANES = 8
VMEM_LIMIT = 58 * 2**20

GLA_CHUNK = 64
RET_CHUNK = 256
SEQ_TILE = 512


def _cparams(*sem):
    return pltpu.CompilerParams(dimension_semantics=sem, vmem_limit_bytes=VMEM_LIMIT)


def _dot(a, b):
    return jnp.dot(a, b, preferred_element_type=F32)


def _dot_tb(a, b):
    return lax.dot_general(a, b, (((1,), (1,)), ((), ())), preferred_element_type=F32)


def _dot_ta(a, b):
    return lax.dot_general(a, b, (((0,), (0,)), ((), ())), preferred_element_type=F32)


def _rms(x, g):
    return x * lax.rsqrt(jnp.mean(x * x, axis=-1, keepdims=True) + EPS) * g


def _sigmoid(x):
    return 1.0 / (1.0 + jnp.exp(-x))


def _silu(x):
    return x * _sigmoid(x)


def _log_sigmoid(x):
    return jnp.minimum(x, 0.0) - jnp.log1p(jnp.exp(-jnp.abs(x)))


def _lane_replicated_column(row):
    return jnp.broadcast_to(row, (LANES, row.shape[-1])).T


def _ffn_body(x_ref, g_ref, w1_ref, w3_ref, w2_ref, g2_ref, *refs, tm, rows, final, emit_w):
    refs = list(refs)
    u_scr = refs.pop()
    wb_refs = [refs.pop() for _ in range(3)][::-1] if emit_w else None
    if final:
        y_ref, = refs
        acc_ref = y_ref
    else:
        h_ref, un_ref = refs
        acc_ref = h_ref
    j = pl.program_id(1)

    @pl.when(j == 0)
    def _():
        for r in range(0, tm, rows):
            u_scr[r:r + rows, :] = _rms(x_ref[r:r + rows, :], g_ref[...]).astype(BF16)
            acc_ref[r:r + rows, :] = jnp.zeros((rows, acc_ref.shape[1]), F32)

    w1, w3, w2 = w1_ref[...].astype(BF16), w3_ref[...].astype(BF16), w2_ref[...].astype(BF16)
    if emit_w:
        for ref, w in zip(wb_refs, (w1, w3, w2)):
            ref[...] = w
    u = u_scr[...]
    a = _dot(u, w1)
    b = _dot(u, w3)
    acc_ref[...] += _dot((_silu(a) * b).astype(BF16), w2)

    @pl.when(j == pl.num_programs(1) - 1)
    def _():
        for r in range(0, tm, rows):
            h = x_ref[r:r + rows, :] + 0.5 * acc_ref[r:r + rows, :]
            if final:
                y_ref[r:r + rows, :] = _rms(h, g2_ref[...])
            else:
                h_ref[r:r + rows, :] = h
                un_ref[r:r + rows, :] = _rms(h, g2_ref[...]).astype(BF16)


def _ffn(x, g, w1, w3, w2, g2, *, final, tm=512):
    t, d = x.shape
    dff = w1.shape[1]
    tm = min(tm, t)
    emit_w = w1.dtype == F32
    assert not emit_w or t == tm, "weights are emitted by a single-token-tile call"
    tf = 256 if emit_w else 512
    grid = (t // tm, dff // tf)
    row = pl.BlockSpec((tm, d), lambda i, j: (i, 0))
    vec = pl.BlockSpec((1, d), lambda i, j: (0, 0))
    w_up = pl.BlockSpec((d, tf), lambda i, j: (0, j))
    w_dn = pl.BlockSpec((tf, d), lambda i, j: (j, 0))
    out_shape = [jax.ShapeDtypeStruct((t, d), F32)] + ([] if final else [jax.ShapeDtypeStruct((t, d), BF16)])
    out_specs = [row] * len(out_shape)
    if emit_w:
        out_shape += [jax.ShapeDtypeStruct(w.shape, BF16) for w in (w1, w3, w2)]
        out_specs += [w_up, w_up, w_dn]
    return pl.pallas_call(
        functools.partial(_ffn_body, tm=tm, rows=min(256, tm), final=final, emit_w=emit_w),
        grid=grid,
        in_specs=[row, vec, w_up, w_up, w_dn, vec],
        out_specs=out_specs,
        out_shape=out_shape,
        scratch_shapes=[pltpu.VMEM((tm, d), BF16)],
        compiler_params=_cparams("parallel", "arbitrary"),
        name=("ffn_final" if final else "ffn_mid") + ("_castw" if emit_w else ""),
    )(x, g.reshape(1, d), w1, w3, w2, g2.reshape(1, d))


def _w_cols(a_ref, b_ref):
    rows = a_ref[...] if b_ref is None else jnp.concatenate([a_ref[b_ref.shape[0]:, :], b_ref[...]], axis=0)
    return rows.T.astype(BF16)


def _w_cols_specs(width, tn, start, col_of):
    shift = start % tn
    base = start - shift
    if shift == 0:
        return [pl.BlockSpec((tn, width), lambda *g: (base // tn + col_of(*g), 0))]
    assert shift % SUBLANES == 0 and tn % shift == 0 and base % shift == 0
    return [pl.BlockSpec((tn, width), lambda *g: (base // tn + col_of(*g), 0)),
            pl.BlockSpec((shift, width), lambda *g: ((base + tn * (col_of(*g) + 1)) // shift, 0))]


def _merge_body(u_ref, og_ref, or_ref, *refs, emit_w):
    u = u_ref[...]
    if not emit_w:
        wza_ref, wzb_ref, wgo_ref, wro_ref, m_ref = refs
        wza, wzb, wgo, wro = wza_ref[...], wzb_ref[...], wgo_ref[...], wro_ref[...]
    else:
        za_a, za_b, zb_a, zb_b, wgo_ref, wro_ref, m_ref, wza_o, wzb_o, wgo_o, wro_o = refs
        wza = _w_cols(za_a, za_b)
        wzb = _w_cols(zb_a, zb_b)
        wgo, wro = wgo_ref[...].astype(BF16), wro_ref[...].astype(BF16)
        wza_o[...], wzb_o[...], wgo_o[...], wro_o[...] = wza, wzb, wgo, wro
    za = _dot(u, wza)
    zb = _dot(u, wzb)
    branch_a = _dot(og_ref[...].astype(BF16), wgo)
    branch_b = _dot(or_ref[...].astype(BF16), wro)
    m_ref[...] = (_sigmoid(za) * branch_a + _sigmoid(zb) * branch_b).astype(BF16)


def _merge_gate(u, og, orr, wz, wgo, wro, *, tm=512):
    t, d = u.shape
    tm = min(tm, t)
    emit_w = wgo.dtype == F32
    assert not emit_w or t == tm, "weights are emitted by a single-token-tile call"
    tn = 256 if emit_w else 512
    col = lambda i, j: j
    acts = [pl.BlockSpec((tm, d), lambda i, j: (i, 0)),
            pl.BlockSpec((tm, GLA_V), lambda i, j: (i, 0)),
            pl.BlockSpec((tm, RET_V), lambda i, j: (i, 0))]
    w_col = lambda rows: pl.BlockSpec((rows, tn), lambda i, j: (0, j))
    out_shape = [jax.ShapeDtypeStruct((t, d), BF16)]
    out_specs = [pl.BlockSpec((tm, tn), lambda i, j: (i, j))]
    if emit_w:
        w_specs = (_w_cols_specs(d, tn, WB_START + WB_ZA, col) + _w_cols_specs(d, tn, WB_START + WB_ZB, col)
                   + [w_col(GLA_V), w_col(RET_V)])
        w_args = (wz, wz, wz, wz, wgo, wro)
        out_shape += [jax.ShapeDtypeStruct((d, d), BF16), jax.ShapeDtypeStruct((d, d), BF16),
                      jax.ShapeDtypeStruct(wgo.shape, BF16), jax.ShapeDtypeStruct(wro.shape, BF16)]
        out_specs += [w_col(d), w_col(d), w_col(GLA_V), w_col(RET_V)]
    else:
        w_specs, w_args = [w_col(d), w_col(d), w_col(GLA_V), w_col(RET_V)], (*wz, wgo, wro)
    return pl.pallas_call(
        functools.partial(_merge_body, emit_w=emit_w),
        grid=(t // tm, d // tn),
        in_specs=acts + w_specs,
        out_specs=out_specs,
        out_shape=out_shape,
        compiler_params=_cparams("parallel", "parallel"),
        name="merge_gate" + ("_castw" if emit_w else ""),
    )(u, og, orr, *w_args)


def _out_body(h_ref, m_ref, wout_ref, o_ref, *wb_ref):
    wout = wout_ref[...].astype(BF16)
    if wb_ref:
        wb_ref[0][...] = wout
    o_ref[...] = h_ref[...] + _dot(m_ref[...], wout)


def _out_proj(h, merged, wout, *, tm=512):
    t, d = h.shape
    tm = min(tm, t)
    emit_w = wout.dtype == F32
    assert not emit_w or t == tm, "weights are emitted by a single-token-tile call"
    tn = 512 if emit_w else d
    out_shape = [jax.ShapeDtypeStruct((t, d), F32)] + ([jax.ShapeDtypeStruct((d, d), BF16)] if emit_w else [])
    out_specs = [pl.BlockSpec((tm, tn), lambda i, j: (i, j))] + ([pl.BlockSpec((d, tn), lambda i, j: (0, j))] if emit_w else [])
    return pl.pallas_call(
        _out_body,
        grid=(t // tm, d // tn),
        in_specs=[pl.BlockSpec((tm, tn), lambda i, j: (i, j)),
                  pl.BlockSpec((tm, d), lambda i, j: (i, 0)),
                  pl.BlockSpec((d, tn), lambda i, j: (0, j))],
        out_specs=out_specs,
        out_shape=out_shape,
        compiler_params=_cparams("parallel", "parallel"),
        name="out_proj" + ("_castw" if emit_w else ""),
    )(h, merged, wout)


def _gla_body(u_ref, w_ref, la_ref, ng_ref,
              dq_ref, dk_ref, dv_ref, dg_ref, dla_ref, ds_ref,
              og_ref, st_ref, dog_ref, dso_ref, s_scr, *, ts, c, nt, bt):
    s = pl.program_id(2)

    @pl.when(s == 0)
    def _():
        s_scr[...] = jnp.zeros_like(s_scr)

    dec = _gla_decode_prep(dq_ref, dk_ref, dv_ref, dla_ref, ds_ref, dso_ref, nt=nt, bt=bt)
    dec.units(0, bt)
    dec.finish(dg_ref, ng_ref, dog_ref)

    qkvg = _dot(u_ref[...], w_ref[...])
    q_all = qkvg[:, :GLA_DK] * (GLA_DK ** -0.5)
    k_all = qkvg[:, GLA_DK:2 * GLA_DK]
    v_all = qkvg[:, 2 * GLA_DK:2 * GLA_DK + GLA_DV].astype(BF16)
    g_all = qkvg[:, 2 * GLA_DK + GLA_DV:]
    la_all = la_ref[...]
    la_hi = la_all.astype(BF16)
    la_lo = (la_all - la_hi.astype(F32)).astype(BF16)

    rr = lax.broadcasted_iota(jnp.int32, (c, c), 0)
    cc = lax.broadcasted_iota(jnp.int32, (c, c), 1)
    causal = rr >= cc
    tril = causal.astype(BF16)
    mid = c // 2
    rows = [slice(ci * c, (ci + 1) * c) for ci in range(ts // c)]

    b = [_dot(tril, la_hi[r]) + _dot(tril, la_lo[r]) for r in rows]
    q_rel, k_rel, q_dec, k_end, decay = [], [], [], [], []
    for r, bc in zip(rows, b):
        q, k = q_all[r], k_all[r]
        b_mid, b_last = bc[mid:mid + 1, :], bc[c - 1:c, :]
        q_rel.append((q * jnp.exp(bc - b_mid)).astype(BF16))
        k_rel.append((k * jnp.exp(b_mid - bc)).astype(BF16))
        q_dec.append((q * jnp.exp(bc)).astype(BF16))
        k_end.append((k * jnp.exp(b_last - bc)).astype(BF16))
        dcol = _lane_replicated_column(jnp.exp(b_last))
        decay.append(jnp.concatenate([dcol] * (GLA_DV // LANES), axis=1))
    att = [jnp.where(causal, _dot_tb(qr, kr), 0.0).astype(BF16) for qr, kr in zip(q_rel, k_rel)]
    o_intra = [_dot(a, v_all[r]) for a, r in zip(att, rows)]
    upd = [_dot_ta(ke, v_all[r]) for ke, r in zip(k_end, rows)]
    st = s_scr[...]
    for ci, r in enumerate(rows):
        o = o_intra[ci] + _dot(q_dec[ci], st.astype(BF16))
        og_ref[r, :] = (_rms(o, ng_ref[...]) * _silu(g_all[r])).astype(BF16)
        st = decay[ci] * st + upd[ci]
    s_scr[...] = st

    @pl.when(s == pl.num_programs(2) - 1)
    def _():
        st_ref[...] = st


def _decode_tile(batch, ns, nb):
    bt = nb // (batch * ns)
    assert bt * batch * ns == nb and bt % SUBLANES == 0
    return bt


def _gla_mixer(u, la_p, batch, seq, w_pack, ng, pg, la, state):
    assert HEAD_PACK * GLA_DK == 2 * GLA_DK + 2 * GLA_DV
    ts, c = SEQ_TILE, GLA_CHUNK
    ns = seq // ts
    d = u.shape[1]
    nt, nb, _ = pg.shape
    bt = _decode_tile(batch, ns, nb)
    kq, kk, kv, kg = WA_Q // GLA_DK, WA_K // GLA_DK, WA_V // GLA_DV, WA_G // GLA_DV
    dec = lambda width, col0: pl.BlockSpec((nt, bt, width), lambda b, h, s: (0, b * ns + s, col0 + h))
    st_spec = pl.BlockSpec((bt, None, GLA_DK, GLA_DV), lambda b, h, s: (b * ns + s, h, 0, 0))
    return pl.pallas_call(
        functools.partial(_gla_body, ts=ts, c=c, nt=nt, bt=bt),
        grid=(batch, GLA_HEADS, ns),
        in_specs=[pl.BlockSpec((ts, d), lambda b, h, s: (b * ns + s, 0)),
                  pl.BlockSpec((d, HEAD_PACK * GLA_DK), lambda b, h, s: (0, h)),
                  pl.BlockSpec((ts, GLA_DK), lambda b, h, s: (b * ns + s, h)),
                  pl.BlockSpec((None, 1, GLA_DV), lambda b, h, s: (h, 0, 0)),
                  dec(GLA_DK, kq), dec(GLA_DK, kk), dec(GLA_DV, kv), dec(GLA_DV, kg), dec(GLA_DK, 0), st_spec],
        out_specs=(pl.BlockSpec((ts, GLA_DV), lambda b, h, s: (b * ns + s, h)),
                   pl.BlockSpec((None, None, GLA_DK, GLA_DV), lambda b, h, s: (b, h, 0, 0)),
                   dec(GLA_DV, 0), st_spec),
        out_shape=(jax.ShapeDtypeStruct((batch * seq, GLA_V), BF16),
                   jax.ShapeDtypeStruct((batch, GLA_HEADS, GLA_DK, GLA_DV), F32),
                   jax.ShapeDtypeStruct((nt, nb, GLA_V), F32),
                   jax.ShapeDtypeStruct(state.shape, F32)),
        scratch_shapes=[pltpu.VMEM((GLA_DK, GLA_DV), F32)],
        compiler_params=_cparams("parallel", "parallel", "arbitrary"),
        name="gla_mixer",
    )(u, w_pack, la_p, ng, pg, pg, pg, pg, la, state)


def _rope(x, cos, sin):
    half = x.shape[-1] // 2
    x1, x2 = x[:, :half], x[:, half:]
    return jnp.concatenate([x1 * cos - x2 * sin, x2 * cos + x1 * sin], axis=-1)


def _ret_body(u_ref, w_ref, cos_ref, sin_ref, dm_ref, qd_ref, kd_ref, cd_ref, ng_ref,
              pw_ref, dq_ref, dk_ref, dv_ref, dg_ref, dcos_ref, dsin_ref, ds_ref,
              or_ref, st_ref, dor_ref, dso_ref, s_scr, *, ts, c, nt, bt):
    s = pl.program_id(2)

    @pl.when(s == 0)
    def _():
        s_scr[...] = jnp.zeros_like(s_scr)

    dec = _ret_decode_prep(pw_ref, dq_ref, dk_ref, dv_ref, dcos_ref, dsin_ref, ds_ref, dso_ref, nt=nt, bt=bt)
    dec.units(0, bt)
    dec.finish(dg_ref, ng_ref, dor_ref)

    qkvg = _dot(u_ref[...], w_ref[...])
    cos, sin = cos_ref[...], sin_ref[...]
    q_all = _rope(qkvg[:, :RET_DK], cos, sin)
    k_all = _rope(qkvg[:, RET_DK:2 * RET_DK], cos, sin) * (RET_DK ** -0.5)
    v_all = qkvg[:, 2 * RET_DK:2 * RET_DK + RET_DV].astype(BF16)
    g_all = qkvg[:, 2 * RET_DK + RET_DV:]

    rows = [slice(ci * c, (ci + 1) * c) for ci in range(ts // c)]
    att = [(_dot_tb(q_all[r].astype(BF16), k_all[r].astype(BF16)) * dm_ref[...]).astype(BF16) for r in rows]
    o_intra = [_dot(a, v_all[r]) for a, r in zip(att, rows)]
    upd = [_dot_ta((k_all[r] * kd_ref[...]).astype(BF16), v_all[r]) for r in rows]
    q_dec = [(q_all[r] * qd_ref[...]).astype(BF16) for r in rows]
    st = s_scr[...]
    for ci, r in enumerate(rows):
        o = o_intra[ci] + _dot(q_dec[ci], st.astype(BF16))
        or_ref[r, :] = (_rms(o, ng_ref[...]) * _silu(g_all[r])).astype(BF16)
        st = cd_ref[...] * st + upd[ci]
    s_scr[...] = st

    @pl.when(s == pl.num_programs(2) - 1)
    def _():
        st_ref[...] = st


def _ret_log_gamma():
    return jnp.log1p(-jnp.exp2(-5.0 - jnp.arange(RET_HEADS, dtype=F32)))


def _rope_tables(pos):
    half = RET_DK // 2
    freqs = ROPE_BASE ** (-jnp.arange(half, dtype=F32) / half)
    ang = pos[:, None] * freqs[None, :]
    return jnp.cos(ang), jnp.sin(ang)


def _ret_mixer(u, batch, seq, w_pack, ng, pr, state):
    ts, c = SEQ_TILE, RET_CHUNK
    ns = seq // ts
    d = u.shape[1]
    nt, nb, _ = pr.shape
    bt = _decode_tile(batch, ns, nb)
    kq, kk, kv, kg = WB_Q // RET_DK, WB_K // RET_DK, WB_V // RET_DV, WB_G // RET_DV
    cos, sin = _rope_tables(jnp.arange(seq, dtype=F32))
    dcos, dsin = _rope_tables(PAST_LEN + jnp.arange(nt, dtype=F32))
    pw = jnp.exp(_ret_log_gamma()[:, None] * jnp.arange(nt + 1, dtype=F32)[None, :])
    dec = lambda width, col0: pl.BlockSpec((nt, bt, width), lambda b, h, s: (0, b * ns + s, col0 + h))
    rope_spec = pl.BlockSpec((nt, RET_DK // 2), lambda b, h, s: (0, 0))
    st_spec = pl.BlockSpec((bt, None, RET_DK, RET_DV), lambda b, h, s: (b * ns + s, h, 0, 0))
    lg = _ret_log_gamma()
    idx = jnp.arange(c, dtype=F32)
    diff = idx[:, None] - idx[None, :]
    dmat = jnp.where(diff >= 0, jnp.exp(lg[:, None, None] * jnp.maximum(diff, 0.0)), 0.0)
    qdec = jnp.broadcast_to(jnp.exp(lg[:, None] * (idx + 1.0)[None, :])[:, :, None], (RET_HEADS, c, RET_DK))
    kdec = jnp.broadcast_to(jnp.exp(lg[:, None] * (c - 1.0 - idx)[None, :])[:, :, None], (RET_HEADS, c, RET_DK))
    cdec = jnp.broadcast_to(jnp.exp(lg * c)[:, None, None], (RET_HEADS, 1, RET_DV))
    return pl.pallas_call(
        functools.partial(_ret_body, ts=ts, c=c, nt=nt, bt=bt),
        grid=(batch, RET_HEADS, ns),
        in_specs=[pl.BlockSpec((ts, d), lambda b, h, s: (b * ns + s, 0)),
                  pl.BlockSpec((d, HEAD_PACK * RET_DK), lambda b, h, s: (0, h)),
                  pl.BlockSpec((ts, RET_DK // 2), lambda b, h, s: (s, 0)),
                  pl.BlockSpec((ts, RET_DK // 2), lambda b, h, s: (s, 0)),
                  pl.BlockSpec((None, c, c), lambda b, h, s: (h, 0, 0)),
                  pl.BlockSpec((None, c, RET_DK), lambda b, h, s: (h, 0, 0)),
                  pl.BlockSpec((None, c, RET_DK), lambda b, h, s: (h, 0, 0)),
                  pl.BlockSpec((None, 1, RET_DV), lambda b, h, s: (h, 0, 0)),
                  pl.BlockSpec((None, 1, RET_DV), lambda b, h, s: (h, 0, 0)),
                  pl.BlockSpec(memory_space=pltpu.SMEM),
                  dec(RET_DK, kq), dec(RET_DK, kk), dec(RET_DV, kv), dec(RET_DV, kg), rope_spec, rope_spec, st_spec],
        out_specs=(pl.BlockSpec((ts, RET_DV), lambda b, h, s: (b * ns + s, h)),
                   pl.BlockSpec((None, None, RET_DK, RET_DV), lambda b, h, s: (b, h, 0, 0)),
                   dec(RET_DV, 0), st_spec),
        out_shape=(jax.ShapeDtypeStruct((batch * seq, RET_V), BF16),
                   jax.ShapeDtypeStruct((batch, RET_HEADS, RET_DK, RET_DV), F32),
                   jax.ShapeDtypeStruct((nt, nb, RET_V), F32),
                   jax.ShapeDtypeStruct(state.shape, F32)),
        scratch_shapes=[pltpu.VMEM((RET_DK, RET_DV), F32)],
        compiler_params=_cparams("parallel", "parallel", "arbitrary"),
        name="ret_mixer",
    )(u, w_pack, cos, sin, dmat, qdec, kdec, cdec, ng, pw, pr, pr, pr, pr, dcos, dsin, state)


def _proj_body(u_ref, *refs):
    *w_refs, o_ref, wb_ref = refs
    w = _w_cols(w_refs[0], w_refs[1] if len(w_refs) > 1 else None)
    wb_ref[...] = w
    o_ref[...] = _dot(u_ref[...], w)


HEAD_PACK = 6


def _packed_block(j, heads):
    sec = (j >= heads).astype(jnp.int32) + (j >= 2 * heads) + (j >= 4 * heads)
    first = jnp.where(sec == 0, 0, jnp.where(sec == 1, heads, jnp.where(sec == 2, 2 * heads, 4 * heads)))
    within = j - first
    wide = sec >= 2
    head = jnp.where(wide, within // 2, within)
    slot = jnp.where(sec == 0, 0, jnp.where(sec == 1, 1, jnp.where(sec == 2, 2, 4))) + jnp.where(wide, within % 2, 0)
    return HEAD_PACK * head + slot


def _proj(u, w_t, start, heads, dk, *, tn=256):
    t, d = u.shape
    assert tn == dk
    n = HEAD_PACK * dk * heads
    w_specs = _w_cols_specs(d, tn, start, lambda j: j)
    return pl.pallas_call(
        _proj_body,
        grid=(n // tn,),
        in_specs=[pl.BlockSpec((t, d), lambda j: (0, 0))] + w_specs,
        out_specs=[pl.BlockSpec((t, tn), lambda j: (0, j)),
                   pl.BlockSpec((d, tn), lambda j: (0, _packed_block(j, heads)))],
        out_shape=[jax.ShapeDtypeStruct((t, n), F32), jax.ShapeDtypeStruct((d, n), BF16)],
        compiler_params=_cparams("parallel"),
        name="sample_proj_castw",
    )(u, *([w_t] * len(w_specs)))


def _gate_body(u_ref, wga_ref, wgu_ref, bg_ref, o_ref):
    ga = _dot(u_ref[...], wga_ref[...])
    logit = _dot(ga.astype(BF16), wgu_ref[...]) + bg_ref[...]
    o_ref[...] = _log_sigmoid(logit) / GLA_GATE_TAU


def _gate(u, wga, wgu, bg, *, tm=512):
    t, d = u.shape
    return pl.pallas_call(
        _gate_body,
        grid=(t // tm,),
        in_specs=[pl.BlockSpec((tm, d), lambda i: (i, 0)),
                  pl.BlockSpec(wga.shape, lambda i: (0, 0)),
                  pl.BlockSpec(wgu.shape, lambda i: (0, 0)),
                  pl.BlockSpec(bg.shape, lambda i: (0, 0))],
        out_specs=pl.BlockSpec((tm, GLA_QK), lambda i: (i, 0)),
        out_shape=jax.ShapeDtypeStruct((t, GLA_QK), F32),
        compiler_params=_cparams("parallel"),
        name="log_gate",
    )(u, wga, wgu, bg)


def _row_group_mask(rows, cols, bt, i):
    return (lax.broadcasted_iota(jnp.int32, (rows, cols), 0) % bt) == i


class _Decode:
    def __init__(self, intra, q_dec, k_end, v_all, new_state, s_ref, so_ref, nt, bt):
        self.intra, self.q_dec, self.k_end, self.v_all = intra, q_dec, k_end, v_all
        self.new_state, self.s_ref, self.so_ref, self.nt, self.bt = new_state, s_ref, so_ref, nt, bt
        self.inter = jnp.zeros((nt * bt, v_all.shape[1]), F32)

    def units(self, lo, hi):
        rows, bt = self.nt * self.bt, self.bt
        for i in range(lo, hi):
            st = self.s_ref[i]
            qs = _dot(self.q_dec, st.astype(BF16))
            self.inter = self.inter + jnp.where(_row_group_mask(rows, qs.shape[1], bt, i), qs, 0.0)
            k_i = jnp.where(_row_group_mask(rows, self.k_end.shape[1], bt, i), self.k_end, 0.0).astype(BF16)
            self.so_ref[i] = self.new_state(i, st, _dot_ta(k_i, self.v_all))

    def finish(self, g_ref, ng_ref, o_ref):
        bt = self.bt
        for t in range(self.nt):
            o = self.intra[t] + self.inter[t * bt:(t + 1) * bt, :]
            o_ref[t] = _rms(o, ng_ref[...]) * _silu(g_ref[t])


def _gla_decode_prep(q_ref, k_ref, v_ref, la_ref, s_ref, so_ref, *, nt, bt):
    q = [q_ref[t] * (GLA_DK ** -0.5) for t in range(nt)]
    k = [k_ref[t] for t in range(nt)]
    v = [v_ref[t] for t in range(nt)]
    b = [la_ref[0]]
    for t in range(1, nt):
        b.append(b[-1] + la_ref[t])
    b_mid, b_last = b[nt // 2], b[nt - 1]
    q_rel = [q[t] * jnp.exp(b[t] - b_mid) for t in range(nt)]
    k_rel = [k[t] * jnp.exp(b_mid - b[t]) for t in range(nt)]
    intra = []
    for t in range(nt):
        acc = None
        for s in range(t + 1):
            term = jnp.sum(q_rel[t] * k_rel[s], axis=-1, keepdims=True) * v[s]
            acc = term if acc is None else acc + term
        intra.append(acc)
    q_dec = jnp.concatenate([q[t] * jnp.exp(b[t]) for t in range(nt)], axis=0).astype(BF16)
    k_end = jnp.concatenate([k[t] * jnp.exp(b_last - b[t]) for t in range(nt)], axis=0)
    v_all = jnp.concatenate(v, axis=0).astype(BF16)
    decay = jnp.exp(b_last)

    def new_state(i, st, upd):
        dcol = _lane_replicated_column(decay[i:i + 1, :])
        return jnp.concatenate([dcol] * (GLA_DV // LANES), axis=1) * st + upd

    return _Decode(intra, q_dec, k_end, v_all, new_state, s_ref, so_ref, nt, bt)


def _ret_decode_prep(pw_ref, q_ref, k_ref, v_ref, cos_ref, sin_ref, s_ref, so_ref, *, nt, bt):
    h = pl.program_id(1)
    pw = [pw_ref[h, n] for n in range(nt + 1)]
    q = [_rope(q_ref[t], cos_ref[t:t + 1, :], sin_ref[t:t + 1, :]) for t in range(nt)]
    k = [_rope(k_ref[t], cos_ref[t:t + 1, :], sin_ref[t:t + 1, :]) * (RET_DK ** -0.5) for t in range(nt)]
    v = [v_ref[t] for t in range(nt)]
    intra = []
    for t in range(nt):
        acc = None
        for s in range(t + 1):
            term = (jnp.sum(q[t] * k[s], axis=-1, keepdims=True) * pw[t - s]) * v[s]
            acc = term if acc is None else acc + term
        intra.append(acc)
    q_dec = jnp.concatenate([q[t] * pw[t + 1] for t in range(nt)], axis=0).astype(BF16)
    k_end = jnp.concatenate([k[t] * pw[nt - 1 - t] for t in range(nt)], axis=0)
    v_all = jnp.concatenate(v, axis=0).astype(BF16)
    return _Decode(intra, q_dec, k_end, v_all, lambda i, st, upd: pw[nt] * st + upd, s_ref, so_ref, nt, bt)


def kernel(x_prompt, x_sample, state_gla, state_ret, ffn1_norm, ffn1_w1, ffn1_w3, ffn1_w2, mix_norm, w_in, w_gate_up, b_gate, gla_norm, w_gla_o, ret_norm, w_ret_o, w_out, ffn2_norm, ffn2_w1, ffn2_w3, ffn2_w2, final_norm):
    depth = w_in.shape[0]
    batch, seq, d = x_prompt.shape
    nb, nt, _ = x_sample.shape

    hp = x_prompt.reshape(batch * seq, d)
    hs = x_sample.transpose(1, 0, 2).reshape(nt * nb, d)
    gla_p, ret_p, gla_s, ret_s = [], [], [], []
    for l in range(depth):
        last = l == depth - 1
        wl = w_in[l].T
        w_ga = w_in[l][:, WA_WIDTH:WB_START].astype(BF16)
        wgu = w_gate_up[l].astype(BF16)
        bg = b_gate[l].reshape(1, GLA_QK)
        ng = gla_norm[l].reshape(GLA_HEADS, 1, GLA_DV)
        nr = ret_norm[l].reshape(RET_HEADS, 1, RET_DV)
        g_next = final_norm if last else ffn1_norm[l + 1]

        hs, us, *f1 = _ffn(hs, ffn1_norm[l], ffn1_w1[l], ffn1_w3[l], ffn1_w2[l], mix_norm[l], final=False)
        hp, up = _ffn(hp, ffn1_norm[l], *f1, mix_norm[l], final=False)

        pg, w_a = _proj(us, wl, 0, GLA_HEADS, GLA_DK)
        pr, w_b = _proj(us, wl, WB_START, RET_HEADS, RET_DK)
        la = _gate(us, w_ga, wgu, bg)
        la_p = _gate(up, w_ga, wgu, bg)
        og, sg, og_s, sg_s = _gla_mixer(up, la_p, batch, seq, w_a, ng,
                                        pg.reshape(nt, nb, -1), la.reshape(nt, nb, -1), state_gla[l])
        orr, sr, or_s, sr_s = _ret_mixer(up, batch, seq, w_b, nr, pr.reshape(nt, nb, -1), state_ret[l])
        gla_s.append(sg_s)
        ret_s.append(sr_s)
        gla_p.append(sg)
        ret_p.append(sr)

        ms, wza, wzb, wgo, wro = _merge_gate(us, og_s.reshape(nt * nb, GLA_V), or_s.reshape(nt * nb, RET_V),
                                             wl, w_gla_o[l], w_ret_o[l])
        mp, = _merge_gate(up, og, orr, (wza, wzb), wgo, wro)
        hs, wout = _out_proj(hs, ms, w_out[l])
        hp, = _out_proj(hp, mp, wout)

        hs, *f2 = _ffn(hs, ffn2_norm[l], ffn2_w1[l], ffn2_w3[l], ffn2_w2[l], g_next, final=last)
        hp, *_ = _ffn(hp, ffn2_norm[l], *f2[-3:], g_next, final=last)

    y_prompt = hp.reshape(batch, seq, d)
    y_sample = hs.reshape(nt, nb, d).transpose(1, 0, 2)
    return (y_prompt, y_sample, jnp.stack(gla_p), jnp.stack(ret_p), jnp.stack(gla_s), jnp.stack(ret_s))
```

```python
import functools

import jax
import jax.numpy as jnp
from jax import lax
from jax.experimental import pallas as pl
from jax.experimental.pallas import tpu as pltpu

F32, BF16 = jnp.float32, jnp.bfloat16

D_MODEL = 2048
PAST_LEN = 16384
GLA_HEADS = 4
GLA_DK = D_MODEL // (2 * GLA_HEADS)
GLA_DV = D_MODEL // GLA_HEADS
GLA_GATE_RANK = 16
GLA_GATE_TAU = 16.0
RET_HEADS = 8
RET_DK = D_MODEL // RET_HEADS
RET_DV = 2 * D_MODEL // RET_HEADS
ROPE_BASE = 10000.0
EPS = 1e-6
GLA_QK = GLA_HEADS * GLA_DK
GLA_V = GLA_HEADS * GLA_DV
RET_QK = RET_HEADS * RET_DK
RET_V = RET_HEADS * RET_DV

WA_Q, WA_K, WA_V, WA_G = 0, GLA_QK, 2 * GLA_QK, 2 * GLA_QK + GLA_V
WA_WIDTH = 2 * GLA_QK + 2 * GLA_V
WB_Q, WB_K, WB_V, WB_G = 0, RET_QK, 2 * RET_QK, 2 * RET_QK + RET_V
WB_ZA = 2 * RET_QK + 2 * RET_V
WB_ZB = WB_ZA + D_MODEL
WB_START = WA_WIDTH + GLA_GATE_RANK

LANES = 128
SUBLANES = 8
VMEM_LIMIT = 58 * 2**20

GLA_CHUNK = 64
RET_CHUNK = 256
SEQ_TILE = 512


def _cparams(*sem):
    return pltpu.CompilerParams(dimension_semantics=sem, vmem_limit_bytes=VMEM_LIMIT)


def _dot(a, b):
    return jnp.dot(a, b, preferred_element_type=F32)


def _dot_tb(a, b):
    return lax.dot_general(a, b, (((1,), (1,)), ((), ())), preferred_element_type=F32)


def _dot_ta(a, b):
    return lax.dot_general(a, b, (((0,), (0,)), ((), ())), preferred_element_type=F32)


def _rms(x, g):
    return x * lax.rsqrt(jnp.mean(x * x, axis=-1, keepdims=True) + EPS) * g


def _sigmoid(x):
    return 1.0 / (1.0 + jnp.exp(-x))


def _silu(x):
    return x * _sigmoid(x)


def _log_sigmoid(x):
    return jnp.minimum(x, 0.0) - jnp.log1p(jnp.exp(-jnp.abs(x)))


def _lane_replicated_column(row):
    return jnp.broadcast_to(row, (LANES, row.shape[-1])).T


def _ffn_body(x_ref, g_ref, w1_ref, w3_ref, w2_ref, g2_ref, *refs, tm, rows, final, emit_w):
    refs = list(refs)
    u_scr = refs.pop()
    wb_refs = [refs.pop() for _ in range(3)][::-1] if emit_w else None
    if final:
        y_ref, = refs
        acc_ref = y_ref
    else:
        h_ref, un_ref = refs
        acc_ref = h_ref
    j = pl.program_id(1)

    @pl.when(j == 0)
    def _():
        for r in range(0, tm, rows):
            u_scr[r:r + rows, :] = _rms(x_ref[r:r + rows, :], g_ref[...]).astype(BF16)
            acc_ref[r:r + rows, :] = jnp.zeros((rows, acc_ref.shape[1]), F32)

    w1, w3, w2 = w1_ref[...].astype(BF16), w3_ref[...].astype(BF16), w2_ref[...].astype(BF16)
    if emit_w:
        for ref, w in zip(wb_refs, (w1, w3, w2)):
            ref[...] = w
    u = u_scr[...]
    a = _dot(u, w1)
    b = _dot(u, w3)
    acc_ref[...] += _dot((_silu(a) * b).astype(BF16), w2)

    @pl.when(j == pl.num_programs(1) - 1)
    def _():
        for r in range(0, tm, rows):
            h = x_ref[r:r + rows, :] + 0.5 * acc_ref[r:r + rows, :]
            if final:
                y_ref[r:r + rows, :] = _rms(h, g2_ref[...])
            else:
                h_ref[r:r + rows, :] = h
                un_ref[r:r + rows, :] = _rms(h, g2_ref[...]).astype(BF16)


def _ffn(x, g, w1, w3, w2, g2, *, final, tm=512):
    t, d = x.shape
    dff = w1.shape[1]
    tm = min(tm, t)
    emit_w = w1.dtype == F32
    assert not emit_w or t == tm, "weights are emitted by a single-token-tile call"
    tf = 256 if emit_w else 512
    grid = (t // tm, dff // tf)
    row = pl.BlockSpec((tm, d), lambda i, j: (i, 0))
    vec = pl.BlockSpec((1, d), lambda i, j: (0, 0))
    w_up = pl.BlockSpec((d, tf), lambda i, j: (0, j))
    w_dn = pl.BlockSpec((tf, d), lambda i, j: (j, 0))
    out_shape = [jax.ShapeDtypeStruct((t, d), F32)] + ([] if final else [jax.ShapeDtypeStruct((t, d), BF16)])
    out_specs = [row] * len(out_shape)
    if emit_w:
        out_shape += [jax.ShapeDtypeStruct(w.shape, BF16) for w in (w1, w3, w2)]
        out_specs += [w_up, w_up, w_dn]
    return pl.pallas_call(
        functools.partial(_ffn_body, tm=tm, rows=min(256, tm), final=final, emit_w=emit_w),
        grid=grid,
        in_specs=[row, vec, w_up, w_up, w_dn, vec],
        out_specs=out_specs,
        out_shape=out_shape,
        scratch_shapes=[pltpu.VMEM((tm, d), BF16)],
        compiler_params=_cparams("parallel", "arbitrary"),
        name=("ffn_final" if final else "ffn_mid") + ("_castw" if emit_w else ""),
    )(x, g.reshape(1, d), w1, w3, w2, g2.reshape(1, d))


def _w_cols(a_ref, b_ref):
    rows = a_ref[...] if b_ref is None else jnp.concatenate([a_ref[b_ref.shape[0]:, :], b_ref[...]], axis=0)
    return rows.T.astype(BF16)


def _w_cols_specs(width, tn, start, col_of):
    shift = start % tn
    base = start - shift
    if shift == 0:
        return [pl.BlockSpec((tn, width), lambda *g: (base // tn + col_of(*g), 0))]
    assert shift % SUBLANES == 0 and tn % shift == 0 and base % shift == 0
    return [pl.BlockSpec((tn, width), lambda *g: (base // tn + col_of(*g), 0)),
            pl.BlockSpec((shift, width), lambda *g: ((base + tn * (col_of(*g) + 1)) // shift, 0))]


def _merge_body(u_ref, og_ref, or_ref, *refs, emit_w):
    u = u_ref[...]
    if not emit_w:
        wza_ref, wzb_ref, wgo_ref, wro_ref, m_ref = refs
        wza, wzb, wgo, wro = wza_ref[...], wzb_ref[...], wgo_ref[...], wro_ref[...]
    else:
        za_a, za_b, zb_a, zb_b, wgo_ref, wro_ref, m_ref, wza_o, wzb_o, wgo_o, wro_o = refs
        wza = _w_cols(za_a, za_b)
        wzb = _w_cols(zb_a, zb_b)
        wgo, wro = wgo_ref[...].astype(BF16), wro_ref[...].astype(BF16)
        wza_o[...], wzb_o[...], wgo_o[...], wro_o[...] = wza, wzb, wgo, wro
    za = _dot(u, wza)
    zb = _dot(u, wzb)
    branch_a = _dot(og_ref[...].astype(BF16), wgo)
    branch_b = _dot(or_ref[...].astype(BF16), wro)
    m_ref[...] = (_sigmoid(za) * branch_a + _sigmoid(zb) * branch_b).astype(BF16)


def _merge_gate(u, og, orr, wz, wgo, wro, *, tm=512):
    t, d = u.shape
    tm = min(tm, t)
    emit_w = wgo.dtype == F32
    assert not emit_w or t == tm, "weights are emitted by a single-token-tile call"
    tn = 256 if emit_w else 512
    col = lambda i, j: j
    acts = [pl.BlockSpec((tm, d), lambda i, j: (i, 0)),
            pl.BlockSpec((tm, GLA_V), lambda i, j: (i, 0)),
            pl.BlockSpec((tm, RET_V), lambda i, j: (i, 0))]
    w_col = lambda rows: pl.BlockSpec((rows, tn), lambda i, j: (0, j))
    out_shape = [jax.ShapeDtypeStruct((t, d), BF16)]
    out_specs = [pl.BlockSpec((tm, tn), lambda i, j: (i, j))]
    if emit_w:
        w_specs = (_w_cols_specs(d, tn, WB_START + WB_ZA, col) + _w_cols_specs(d, tn, WB_START + WB_ZB, col)
                   + [w_col(GLA_V), w_col(RET_V)])
        w_args = (wz, wz, wz, wz, wgo, wro)
        out_shape += [jax.ShapeDtypeStruct((d, d), BF16), jax.ShapeDtypeStruct((d, d), BF16),
                      jax.ShapeDtypeStruct(wgo.shape, BF16), jax.ShapeDtypeStruct(wro.shape, BF16)]
        out_specs += [w_col(d), w_col(d), w_col(GLA_V), w_col(RET_V)]
    else:
        w_specs, w_args = [w_col(d), w_col(d), w_col(GLA_V), w_col(RET_V)], (*wz, wgo, wro)
    return pl.pallas_call(
        functools.partial(_merge_body, emit_w=emit_w),
        grid=(t // tm, d // tn),
        in_specs=acts + w_specs,
        out_specs=out_specs,
        out_shape=out_shape,
        compiler_params=_cparams("parallel", "parallel"),
        name="merge_gate" + ("_castw" if emit_w else ""),
    )(u, og, orr, *w_args)


def _out_body(h_ref, m_ref, wout_ref, o_ref, *wb_ref):
    wout = wout_ref[...].astype(BF16)
    if wb_ref:
        wb_ref[0][...] = wout
    o_ref[...] = h_ref[...] + _dot(m_ref[...], wout)


def _out_proj(h, merged, wout, *, tm=512):
    t, d = h.shape
    tm = min(tm, t)
    emit_w = wout.dtype == F32
    assert not emit_w or t == tm, "weights are emitted by a single-token-tile call"
    tn = 512 if emit_w else d
    out_shape = [jax.ShapeDtypeStruct((t, d), F32)] + ([jax.ShapeDtypeStruct((d, d), BF16)] if emit_w else [])
    out_specs = [pl.BlockSpec((tm, tn), lambda i, j: (i, j))] + ([pl.BlockSpec((d, tn), lambda i, j: (0, j))] if emit_w else [])
    return pl.pallas_call(
        _out_body,
        grid=(t // tm, d // tn),
        in_specs=[pl.BlockSpec((tm, tn), lambda i, j: (i, j)),
                  pl.BlockSpec((tm, d), lambda i, j: (i, 0)),
                  pl.BlockSpec((d, tn), lambda i, j: (0, j))],
        out_specs=out_specs,
        out_shape=out_shape,
        compiler_params=_cparams("parallel", "parallel"),
        name="out_proj" + ("_castw" if emit_w else ""),
    )(h, merged, wout)


def _gla_body(u_ref, wq_ref, wk_ref, wv_ref, wg_ref, la_ref, ng_ref,
              dq_ref, dk_ref, dv_ref, dg_ref, dla_ref, ds_ref,
              og_ref, st_ref, dog_ref, dso_ref, s_scr, *, ts, c, nt, bt):
    s = pl.program_id(2)

    @pl.when(s == 0)
    def _():
        s_scr[...] = jnp.zeros_like(s_scr)

    dec = _gla_decode_prep(dq_ref, dk_ref, dv_ref, dla_ref, ds_ref, dso_ref, nt=nt, bt=bt)
    dec.units(0, bt)
    dec.finish(dg_ref, ng_ref, dog_ref)

    u = u_ref[...]
    q_all = _dot(u, wq_ref[...]) * (GLA_DK ** -0.5)
    k_all = _dot(u, wk_ref[...])
    v_all = _dot(u, wv_ref[...]).astype(BF16)
    g_all = _dot(u, wg_ref[...])
    la_all = la_ref[...]
    la_hi = la_all.astype(BF16)
    la_lo = (la_all - la_hi.astype(F32)).astype(BF16)

    rr = lax.broadcasted_iota(jnp.int32, (c, c), 0)
    cc = lax.broadcasted_iota(jnp.int32, (c, c), 1)
    causal = rr >= cc
    tril = causal.astype(BF16)
    mid = c // 2
    rows = [slice(ci * c, (ci + 1) * c) for ci in range(ts // c)]

    b = [_dot(tril, la_hi[r]) + _dot(tril, la_lo[r]) for r in rows]
    q_rel, k_rel, q_dec, k_end, decay = [], [], [], [], []
    for r, bc in zip(rows, b):
        q, k = q_all[r], k_all[r]
        b_mid, b_last = bc[mid:mid + 1, :], bc[c - 1:c, :]
        q_rel.append((q * jnp.exp(bc - b_mid)).astype(BF16))
        k_rel.append((k * jnp.exp(b_mid - bc)).astype(BF16))
        q_dec.append((q * jnp.exp(bc)).astype(BF16))
        k_end.append((k * jnp.exp(b_last - bc)).astype(BF16))
        dcol = _lane_replicated_column(jnp.exp(b_last))
        decay.append(jnp.concatenate([dcol] * (GLA_DV // LANES), axis=1))
    att = [jnp.where(causal, _dot_tb(qr, kr), 0.0).astype(BF16) for qr, kr in zip(q_rel, k_rel)]
    o_intra = [_dot(a, v_all[r]) for a, r in zip(att, rows)]
    upd = [_dot_ta(ke, v_all[r]) for ke, r in zip(k_end, rows)]
    st = s_scr[...]
    for ci, r in enumerate(rows):
        o = o_intra[ci] + _dot(q_dec[ci], st.astype(BF16))
        og_ref[r, :] = (_rms(o, ng_ref[...]) * _silu(g_all[r])).astype(BF16)
        st = decay[ci] * st + upd[ci]
    s_scr[...] = st

    @pl.when(s == pl.num_programs(2) - 1)
    def _():
        st_ref[...] = st


def _decode_tile(batch, ns, nb):
    bt = nb // (batch * ns)
    assert bt * batch * ns == nb and bt % SUBLANES == 0
    return bt


def _gla_mixer(u, la_p, batch, seq, w_a, ng, pg, la, state):
    ts, c = SEQ_TILE, GLA_CHUNK
    ns = seq // ts
    d = u.shape[1]
    nt, nb, _ = pg.shape
    bt = _decode_tile(batch, ns, nb)
    kq, kk, kv, kg = WA_Q // GLA_DK, WA_K // GLA_DK, WA_V // GLA_DV, WA_G // GLA_DV
    dec = lambda width, col0: pl.BlockSpec((nt, bt, width), lambda b, h, s: (0, b * ns + s, col0 + h))
    st_spec = pl.BlockSpec((bt, None, GLA_DK, GLA_DV), lambda b, h, s: (b * ns + s, h, 0, 0))
    return pl.pallas_call(
        functools.partial(_gla_body, ts=ts, c=c, nt=nt, bt=bt),
        grid=(batch, GLA_HEADS, ns),
        in_specs=[pl.BlockSpec((ts, d), lambda b, h, s: (b * ns + s, 0)),
                  pl.BlockSpec((d, GLA_DK), lambda b, h, s: (0, kq + h)),
                  pl.BlockSpec((d, GLA_DK), lambda b, h, s: (0, kk + h)),
                  pl.BlockSpec((d, GLA_DV), lambda b, h, s: (0, kv + h)),
                  pl.BlockSpec((d, GLA_DV), lambda b, h, s: (0, kg + h)),
                  pl.BlockSpec((ts, GLA_DK), lambda b, h, s: (b * ns + s, h)),
                  pl.BlockSpec((None, 1, GLA_DV), lambda b, h, s: (h, 0, 0)),
                  dec(GLA_DK, kq), dec(GLA_DK, kk), dec(GLA_DV, kv), dec(GLA_DV, kg), dec(GLA_DK, 0), st_spec],
        out_specs=(pl.BlockSpec((ts, GLA_DV), lambda b, h, s: (b * ns + s, h)),
                   pl.BlockSpec((None, None, GLA_DK, GLA_DV), lambda b, h, s: (b, h, 0, 0)),
                   dec(GLA_DV, 0), st_spec),
        out_shape=(jax.ShapeDtypeStruct((batch * seq, GLA_V), BF16),
                   jax.ShapeDtypeStruct((batch, GLA_HEADS, GLA_DK, GLA_DV), F32),
                   jax.ShapeDtypeStruct((nt, nb, GLA_V), F32),
                   jax.ShapeDtypeStruct(state.shape, F32)),
        scratch_shapes=[pltpu.VMEM((GLA_DK, GLA_DV), F32)],
        compiler_params=_cparams("parallel", "parallel", "arbitrary"),
        name="gla_mixer",
    )(u, w_a, w_a, w_a, w_a, la_p, ng, pg, pg, pg, pg, la, state)


def _rope(x, cos, sin):
    half = x.shape[-1] // 2
    x1, x2 = x[:, :half], x[:, half:]
    return jnp.concatenate([x1 * cos - x2 * sin, x2 * cos + x1 * sin], axis=-1)


def _ret_body(u_ref, wq_ref, wk_ref, wv_ref, wg_ref, cos_ref, sin_ref, dm_ref, qd_ref, kd_ref, cd_ref, ng_ref,
              pw_ref, dq_ref, dk_ref, dv_ref, dg_ref, dcos_ref, dsin_ref, ds_ref,
              or_ref, st_ref, dor_ref, dso_ref, s_scr, *, ts, c, nt, bt):
    s = pl.program_id(2)

    @pl.when(s == 0)
    def _():
        s_scr[...] = jnp.zeros_like(s_scr)

    dec = _ret_decode_prep(pw_ref, dq_ref, dk_ref, dv_ref, dcos_ref, dsin_ref, ds_ref, dso_ref, nt=nt, bt=bt)
    dec.units(0, bt)
    dec.finish(dg_ref, ng_ref, dor_ref)

    u = u_ref[...]
    tok = pl.ds(pl.multiple_of(s * ts, ts), ts)
    cos, sin = cos_ref[tok, :], sin_ref[tok, :]
    q_all = _rope(_dot(u, wq_ref[...]), cos, sin)
    k_all = _rope(_dot(u, wk_ref[...]), cos, sin) * (RET_DK ** -0.5)
    v_all = _dot(u, wv_ref[...]).astype(BF16)
    g_all = _dot(u, wg_ref[...])

    rows = [slice(ci * c, (ci + 1) * c) for ci in range(ts // c)]
    att = [(_dot_tb(q_all[r].astype(BF16), k_all[r].astype(BF16)) * dm_ref[...]).astype(BF16) for r in rows]
    o_intra = [_dot(a, v_all[r]) for a, r in zip(att, rows)]
    upd = [_dot_ta((k_all[r] * kd_ref[...]).astype(BF16), v_all[r]) for r in rows]
    q_dec = [(q_all[r] * qd_ref[...]).astype(BF16) for r in rows]
    st = s_scr[...]
    for ci, r in enumerate(rows):
        o = o_intra[ci] + _dot(q_dec[ci], st.astype(BF16))
        or_ref[r, :] = (_rms(o, ng_ref[...]) * _silu(g_all[r])).astype(BF16)
        st = cd_ref[...] * st + upd[ci]
    s_scr[...] = st

    @pl.when(s == pl.num_programs(2) - 1)
    def _():
        st_ref[...] = st


def _ret_log_gamma():
    return jnp.log1p(-jnp.exp2(-5.0 - jnp.arange(RET_HEADS, dtype=F32)))


def _rope_tables(pos):
    half = RET_DK // 2
    freqs = ROPE_BASE ** (-jnp.arange(half, dtype=F32) / half)
    ang = pos[:, None] * freqs[None, :]
    return jnp.cos(ang), jnp.sin(ang)


def _ret_mixer(u, batch, seq, w_b, ng, pr, state):
    ts, c = SEQ_TILE, RET_CHUNK
    ns = seq // ts
    d = u.shape[1]
    nt, nb, _ = pr.shape
    bt = _decode_tile(batch, ns, nb)
    kq, kk, kv, kg = WB_Q // RET_DK, WB_K // RET_DK, WB_V // RET_DV, WB_G // RET_DV
    cos, sin = _rope_tables(jnp.arange(seq, dtype=F32))
    dcos, dsin = _rope_tables(PAST_LEN + jnp.arange(nt, dtype=F32))
    pw = jnp.exp(_ret_log_gamma()[:, None] * jnp.arange(nt + 1, dtype=F32)[None, :])
    dec = lambda width, col0: pl.BlockSpec((nt, bt, width), lambda b, h, s: (0, b * ns + s, col0 + h))
    rope_spec = pl.BlockSpec((nt, RET_DK // 2), lambda b, h, s: (0, 0))
    st_spec = pl.BlockSpec((bt, None, RET_DK, RET_DV), lambda b, h, s: (b * ns + s, h, 0, 0))
    lg = _ret_log_gamma()
    idx = jnp.arange(c, dtype=F32)
    diff = idx[:, None] - idx[None, :]
    dmat = jnp.where(diff >= 0, jnp.exp(lg[:, None, None] * jnp.maximum(diff, 0.0)), 0.0)
    qdec = jnp.broadcast_to(jnp.exp(lg[:, None] * (idx + 1.0)[None, :])[:, :, None], (RET_HEADS, c, RET_DK))
    kdec = jnp.broadcast_to(jnp.exp(lg[:, None] * (c - 1.0 - idx)[None, :])[:, :, None], (RET_HEADS, c, RET_DK))
    cdec = jnp.broadcast_to(jnp.exp(lg * c)[:, None, None], (RET_HEADS, 1, RET_DV))
    return pl.pallas_call(
        functools.partial(_ret_body, ts=ts, c=c, nt=nt, bt=bt),
        grid=(batch, RET_HEADS, ns),
        in_specs=[pl.BlockSpec((ts, d), lambda b, h, s: (b * ns + s, 0)),
                  pl.BlockSpec((d, RET_DK), lambda b, h, s: (0, kq + h)),
                  pl.BlockSpec((d, RET_DK), lambda b, h, s: (0, kk + h)),
                  pl.BlockSpec((d, RET_DV), lambda b, h, s: (0, kv + h)),
                  pl.BlockSpec((d, RET_DV), lambda b, h, s: (0, kg + h)),
                  pl.BlockSpec((seq, RET_DK // 2), lambda b, h, s: (0, 0)),
                  pl.BlockSpec((seq, RET_DK // 2), lambda b, h, s: (0, 0)),
                  pl.BlockSpec((None, c, c), lambda b, h, s: (h, 0, 0)),
                  pl.BlockSpec((None, c, RET_DK), lambda b, h, s: (h, 0, 0)),
                  pl.BlockSpec((None, c, RET_DK), lambda b, h, s: (h, 0, 0)),
                  pl.BlockSpec((None, 1, RET_DV), lambda b, h, s: (h, 0, 0)),
                  pl.BlockSpec((None, 1, RET_DV), lambda b, h, s: (h, 0, 0)),
                  pl.BlockSpec(memory_space=pltpu.SMEM),
                  dec(RET_DK, kq), dec(RET_DK, kk), dec(RET_DV, kv), dec(RET_DV, kg), rope_spec, rope_spec, st_spec],
        out_specs=(pl.BlockSpec((ts, RET_DV), lambda b, h, s: (b * ns + s, h)),
                   pl.BlockSpec((None, None, RET_DK, RET_DV), lambda b, h, s: (b, h, 0, 0)),
                   dec(RET_DV, 0), st_spec),
        out_shape=(jax.ShapeDtypeStruct((batch * seq, RET_V), BF16),
                   jax.ShapeDtypeStruct((batch, RET_HEADS, RET_DK, RET_DV), F32),
                   jax.ShapeDtypeStruct((nt, nb, RET_V), F32),
                   jax.ShapeDtypeStruct(state.shape, F32)),
        scratch_shapes=[pltpu.VMEM((RET_DK, RET_DV), F32)],
        compiler_params=_cparams("parallel", "parallel", "arbitrary"),
        name="ret_mixer",
    )(u, w_b, w_b, w_b, w_b, cos, sin, dmat, qdec, kdec, cdec, ng, pw, pr, pr, pr, pr, dcos, dsin, state)


def _proj_body(u_ref, *refs):
    *w_refs, o_ref, wb_ref = refs
    w = _w_cols(w_refs[0], w_refs[1] if len(w_refs) > 1 else None)
    wb_ref[...] = w
    o_ref[...] = _dot(u_ref[...], w)


def _proj(u, w_t, start, n, *, tn=1024):
    t, d = u.shape
    w_specs = _w_cols_specs(d, tn, start, lambda j: j)
    return pl.pallas_call(
        _proj_body,
        grid=(n // tn,),
        in_specs=[pl.BlockSpec((t, d), lambda j: (0, 0))] + w_specs,
        out_specs=[pl.BlockSpec((t, tn), lambda j: (0, j)), pl.BlockSpec((d, tn), lambda j: (0, j))],
        out_shape=[jax.ShapeDtypeStruct((t, n), F32), jax.ShapeDtypeStruct((d, n), BF16)],
        compiler_params=_cparams("parallel"),
        name="sample_proj_castw",
    )(u, *([w_t] * len(w_specs)))


def _gate_body(u_ref, wga_ref, wgu_ref, bg_ref, o_ref):
    ga = _dot(u_ref[...], wga_ref[...])
    logit = _dot(ga.astype(BF16), wgu_ref[...]) + bg_ref[...]
    o_ref[...] = _log_sigmoid(logit) / GLA_GATE_TAU


def _gate(u, wga, wgu, bg, *, tm=512):
    t, d = u.shape
    return pl.pallas_call(
        _gate_body,
        grid=(t // tm,),
        in_specs=[pl.BlockSpec((tm, d), lambda i: (i, 0)),
                  pl.BlockSpec(wga.shape, lambda i: (0, 0)),
                  pl.BlockSpec(wgu.shape, lambda i: (0, 0)),
                  pl.BlockSpec(bg.shape, lambda i: (0, 0))],
        out_specs=pl.BlockSpec((tm, GLA_QK), lambda i: (i, 0)),
        out_shape=jax.ShapeDtypeStruct((t, GLA_QK), F32),
        compiler_params=_cparams("parallel"),
        name="log_gate",
    )(u, wga, wgu, bg)


def _row_group_mask(rows, cols, bt, i):
    return (lax.broadcasted_iota(jnp.int32, (rows, cols), 0) % bt) == i


class _Decode:
    def __init__(self, intra, q_dec, k_end, v_all, new_state, s_ref, so_ref, nt, bt):
        self.intra, self.q_dec, self.k_end, self.v_all = intra, q_dec, k_end, v_all
        self.new_state, self.s_ref, self.so_ref, self.nt, self.bt = new_state, s_ref, so_ref, nt, bt
        self.inter = jnp.zeros((nt * bt, v_all.shape[1]), F32)

    def units(self, lo, hi):
        rows, bt = self.nt * self.bt, self.bt
        for i in range(lo, hi):
            st = self.s_ref[i]
            qs = _dot(self.q_dec, st.astype(BF16))
            self.inter = self.inter + jnp.where(_row_group_mask(rows, qs.shape[1], bt, i), qs, 0.0)
            k_i = jnp.where(_row_group_mask(rows, self.k_end.shape[1], bt, i), self.k_end, 0.0).astype(BF16)
            self.so_ref[i] = self.new_state(i, st, _dot_ta(k_i, self.v_all))

    def finish(self, g_ref, ng_ref, o_ref):
        bt = self.bt
        for t in range(self.nt):
            o = self.intra[t] + self.inter[t * bt:(t + 1) * bt, :]
            o_ref[t] = _rms(o, ng_ref[...]) * _silu(g_ref[t])


def _gla_decode_prep(q_ref, k_ref, v_ref, la_ref, s_ref, so_ref, *, nt, bt):
    q = [q_ref[t] * (GLA_DK ** -0.5) for t in range(nt)]
    k = [k_ref[t] for t in range(nt)]
    v = [v_ref[t] for t in range(nt)]
    b = [la_ref[0]]
    for t in range(1, nt):
        b.append(b[-1] + la_ref[t])
    b_mid, b_last = b[nt // 2], b[nt - 1]
    q_rel = [q[t] * jnp.exp(b[t] - b_mid) for t in range(nt)]
    k_rel = [k[t] * jnp.exp(b_mid - b[t]) for t in range(nt)]
    intra = []
    for t in range(nt):
        acc = None
        for s in range(t + 1):
            term = jnp.sum(q_rel[t] * k_rel[s], axis=-1, keepdims=True) * v[s]
            acc = term if acc is None else acc + term
        intra.append(acc)
    q_dec = jnp.concatenate([q[t] * jnp.exp(b[t]) for t in range(nt)], axis=0).astype(BF16)
    k_end = jnp.concatenate([k[t] * jnp.exp(b_last - b[t]) for t in range(nt)], axis=0)
    v_all = jnp.concatenate(v, axis=0).astype(BF16)
    decay = jnp.exp(b_last)

    def new_state(i, st, upd):
        dcol = _lane_replicated_column(decay[i:i + 1, :])
        return jnp.concatenate([dcol] * (GLA_DV // LANES), axis=1) * st + upd

    return _Decode(intra, q_dec, k_end, v_all, new_state, s_ref, so_ref, nt, bt)


def _ret_decode_prep(pw_ref, q_ref, k_ref, v_ref, cos_ref, sin_ref, s_ref, so_ref, *, nt, bt):
    h = pl.program_id(1)
    pw = [pw_ref[h, n] for n in range(nt + 1)]
    q = [_rope(q_ref[t], cos_ref[t:t + 1, :], sin_ref[t:t + 1, :]) for t in range(nt)]
    k = [_rope(k_ref[t], cos_ref[t:t + 1, :], sin_ref[t:t + 1, :]) * (RET_DK ** -0.5) for t in range(nt)]
    v = [v_ref[t] for t in range(nt)]
    intra = []
    for t in range(nt):
        acc = None
        for s in range(t + 1):
            term = (jnp.sum(q[t] * k[s], axis=-1, keepdims=True) * pw[t - s]) * v[s]
            acc = term if acc is None else acc + term
        intra.append(acc)
    q_dec = jnp.concatenate([q[t] * pw[t + 1] for t in range(nt)], axis=0).astype(BF16)
    k_end = jnp.concatenate([k[t] * pw[nt - 1 - t] for t in range(nt)], axis=0)
    v_all = jnp.concatenate(v, axis=0).astype(BF16)
    return _Decode(intra, q_dec, k_end, v_all, lambda i, st, upd: pw[nt] * st + upd, s_ref, so_ref, nt, bt)


def kernel(x_prompt, x_sample, state_gla, state_ret, ffn1_norm, ffn1_w1, ffn1_w3, ffn1_w2, mix_norm, w_in, w_gate_up, b_gate, gla_norm, w_gla_o, ret_norm, w_ret_o, w_out, ffn2_norm, ffn2_w1, ffn2_w3, ffn2_w2, final_norm):
    depth = w_in.shape[0]
    batch, seq, d = x_prompt.shape
    nb, nt, _ = x_sample.shape

    hp = x_prompt.reshape(batch * seq, d)
    hs = x_sample.transpose(1, 0, 2).reshape(nt * nb, d)
    gla_p, ret_p, gla_s, ret_s = [], [], [], []
    for l in range(depth):
        last = l == depth - 1
        wl = w_in[l].T
        w_ga = w_in[l][:, WA_WIDTH:WB_START].astype(BF16)
        wgu = w_gate_up[l].astype(BF16)
        bg = b_gate[l].reshape(1, GLA_QK)
        ng = gla_norm[l].reshape(GLA_HEADS, 1, GLA_DV)
        nr = ret_norm[l].reshape(RET_HEADS, 1, RET_DV)
        g_next = final_norm if last else ffn1_norm[l + 1]

        hs, us, *f1 = _ffn(hs, ffn1_norm[l], ffn1_w1[l], ffn1_w3[l], ffn1_w2[l], mix_norm[l], final=False)
        hp, up = _ffn(hp, ffn1_norm[l], *f1, mix_norm[l], final=False)

        pg, w_a = _proj(us, wl, 0, WA_WIDTH)
        pr, w_b = _proj(us, wl, WB_START, WB_ZA)
        la = _gate(us, w_ga, wgu, bg)
        la_p = _gate(up, w_ga, wgu, bg)
        og, sg, og_s, sg_s = _gla_mixer(up, la_p, batch, seq, w_a, ng,
                                        pg.reshape(nt, nb, -1), la.reshape(nt, nb, -1), state_gla[l])
        orr, sr, or_s, sr_s = _ret_mixer(up, batch, seq, w_b, nr, pr.reshape(nt, nb, -1), state_ret[l])
        gla_s.append(sg_s)
        ret_s.append(sr_s)
        gla_p.append(sg)
        ret_p.append(sr)

        ms, wza, wzb, wgo, wro = _merge_gate(us, og_s.reshape(nt * nb, GLA_V), or_s.reshape(nt * nb, RET_V),
                                             wl, w_gla_o[l], w_ret_o[l])
        mp, = _merge_gate(up, og, orr, (wza, wzb), wgo, wro)
        hs, wout = _out_proj(hs, ms, w_out[l])
        hp, = _out_proj(hp, mp, wout)

        hs, *f2 = _ffn(hs, ffn2_norm[l], ffn2_w1[l], ffn2_w3[l], ffn2_w2[l], g_next, final=last)
        hp, *_ = _ffn(hp, ffn2_norm[l], *f2[-3:], g_next, final=last)

    y_prompt = hp.reshape(batch, seq, d)
    y_sample = hs.reshape(nt, nb, d).transpose(1, 0, 2)
    return (y_prompt, y_sample, jnp.stack(gla_p), jnp.stack(ret_p), jnp.stack(gla_s), jnp.stack(ret_s))
```

```python
import functools

import jax
import jax.numpy as jnp
from jax import lax
from jax.experimental import pallas as pl
from jax.experimental.pallas import tpu as pltpu

F32, BF16 = jnp.float32, jnp.bfloat16

D_MODEL = 2048
PAST_LEN = 16384
GLA_HEADS = 4
GLA_DK = D_MODEL // (2 * GLA_HEADS)
GLA_DV = D_MODEL // GLA_HEADS
GLA_GATE_RANK = 16
GLA_GATE_TAU = 16.0
RET_HEADS = 8
RET_DK = D_MODEL // RET_HEADS
RET_DV = 2 * D_MODEL // RET_HEADS
ROPE_BASE = 10000.0
EPS = 1e-6
GLA_QK = GLA_HEADS * GLA_DK
GLA_V = GLA_HEADS * GLA_DV
RET_QK = RET_HEADS * RET_DK
RET_V = RET_HEADS * RET_DV

WA_Q, WA_K, WA_V, WA_G = 0, GLA_QK, 2 * GLA_QK, 2 * GLA_QK + GLA_V
WA_WIDTH = 2 * GLA_QK + 2 * GLA_V
WB_Q, WB_K, WB_V, WB_G = 0, RET_QK, 2 * RET_QK, 2 * RET_QK + RET_V
WB_ZA = 2 * RET_QK + 2 * RET_V
WB_ZB = WB_ZA + D_MODEL
WB_START = WA_WIDTH + GLA_GATE_RANK

LANES = 128
SUBLANES = 8
VMEM_LIMIT = 58 * 2**20

GLA_CHUNK = 64
RET_CHUNK = 256
SEQ_TILE = 512


def _cparams(*sem):
    return pltpu.CompilerParams(dimension_semantics=sem, vmem_limit_bytes=VMEM_LIMIT)


def _dot(a, b):
    return jnp.dot(a, b, preferred_element_type=F32)


def _dot_tb(a, b):
    return lax.dot_general(a, b, (((1,), (1,)), ((), ())), preferred_element_type=F32)


def _dot_ta(a, b):
    return lax.dot_general(a, b, (((0,), (0,)), ((), ())), preferred_element_type=F32)


def _rms(x, g):
    return x * lax.rsqrt(jnp.mean(x * x, axis=-1, keepdims=True) + EPS) * g


def _sigmoid(x):
    return 1.0 / (1.0 + jnp.exp(-x))


def _silu(x):
    return x * _sigmoid(x)


def _log_sigmoid(x):
    return jnp.minimum(x, 0.0) - jnp.log1p(jnp.exp(-jnp.abs(x)))


def _lane_replicated_column(row):
    return jnp.broadcast_to(row, (LANES, row.shape[-1])).T


def _ffn_body(x_ref, g_ref, w1_ref, w3_ref, w2_ref, g2_ref, *refs, tm, rows, final, emit_w):
    refs = list(refs)
    u_scr = refs.pop()
    wb_refs = [refs.pop() for _ in range(3)][::-1] if emit_w else None
    if final:
        y_ref, = refs
        acc_ref = y_ref
    else:
        h_ref, un_ref = refs
        acc_ref = h_ref
    j = pl.program_id(1)

    @pl.when(j == 0)
    def _():
        for r in range(0, tm, rows):
            u_scr[r:r + rows, :] = _rms(x_ref[r:r + rows, :], g_ref[...]).astype(BF16)
            acc_ref[r:r + rows, :] = jnp.zeros((rows, acc_ref.shape[1]), F32)

    w1, w3, w2 = w1_ref[...].astype(BF16), w3_ref[...].astype(BF16), w2_ref[...].astype(BF16)
    if emit_w:
        for ref, w in zip(wb_refs, (w1, w3, w2)):
            ref[...] = w
    u = u_scr[...]
    a = _dot(u, w1)
    b = _dot(u, w3)
    acc_ref[...] += _dot((_silu(a) * b).astype(BF16), w2)

    @pl.when(j == pl.num_programs(1) - 1)
    def _():
        for r in range(0, tm, rows):
            h = x_ref[r:r + rows, :] + 0.5 * acc_ref[r:r + rows, :]
            if final:
                y_ref[r:r + rows, :] = _rms(h, g2_ref[...])
            else:
                h_ref[r:r + rows, :] = h
                un_ref[r:r + rows, :] = _rms(h, g2_ref[...]).astype(BF16)


def _ffn(x, g, w1, w3, w2, g2, *, final, tm=512):
    t, d = x.shape
    dff = w1.shape[1]
    tm = min(tm, t)
    emit_w = w1.dtype == F32
    assert not emit_w or t == tm, "weights are emitted by a single-token-tile call"
    tf = 256 if emit_w else 512
    grid = (t // tm, dff // tf)
    row = pl.BlockSpec((tm, d), lambda i, j: (i, 0))
    vec = pl.BlockSpec((1, d), lambda i, j: (0, 0))
    w_up = pl.BlockSpec((d, tf), lambda i, j: (0, j))
    w_dn = pl.BlockSpec((tf, d), lambda i, j: (j, 0))
    out_shape = [jax.ShapeDtypeStruct((t, d), F32)] + ([] if final else [jax.ShapeDtypeStruct((t, d), BF16)])
    out_specs = [row] * len(out_shape)
    if emit_w:
        out_shape += [jax.ShapeDtypeStruct(w.shape, BF16) for w in (w1, w3, w2)]
        out_specs += [w_up, w_up, w_dn]
    return pl.pallas_call(
        functools.partial(_ffn_body, tm=tm, rows=min(256, tm), final=final, emit_w=emit_w),
        grid=grid,
        in_specs=[row, vec, w_up, w_up, w_dn, vec],
        out_specs=out_specs,
        out_shape=out_shape,
        scratch_shapes=[pltpu.VMEM((tm, d), BF16)],
        compiler_params=_cparams("parallel", "arbitrary"),
        name=("ffn_final" if final else "ffn_mid") + ("_castw" if emit_w else ""),
    )(x, g.reshape(1, d), w1, w3, w2, g2.reshape(1, d))


def _ffn_stream_body(x_ref, g_ref, w1_hbm, w3_hbm, w2_hbm, g2_ref, *refs, tm, tf, nj, rows, final):
    *out_refs, u_scr, w1_s, w3_s, w2_s, sem = refs
    i = pl.program_id(0)

    def tile_copies(j, slot):
        cols = pl.ds(pl.multiple_of(j * tf, tf), tf)
        return (pltpu.make_async_copy(w1_hbm.at[:, cols], w1_s.at[slot], sem.at[slot, 0]),
                pltpu.make_async_copy(w3_hbm.at[:, cols], w3_s.at[slot], sem.at[slot, 1]),
                pltpu.make_async_copy(w2_hbm.at[cols, :], w2_s.at[slot], sem.at[slot, 2]))

    def start(j, slot):
        for cp in tile_copies(j, slot):
            cp.start()

    @pl.when(i == 0)
    def _():
        start(0, 0)

    acc_ref = out_refs[0]
    for r in range(0, tm, rows):
        u_scr[r:r + rows, :] = _rms(x_ref[r:r + rows, :], g_ref[...]).astype(BF16)
        acc_ref[r:r + rows, :] = jnp.zeros((rows, acc_ref.shape[1]), F32)

    def step(j, carry):
        slot = lax.rem(i + j, 2)
        for cp in tile_copies(j, slot):
            cp.wait()

        @pl.when(j + 1 < nj)
        def _():
            start(j + 1, 1 - slot)

        @pl.when(jnp.logical_and(j + 1 == nj, i + 1 < pl.num_programs(0)))
        def _():
            start(0, 1 - slot)

        u = u_scr[...]
        a = _dot(u, w1_s[slot])
        b = _dot(u, w3_s[slot])
        acc_ref[...] += _dot((_silu(a) * b).astype(BF16), w2_s[slot])
        return carry

    lax.fori_loop(0, nj, step, 0)

    for r in range(0, tm, rows):
        h = x_ref[r:r + rows, :] + 0.5 * acc_ref[r:r + rows, :]
        if final:
            out_refs[0][r:r + rows, :] = _rms(h, g2_ref[...])
        else:
            out_refs[0][r:r + rows, :] = h
            out_refs[1][r:r + rows, :] = _rms(h, g2_ref[...]).astype(BF16)


def _ffn_stream(x, g, w1, w3, w2, g2, *, final, tm=512, tf=512):
    t, d = x.shape
    dff = w1.shape[1]
    nj = dff // tf
    assert t % tm == 0 and nj * tf == dff and nj % 2 == 1
    row = pl.BlockSpec((tm, d), lambda i: (i, 0))
    vec = pl.BlockSpec((1, d), lambda i: (0, 0))
    hbm = pl.BlockSpec(memory_space=pl.ANY)
    out_shape = [jax.ShapeDtypeStruct((t, d), F32)] + ([] if final else [jax.ShapeDtypeStruct((t, d), BF16)])
    return pl.pallas_call(
        functools.partial(_ffn_stream_body, tm=tm, tf=tf, nj=nj, rows=min(256, tm), final=final),
        grid=(t // tm,),
        in_specs=[row, vec, hbm, hbm, hbm, vec],
        out_specs=[row] * len(out_shape),
        out_shape=out_shape,
        scratch_shapes=[pltpu.VMEM((tm, d), BF16),
                        pltpu.VMEM((2, d, tf), BF16), pltpu.VMEM((2, d, tf), BF16), pltpu.VMEM((2, tf, d), BF16),
                        pltpu.SemaphoreType.DMA((2, 3))],
        compiler_params=_cparams("arbitrary"),
        name="ffn_final_stream" if final else "ffn_mid_stream",
    )(x, g.reshape(1, d), w1, w3, w2, g2.reshape(1, d))


def _w_cols(a_ref, b_ref):
    rows = a_ref[...] if b_ref is None else jnp.concatenate([a_ref[b_ref.shape[0]:, :], b_ref[...]], axis=0)
    return rows.T.astype(BF16)


def _w_cols_specs(width, tn, start, col_of):
    shift = start % tn
    base = start - shift
    if shift == 0:
        return [pl.BlockSpec((tn, width), lambda *g: (base // tn + col_of(*g), 0))]
    assert shift % SUBLANES == 0 and tn % shift == 0 and base % shift == 0
    return [pl.BlockSpec((tn, width), lambda *g: (base // tn + col_of(*g), 0)),
            pl.BlockSpec((shift, width), lambda *g: ((base + tn * (col_of(*g) + 1)) // shift, 0))]


def _merge_body(u_ref, og_ref, or_ref, *refs, emit_w):
    u = u_ref[...]
    if not emit_w:
        wza_ref, wzb_ref, wgo_ref, wro_ref, m_ref = refs
        wza, wzb, wgo, wro = wza_ref[...], wzb_ref[...], wgo_ref[...], wro_ref[...]
    else:
        za_a, za_b, zb_a, zb_b, wgo_ref, wro_ref, m_ref, wza_o, wzb_o, wgo_o, wro_o = refs
        wza = _w_cols(za_a, za_b)
        wzb = _w_cols(zb_a, zb_b)
        wgo, wro = wgo_ref[...].astype(BF16), wro_ref[...].astype(BF16)
        wza_o[...], wzb_o[...], wgo_o[...], wro_o[...] = wza, wzb, wgo, wro
    za = _dot(u, wza)
    zb = _dot(u, wzb)
    branch_a = _dot(og_ref[...].astype(BF16), wgo)
    branch_b = _dot(or_ref[...].astype(BF16), wro)
    m_ref[...] = (_sigmoid(za) * branch_a + _sigmoid(zb) * branch_b).astype(BF16)


def _merge_gate(u, og, orr, wz, wgo, wro, *, tm=512):
    t, d = u.shape
    tm = min(tm, t)
    emit_w = wgo.dtype == F32
    assert not emit_w or t == tm, "weights are emitted by a single-token-tile call"
    tn = 256 if emit_w else 512
    col = lambda i, j: j
    acts = [pl.BlockSpec((tm, d), lambda i, j: (i, 0)),
            pl.BlockSpec((tm, GLA_V), lambda i, j: (i, 0)),
            pl.BlockSpec((tm, RET_V), lambda i, j: (i, 0))]
    w_col = lambda rows: pl.BlockSpec((rows, tn), lambda i, j: (0, j))
    out_shape = [jax.ShapeDtypeStruct((t, d), BF16)]
    out_specs = [pl.BlockSpec((tm, tn), lambda i, j: (i, j))]
    if emit_w:
        w_specs = (_w_cols_specs(d, tn, WB_START + WB_ZA, col) + _w_cols_specs(d, tn, WB_START + WB_ZB, col)
                   + [w_col(GLA_V), w_col(RET_V)])
        w_args = (wz, wz, wz, wz, wgo, wro)
        out_shape += [jax.ShapeDtypeStruct((d, d), BF16), jax.ShapeDtypeStruct((d, d), BF16),
                      jax.ShapeDtypeStruct(wgo.shape, BF16), jax.ShapeDtypeStruct(wro.shape, BF16)]
        out_specs += [w_col(d), w_col(d), w_col(GLA_V), w_col(RET_V)]
    else:
        w_specs, w_args = [w_col(d), w_col(d), w_col(GLA_V), w_col(RET_V)], (*wz, wgo, wro)
    return pl.pallas_call(
        functools.partial(_merge_body, emit_w=emit_w),
        grid=(t // tm, d // tn),
        in_specs=acts + w_specs,
        out_specs=out_specs,
        out_shape=out_shape,
        compiler_params=_cparams("parallel", "parallel"),
        name="merge_gate" + ("_castw" if emit_w else ""),
    )(u, og, orr, *w_args)


def _out_body(h_ref, m_ref, wout_ref, o_ref, *wb_ref):
    wout = wout_ref[...].astype(BF16)
    if wb_ref:
        wb_ref[0][...] = wout
    o_ref[...] = h_ref[...] + _dot(m_ref[...], wout)


def _out_proj(h, merged, wout, *, tm=512):
    t, d = h.shape
    tm = min(tm, t)
    emit_w = wout.dtype == F32
    assert not emit_w or t == tm, "weights are emitted by a single-token-tile call"
    tn = 512 if emit_w else d
    out_shape = [jax.ShapeDtypeStruct((t, d), F32)] + ([jax.ShapeDtypeStruct((d, d), BF16)] if emit_w else [])
    out_specs = [pl.BlockSpec((tm, tn), lambda i, j: (i, j))] + ([pl.BlockSpec((d, tn), lambda i, j: (0, j))] if emit_w else [])
    return pl.pallas_call(
        _out_body,
        grid=(t // tm, d // tn),
        in_specs=[pl.BlockSpec((tm, tn), lambda i, j: (i, j)),
                  pl.BlockSpec((tm, d), lambda i, j: (i, 0)),
                  pl.BlockSpec((d, tn), lambda i, j: (0, j))],
        out_specs=out_specs,
        out_shape=out_shape,
        compiler_params=_cparams("parallel", "parallel"),
        name="out_proj" + ("_castw" if emit_w else ""),
    )(h, merged, wout)


def _gla_body(u_ref, wq_ref, wk_ref, wv_ref, wg_ref, la_ref, ng_ref,
              dq_ref, dk_ref, dv_ref, dg_ref, dla_ref, ds_ref,
              og_ref, st_ref, dog_ref, dso_ref, s_scr, *, ts, c, nt, bt):
    s = pl.program_id(2)

    @pl.when(s == 0)
    def _():
        s_scr[...] = jnp.zeros_like(s_scr)

    dec = _gla_decode_prep(dq_ref, dk_ref, dv_ref, dla_ref, ds_ref, dso_ref, nt=nt, bt=bt)
    dec.units(0, bt)
    dec.finish(dg_ref, ng_ref, dog_ref)

    u = u_ref[...]
    q_all = _dot(u, wq_ref[...]) * (GLA_DK ** -0.5)
    k_all = _dot(u, wk_ref[...])
    v_all = _dot(u, wv_ref[...]).astype(BF16)
    g_all = _dot(u, wg_ref[...])
    la_all = la_ref[...]
    la_hi = la_all.astype(BF16)
    la_lo = (la_all - la_hi.astype(F32)).astype(BF16)

    rr = lax.broadcasted_iota(jnp.int32, (c, c), 0)
    cc = lax.broadcasted_iota(jnp.int32, (c, c), 1)
    causal = rr >= cc
    tril = causal.astype(BF16)
    mid = c // 2
    rows = [slice(ci * c, (ci + 1) * c) for ci in range(ts // c)]

    b = [_dot(tril, la_hi[r]) + _dot(tril, la_lo[r]) for r in rows]
    q_rel, k_rel, q_dec, k_end, decay = [], [], [], [], []
    for r, bc in zip(rows, b):
        q, k = q_all[r], k_all[r]
        b_mid, b_last = bc[mid:mid + 1, :], bc[c - 1:c, :]
        q_rel.append((q * jnp.exp(bc - b_mid)).astype(BF16))
        k_rel.append((k * jnp.exp(b_mid - bc)).astype(BF16))
        q_dec.append((q * jnp.exp(bc)).astype(BF16))
        k_end.append((k * jnp.exp(b_last - bc)).astype(BF16))
        dcol = _lane_replicated_column(jnp.exp(b_last))
        decay.append(jnp.concatenate([dcol] * (GLA_DV // LANES), axis=1))
    att = [jnp.where(causal, _dot_tb(qr, kr), 0.0).astype(BF16) for qr, kr in zip(q_rel, k_rel)]
    o_intra = [_dot(a, v_all[r]) for a, r in zip(att, rows)]
    upd = [_dot_ta(ke, v_all[r]) for ke, r in zip(k_end, rows)]
    st = s_scr[...]
    for ci, r in enumerate(rows):
        o = o_intra[ci] + _dot(q_dec[ci], st.astype(BF16))
        og_ref[r, :] = (_rms(o, ng_ref[...]) * _silu(g_all[r])).astype(BF16)
        st = decay[ci] * st + upd[ci]
    s_scr[...] = st

    @pl.when(s == pl.num_programs(2) - 1)
    def _():
        st_ref[...] = st


def _decode_tile(batch, ns, nb):
    bt = nb // (batch * ns)
    assert bt * batch * ns == nb and bt % SUBLANES == 0
    return bt


def _gla_mixer(u, la_p, batch, seq, w_a, ng, pg, la, state):
    ts, c = SEQ_TILE, GLA_CHUNK
    ns = seq // ts
    d = u.shape[1]
    nt, nb, _ = pg.shape
    bt = _decode_tile(batch, ns, nb)
    kq, kk, kv, kg = WA_Q // GLA_DK, WA_K // GLA_DK, WA_V // GLA_DV, WA_G // GLA_DV
    dec = lambda width, col0: pl.BlockSpec((nt, bt, width), lambda b, h, s: (0, b * ns + s, col0 + h))
    st_spec = pl.BlockSpec((bt, None, GLA_DK, GLA_DV), lambda b, h, s: (b * ns + s, h, 0, 0))
    return pl.pallas_call(
        functools.partial(_gla_body, ts=ts, c=c, nt=nt, bt=bt),
        grid=(batch, GLA_HEADS, ns),
        in_specs=[pl.BlockSpec((ts, d), lambda b, h, s: (b * ns + s, 0)),
                  pl.BlockSpec((d, GLA_DK), lambda b, h, s: (0, kq + h)),
                  pl.BlockSpec((d, GLA_DK), lambda b, h, s: (0, kk + h)),
                  pl.BlockSpec((d, GLA_DV), lambda b, h, s: (0, kv + h)),
                  pl.BlockSpec((d, GLA_DV), lambda b, h, s: (0, kg + h)),
                  pl.BlockSpec((ts, GLA_DK), lambda b, h, s: (b * ns + s, h)),
                  pl.BlockSpec((None, 1, GLA_DV), lambda b, h, s: (h, 0, 0)),
                  dec(GLA_DK, kq), dec(GLA_DK, kk), dec(GLA_DV, kv), dec(GLA_DV, kg), dec(GLA_DK, 0), st_spec],
        out_specs=(pl.BlockSpec((ts, GLA_DV), lambda b, h, s: (b * ns + s, h)),
                   pl.BlockSpec((None, None, GLA_DK, GLA_DV), lambda b, h, s: (b, h, 0, 0)),
                   dec(GLA_DV, 0), st_spec),
        out_shape=(jax.ShapeDtypeStruct((batch * seq, GLA_V), BF16),
                   jax.ShapeDtypeStruct((batch, GLA_HEADS, GLA_DK, GLA_DV), F32),
                   jax.ShapeDtypeStruct((nt, nb, GLA_V), F32),
                   jax.ShapeDtypeStruct(state.shape, F32)),
        scratch_shapes=[pltpu.VMEM((GLA_DK, GLA_DV), F32)],
        compiler_params=_cparams("parallel", "parallel", "arbitrary"),
        name="gla_mixer",
    )(u, w_a, w_a, w_a, w_a, la_p, ng, pg, pg, pg, pg, la, state)


def _rope(x, cos, sin):
    half = x.shape[-1] // 2
    x1, x2 = x[:, :half], x[:, half:]
    return jnp.concatenate([x1 * cos - x2 * sin, x2 * cos + x1 * sin], axis=-1)


def _ret_body(u_ref, wq_ref, wk_ref, wv_ref, wg_ref, cos_ref, sin_ref, dm_ref, qd_ref, kd_ref, cd_ref, ng_ref,
              pw_ref, dq_ref, dk_ref, dv_ref, dg_ref, dcos_ref, dsin_ref, ds_ref,
              or_ref, st_ref, dor_ref, dso_ref, s_scr, *, ts, c, nt, bt):
    s = pl.program_id(2)

    @pl.when(s == 0)
    def _():
        s_scr[...] = jnp.zeros_like(s_scr)

    dec = _ret_decode_prep(pw_ref, dq_ref, dk_ref, dv_ref, dcos_ref, dsin_ref, ds_ref, dso_ref, nt=nt, bt=bt)
    dec.units(0, bt)
    dec.finish(dg_ref, ng_ref, dor_ref)

    u = u_ref[...]
    tok = pl.ds(pl.multiple_of(s * ts, ts), ts)
    cos, sin = cos_ref[tok, :], sin_ref[tok, :]
    q_all = _rope(_dot(u, wq_ref[...]), cos, sin)
    k_all = _rope(_dot(u, wk_ref[...]), cos, sin) * (RET_DK ** -0.5)
    v_all = _dot(u, wv_ref[...]).astype(BF16)
    g_all = _dot(u, wg_ref[...])

    rows = [slice(ci * c, (ci + 1) * c) for ci in range(ts // c)]
    att = [(_dot_tb(q_all[r].astype(BF16), k_all[r].astype(BF16)) * dm_ref[...]).astype(BF16) for r in rows]
    o_intra = [_dot(a, v_all[r]) for a, r in zip(att, rows)]
    upd = [_dot_ta((k_all[r] * kd_ref[...]).astype(BF16), v_all[r]) for r in rows]
    q_dec = [(q_all[r] * qd_ref[...]).astype(BF16) for r in rows]
    st = s_scr[...]
    for ci, r in enumerate(rows):
        o = o_intra[ci] + _dot(q_dec[ci], st.astype(BF16))
        or_ref[r, :] = (_rms(o, ng_ref[...]) * _silu(g_all[r])).astype(BF16)
        st = cd_ref[...] * st + upd[ci]
    s_scr[...] = st

    @pl.when(s == pl.num_programs(2) - 1)
    def _():
        st_ref[...] = st


def _ret_log_gamma():
    return jnp.log1p(-jnp.exp2(-5.0 - jnp.arange(RET_HEADS, dtype=F32)))


def _rope_tables(pos):
    half = RET_DK // 2
    freqs = ROPE_BASE ** (-jnp.arange(half, dtype=F32) / half)
    ang = pos[:, None] * freqs[None, :]
    return jnp.cos(ang), jnp.sin(ang)


def _ret_mixer(u, batch, seq, w_b, ng, pr, state):
    ts, c = SEQ_TILE, RET_CHUNK
    ns = seq // ts
    d = u.shape[1]
    nt, nb, _ = pr.shape
    bt = _decode_tile(batch, ns, nb)
    kq, kk, kv, kg = WB_Q // RET_DK, WB_K // RET_DK, WB_V // RET_DV, WB_G // RET_DV
    cos, sin = _rope_tables(jnp.arange(seq, dtype=F32))
    dcos, dsin = _rope_tables(PAST_LEN + jnp.arange(nt, dtype=F32))
    pw = jnp.exp(_ret_log_gamma()[:, None] * jnp.arange(nt + 1, dtype=F32)[None, :])
    dec = lambda width, col0: pl.BlockSpec((nt, bt, width), lambda b, h, s: (0, b * ns + s, col0 + h))
    rope_spec = pl.BlockSpec((nt, RET_DK // 2), lambda b, h, s: (0, 0))
    st_spec = pl.BlockSpec((bt, None, RET_DK, RET_DV), lambda b, h, s: (b * ns + s, h, 0, 0))
    lg = _ret_log_gamma()
    idx = jnp.arange(c, dtype=F32)
    diff = idx[:, None] - idx[None, :]
    dmat = jnp.where(diff >= 0, jnp.exp(lg[:, None, None] * jnp.maximum(diff, 0.0)), 0.0)
    qdec = jnp.broadcast_to(jnp.exp(lg[:, None] * (idx + 1.0)[None, :])[:, :, None], (RET_HEADS, c, RET_DK))
    kdec = jnp.broadcast_to(jnp.exp(lg[:, None] * (c - 1.0 - idx)[None, :])[:, :, None], (RET_HEADS, c, RET_DK))
    cdec = jnp.broadcast_to(jnp.exp(lg * c)[:, None, None], (RET_HEADS, 1, RET_DV))
    return pl.pallas_call(
        functools.partial(_ret_body, ts=ts, c=c, nt=nt, bt=bt),
        grid=(batch, RET_HEADS, ns),
        in_specs=[pl.BlockSpec((ts, d), lambda b, h, s: (b * ns + s, 0)),
                  pl.BlockSpec((d, RET_DK), lambda b, h, s: (0, kq + h)),
                  pl.BlockSpec((d, RET_DK), lambda b, h, s: (0, kk + h)),
                  pl.BlockSpec((d, RET_DV), lambda b, h, s: (0, kv + h)),
                  pl.BlockSpec((d, RET_DV), lambda b, h, s: (0, kg + h)),
                  pl.BlockSpec((seq, RET_DK // 2), lambda b, h, s: (0, 0)),
                  pl.BlockSpec((seq, RET_DK // 2), lambda b, h, s: (0, 0)),
                  pl.BlockSpec((None, c, c), lambda b, h, s: (h, 0, 0)),
                  pl.BlockSpec((None, c, RET_DK), lambda b, h, s: (h, 0, 0)),
                  pl.BlockSpec((None, c, RET_DK), lambda b, h, s: (h, 0, 0)),
                  pl.BlockSpec((None, 1, RET_DV), lambda b, h, s: (h, 0, 0)),
                  pl.BlockSpec((None, 1, RET_DV), lambda b, h, s: (h, 0, 0)),
                  pl.BlockSpec(memory_space=pltpu.SMEM),
                  dec(RET_DK, kq), dec(RET_DK, kk), dec(RET_DV, kv), dec(RET_DV, kg), rope_spec, rope_spec, st_spec],
        out_specs=(pl.BlockSpec((ts, RET_DV), lambda b, h, s: (b * ns + s, h)),
                   pl.BlockSpec((None, None, RET_DK, RET_DV), lambda b, h, s: (b, h, 0, 0)),
                   dec(RET_DV, 0), st_spec),
        out_shape=(jax.ShapeDtypeStruct((batch * seq, RET_V), BF16),
                   jax.ShapeDtypeStruct((batch, RET_HEADS, RET_DK, RET_DV), F32),
                   jax.ShapeDtypeStruct((nt, nb, RET_V), F32),
                   jax.ShapeDtypeStruct(state.shape, F32)),
        scratch_shapes=[pltpu.VMEM((RET_DK, RET_DV), F32)],
        compiler_params=_cparams("parallel", "parallel", "arbitrary"),
        name="ret_mixer",
    )(u, w_b, w_b, w_b, w_b, cos, sin, dmat, qdec, kdec, cdec, ng, pw, pr, pr, pr, pr, dcos, dsin, state)


def _proj_body(u_ref, *refs):
    *w_refs, o_ref, wb_ref = refs
    w = _w_cols(w_refs[0], w_refs[1] if len(w_refs) > 1 else None)
    wb_ref[...] = w
    o_ref[...] = _dot(u_ref[...], w)


def _proj(u, w_t, start, n, *, tn=1024):
    t, d = u.shape
    w_specs = _w_cols_specs(d, tn, start, lambda j: j)
    return pl.pallas_call(
        _proj_body,
        grid=(n // tn,),
        in_specs=[pl.BlockSpec((t, d), lambda j: (0, 0))] + w_specs,
        out_specs=[pl.BlockSpec((t, tn), lambda j: (0, j)), pl.BlockSpec((d, tn), lambda j: (0, j))],
        out_shape=[jax.ShapeDtypeStruct((t, n), F32), jax.ShapeDtypeStruct((d, n), BF16)],
        compiler_params=_cparams("parallel"),
        name="sample_proj_castw",
    )(u, *([w_t] * len(w_specs)))


def _gate_body(u_ref, wga_ref, wgu_ref, bg_ref, o_ref):
    ga = _dot(u_ref[...], wga_ref[...])
    logit = _dot(ga.astype(BF16), wgu_ref[...]) + bg_ref[...]
    o_ref[...] = _log_sigmoid(logit) / GLA_GATE_TAU


def _gate(u, wga, wgu, bg, *, tm=512):
    t, d = u.shape
    return pl.pallas_call(
        _gate_body,
        grid=(t // tm,),
        in_specs=[pl.BlockSpec((tm, d), lambda i: (i, 0)),
                  pl.BlockSpec(wga.shape, lambda i: (0, 0)),
                  pl.BlockSpec(wgu.shape, lambda i: (0, 0)),
                  pl.BlockSpec(bg.shape, lambda i: (0, 0))],
        out_specs=pl.BlockSpec((tm, GLA_QK), lambda i: (i, 0)),
        out_shape=jax.ShapeDtypeStruct((t, GLA_QK), F32),
        compiler_params=_cparams("parallel"),
        name="log_gate",
    )(u, wga, wgu, bg)


def _row_group_mask(rows, cols, bt, i):
    return (lax.broadcasted_iota(jnp.int32, (rows, cols), 0) % bt) == i


class _Decode:
    def __init__(self, intra, q_dec, k_end, v_all, new_state, s_ref, so_ref, nt, bt):
        self.intra, self.q_dec, self.k_end, self.v_all = intra, q_dec, k_end, v_all
        self.new_state, self.s_ref, self.so_ref, self.nt, self.bt = new_state, s_ref, so_ref, nt, bt
        self.inter = jnp.zeros((nt * bt, v_all.shape[1]), F32)

    def units(self, lo, hi):
        rows, bt = self.nt * self.bt, self.bt
        for i in range(lo, hi):
            st = self.s_ref[i]
            qs = _dot(self.q_dec, st.astype(BF16))
            self.inter = self.inter + jnp.where(_row_group_mask(rows, qs.shape[1], bt, i), qs, 0.0)
            k_i = jnp.where(_row_group_mask(rows, self.k_end.shape[1], bt, i), self.k_end, 0.0).astype(BF16)
            self.so_ref[i] = self.new_state(i, st, _dot_ta(k_i, self.v_all))

    def finish(self, g_ref, ng_ref, o_ref):
        bt = self.bt
        for t in range(self.nt):
            o = self.intra[t] + self.inter[t * bt:(t + 1) * bt, :]
            o_ref[t] = _rms(o, ng_ref[...]) * _silu(g_ref[t])


def _gla_decode_prep(q_ref, k_ref, v_ref, la_ref, s_ref, so_ref, *, nt, bt):
    q = [q_ref[t] * (GLA_DK ** -0.5) for t in range(nt)]
    k = [k_ref[t] for t in range(nt)]
    v = [v_ref[t] for t in range(nt)]
    b = [la_ref[0]]
    for t in range(1, nt):
        b.append(b[-1] + la_ref[t])
    b_mid, b_last = b[nt // 2], b[nt - 1]
    q_rel = [q[t] * jnp.exp(b[t] - b_mid) for t in range(nt)]
    k_rel = [k[t] * jnp.exp(b_mid - b[t]) for t in range(nt)]
    intra = []
    for t in range(nt):
        acc = None
        for s in range(t + 1):
            term = jnp.sum(q_rel[t] * k_rel[s], axis=-1, keepdims=True) * v[s]
            acc = term if acc is None else acc + term
        intra.append(acc)
    q_dec = jnp.concatenate([q[t] * jnp.exp(b[t]) for t in range(nt)], axis=0).astype(BF16)
    k_end = jnp.concatenate([k[t] * jnp.exp(b_last - b[t]) for t in range(nt)], axis=0)
    v_all = jnp.concatenate(v, axis=0).astype(BF16)
    decay = jnp.exp(b_last)

    def new_state(i, st, upd):
        dcol = _lane_replicated_column(decay[i:i + 1, :])
        return jnp.concatenate([dcol] * (GLA_DV // LANES), axis=1) * st + upd

    return _Decode(intra, q_dec, k_end, v_all, new_state, s_ref, so_ref, nt, bt)


def _ret_decode_prep(pw_ref, q_ref, k_ref, v_ref, cos_ref, sin_ref, s_ref, so_ref, *, nt, bt):
    h = pl.program_id(1)
    pw = [pw_ref[h, n] for n in range(nt + 1)]
    q = [_rope(q_ref[t], cos_ref[t:t + 1, :], sin_ref[t:t + 1, :]) for t in range(nt)]
    k = [_rope(k_ref[t], cos_ref[t:t + 1, :], sin_ref[t:t + 1, :]) * (RET_DK ** -0.5) for t in range(nt)]
    v = [v_ref[t] for t in range(nt)]
    intra = []
    for t in range(nt):
        acc = None
        for s in range(t + 1):
            term = (jnp.sum(q[t] * k[s], axis=-1, keepdims=True) * pw[t - s]) * v[s]
            acc = term if acc is None else acc + term
        intra.append(acc)
    q_dec = jnp.concatenate([q[t] * pw[t + 1] for t in range(nt)], axis=0).astype(BF16)
    k_end = jnp.concatenate([k[t] * pw[nt - 1 - t] for t in range(nt)], axis=0)
    v_all = jnp.concatenate(v, axis=0).astype(BF16)
    return _Decode(intra, q_dec, k_end, v_all, lambda i, st, upd: pw[nt] * st + upd, s_ref, so_ref, nt, bt)


def kernel(x_prompt, x_sample, state_gla, state_ret, ffn1_norm, ffn1_w1, ffn1_w3, ffn1_w2, mix_norm, w_in, w_gate_up, b_gate, gla_norm, w_gla_o, ret_norm, w_ret_o, w_out, ffn2_norm, ffn2_w1, ffn2_w3, ffn2_w2, final_norm):
    depth = w_in.shape[0]
    batch, seq, d = x_prompt.shape
    nb, nt, _ = x_sample.shape

    hp = x_prompt.reshape(batch * seq, d)
    hs = x_sample.transpose(1, 0, 2).reshape(nt * nb, d)
    gla_p, ret_p, gla_s, ret_s = [], [], [], []
    for l in range(depth):
        last = l == depth - 1
        wl = w_in[l].T
        w_ga = w_in[l][:, WA_WIDTH:WB_START].astype(BF16)
        wgu = w_gate_up[l].astype(BF16)
        bg = b_gate[l].reshape(1, GLA_QK)
        ng = gla_norm[l].reshape(GLA_HEADS, 1, GLA_DV)
        nr = ret_norm[l].reshape(RET_HEADS, 1, RET_DV)
        g_next = final_norm if last else ffn1_norm[l + 1]

        hs, us, *f1 = _ffn(hs, ffn1_norm[l], ffn1_w1[l], ffn1_w3[l], ffn1_w2[l], mix_norm[l], final=False)
        hp, up = _ffn_stream(hp, ffn1_norm[l], *f1, mix_norm[l], final=False)

        pg, w_a = _proj(us, wl, 0, WA_WIDTH)
        pr, w_b = _proj(us, wl, WB_START, WB_ZA)
        la = _gate(us, w_ga, wgu, bg)
        la_p = _gate(up, w_ga, wgu, bg)
        og, sg, og_s, sg_s = _gla_mixer(up, la_p, batch, seq, w_a, ng,
                                        pg.reshape(nt, nb, -1), la.reshape(nt, nb, -1), state_gla[l])
        orr, sr, or_s, sr_s = _ret_mixer(up, batch, seq, w_b, nr, pr.reshape(nt, nb, -1), state_ret[l])
        gla_s.append(sg_s)
        ret_s.append(sr_s)
        gla_p.append(sg)
        ret_p.append(sr)

        ms, wza, wzb, wgo, wro = _merge_gate(us, og_s.reshape(nt * nb, GLA_V), or_s.reshape(nt * nb, RET_V),
                                             wl, w_gla_o[l], w_ret_o[l])
        mp, = _merge_gate(up, og, orr, (wza, wzb), wgo, wro)
        hs, wout = _out_proj(hs, ms, w_out[l])
        hp, = _out_proj(hp, mp, wout)

        hs, *f2 = _ffn(hs, ffn2_norm[l], ffn2_w1[l], ffn2_w3[l], ffn2_w2[l], g_next, final=last)
        hp, *_ = _ffn_stream(hp, ffn2_norm[l], *f2[-3:], g_next, final=last)

    y_prompt = hp.reshape(batch, seq, d)
    y_sample = hs.reshape(nt, nb, d).transpose(1, 0, 2)
    return (y_prompt, y_sample, jnp.stack(gla_p), jnp.stack(ret_p), jnp.stack(gla_s), jnp.stack(ret_s))
```

```python
import functools

import jax
import jax.numpy as jnp
from jax import lax
from jax.experimental import pallas as pl
from jax.experimental.pallas import tpu as pltpu

F32, BF16 = jnp.float32, jnp.bfloat16

D_MODEL = 2048
PAST_LEN = 16384
GLA_HEADS = 4
GLA_DK = D_MODEL // (2 * GLA_HEADS)
GLA_DV = D_MODEL // GLA_HEADS
GLA_GATE_RANK = 16
GLA_GATE_TAU = 16.0
RET_HEADS = 8
RET_DK = D_MODEL // RET_HEADS
RET_DV = 2 * D_MODEL // RET_HEADS
ROPE_BASE = 10000.0
EPS = 1e-6
GLA_QK = GLA_HEADS * GLA_DK
GLA_V = GLA_HEADS * GLA_DV
RET_QK = RET_HEADS * RET_DK
RET_V = RET_HEADS * RET_DV

WA_Q, WA_K, WA_V, WA_G = 0, GLA_QK, 2 * GLA_QK, 2 * GLA_QK + GLA_V
WA_WIDTH = 2 * GLA_QK + 2 * GLA_V
WB_Q, WB_K, WB_V, WB_G = 0, RET_QK, 2 * RET_QK, 2 * RET_QK + RET_V
WB_ZA = 2 * RET_QK + 2 * RET_V
WB_ZB = WB_ZA + D_MODEL
WB_START = WA_WIDTH + GLA_GATE_RANK

LANES = 128
SUBLANES = 8
VMEM_LIMIT = 58 * 2**20

GLA_CHUNK = 64
RET_CHUNK = 256
SEQ_TILE = 512


def _cparams(*sem):
    return pltpu.CompilerParams(dimension_semantics=sem, vmem_limit_bytes=VMEM_LIMIT)


def _dot(a, b):
    return jnp.dot(a, b, preferred_element_type=F32)


def _dot_tb(a, b):
    return lax.dot_general(a, b, (((1,), (1,)), ((), ())), preferred_element_type=F32)


def _dot_ta(a, b):
    return lax.dot_general(a, b, (((0,), (0,)), ((), ())), preferred_element_type=F32)


def _rms(x, g):
    return x * lax.rsqrt(jnp.mean(x * x, axis=-1, keepdims=True) + EPS) * g


def _sigmoid(x):
    return 1.0 / (1.0 + jnp.exp(-x))


def _silu(x):
    return x * _sigmoid(x)


def _log_sigmoid(x):
    return jnp.minimum(x, 0.0) - jnp.log1p(jnp.exp(-jnp.abs(x)))


def _lane_replicated_column(row):
    return jnp.broadcast_to(row, (LANES, row.shape[-1])).T


def _ffn_body(x_ref, g_ref, w1_ref, w3_ref, w2_ref, g2_ref, *refs, tm, rows, final, emit_w):
    refs = list(refs)
    u_scr = refs.pop()
    wb_refs = [refs.pop() for _ in range(3)][::-1] if emit_w else None
    if final:
        y_ref, = refs
        acc_ref = y_ref
    else:
        h_ref, un_ref = refs
        acc_ref = h_ref
    j = pl.program_id(1)

    @pl.when(j == 0)
    def _():
        for r in range(0, tm, rows):
            u_scr[r:r + rows, :] = _rms(x_ref[r:r + rows, :], g_ref[...]).astype(BF16)
            acc_ref[r:r + rows, :] = jnp.zeros((rows, acc_ref.shape[1]), F32)

    w1, w3, w2 = w1_ref[...].astype(BF16), w3_ref[...].astype(BF16), w2_ref[...].astype(BF16)
    if emit_w:
        for ref, w in zip(wb_refs, (w1, w3, w2)):
            ref[...] = w
    u = u_scr[...]
    a = _dot(u, w1)
    b = _dot(u, w3)
    acc_ref[...] += _dot((_silu(a) * b).astype(BF16), w2)

    @pl.when(j == pl.num_programs(1) - 1)
    def _():
        for r in range(0, tm, rows):
            h = x_ref[r:r + rows, :] + 0.5 * acc_ref[r:r + rows, :]
            if final:
                y_ref[r:r + rows, :] = _rms(h, g2_ref[...])
            else:
                h_ref[r:r + rows, :] = h
                un_ref[r:r + rows, :] = _rms(h, g2_ref[...]).astype(BF16)


def _ffn(x, g, w1, w3, w2, g2, *, final, tm=512):
    t, d = x.shape
    dff = w1.shape[1]
    tm = min(tm, t)
    emit_w = w1.dtype == F32
    assert not emit_w or t == tm, "weights are emitted by a single-token-tile call"
    tf = 256 if emit_w else 512
    grid = (t // tm, dff // tf)
    row = pl.BlockSpec((tm, d), lambda i, j: (i, 0))
    vec = pl.BlockSpec((1, d), lambda i, j: (0, 0))
    w_up = pl.BlockSpec((d, tf), lambda i, j: (0, j))
    w_dn = pl.BlockSpec((tf, d), lambda i, j: (j, 0))
    out_shape = [jax.ShapeDtypeStruct((t, d), F32)] + ([] if final else [jax.ShapeDtypeStruct((t, d), BF16)])
    out_specs = [row] * len(out_shape)
    if emit_w:
        out_shape += [jax.ShapeDtypeStruct(w.shape, BF16) for w in (w1, w3, w2)]
        out_specs += [w_up, w_up, w_dn]
    return pl.pallas_call(
        functools.partial(_ffn_body, tm=tm, rows=min(256, tm), final=final, emit_w=emit_w),
        grid=grid,
        in_specs=[row, vec, w_up, w_up, w_dn, vec],
        out_specs=out_specs,
        out_shape=out_shape,
        scratch_shapes=[pltpu.VMEM((tm, d), BF16)],
        compiler_params=_cparams("parallel", "arbitrary"),
        name=("ffn_final" if final else "ffn_mid") + ("_castw" if emit_w else ""),
    )(x, g.reshape(1, d), w1, w3, w2, g2.reshape(1, d))


def _w_cols(a_ref, b_ref):
    rows = a_ref[...] if b_ref is None else jnp.concatenate([a_ref[b_ref.shape[0]:, :], b_ref[...]], axis=0)
    return rows.T.astype(BF16)


def _w_cols_specs(width, tn, start, col_of):
    shift = start % tn
    base = start - shift
    if shift == 0:
        return [pl.BlockSpec((tn, width), lambda *g: (base // tn + col_of(*g), 0))]
    assert shift % SUBLANES == 0 and tn % shift == 0 and base % shift == 0
    return [pl.BlockSpec((tn, width), lambda *g: (base // tn + col_of(*g), 0)),
            pl.BlockSpec((shift, width), lambda *g: ((base + tn * (col_of(*g) + 1)) // shift, 0))]


def _merge_body(u_ref, og_ref, or_ref, *refs, emit_w):
    u = u_ref[...]
    if not emit_w:
        wza_ref, wzb_ref, wgo_ref, wro_ref, m_ref = refs
        wza, wzb, wgo, wro = wza_ref[...], wzb_ref[...], wgo_ref[...], wro_ref[...]
    else:
        za_a, za_b, zb_a, zb_b, wgo_ref, wro_ref, m_ref, wza_o, wzb_o, wgo_o, wro_o = refs
        wza = _w_cols(za_a, za_b)
        wzb = _w_cols(zb_a, zb_b)
        wgo, wro = wgo_ref[...].astype(BF16), wro_ref[...].astype(BF16)
        wza_o[...], wzb_o[...], wgo_o[...], wro_o[...] = wza, wzb, wgo, wro
    za = _dot(u, wza)
    zb = _dot(u, wzb)
    branch_a = _dot(og_ref[...].astype(BF16), wgo)
    branch_b = _dot(or_ref[...].astype(BF16), wro)
    m_ref[...] = (_sigmoid(za) * branch_a + _sigmoid(zb) * branch_b).astype(BF16)


def _merge_gate(u, og, orr, wz, wgo, wro, *, tm=1024):
    t, d = u.shape
    tm = min(tm, t)
    emit_w = wgo.dtype == F32
    assert not emit_w or t == tm, "weights are emitted by a single-token-tile call"
    tn = 256
    col = lambda i, j: j
    acts = [pl.BlockSpec((tm, d), lambda i, j: (i, 0)),
            pl.BlockSpec((tm, GLA_V), lambda i, j: (i, 0)),
            pl.BlockSpec((tm, RET_V), lambda i, j: (i, 0))]
    w_col = lambda rows: pl.BlockSpec((rows, tn), lambda i, j: (0, j))
    out_shape = [jax.ShapeDtypeStruct((t, d), BF16)]
    out_specs = [pl.BlockSpec((tm, tn), lambda i, j: (i, j))]
    if emit_w:
        w_specs = (_w_cols_specs(d, tn, WB_START + WB_ZA, col) + _w_cols_specs(d, tn, WB_START + WB_ZB, col)
                   + [w_col(GLA_V), w_col(RET_V)])
        w_args = (wz, wz, wz, wz, wgo, wro)
        out_shape += [jax.ShapeDtypeStruct((d, d), BF16), jax.ShapeDtypeStruct((d, d), BF16),
                      jax.ShapeDtypeStruct(wgo.shape, BF16), jax.ShapeDtypeStruct(wro.shape, BF16)]
        out_specs += [w_col(d), w_col(d), w_col(GLA_V), w_col(RET_V)]
    else:
        w_specs, w_args = [w_col(d), w_col(d), w_col(GLA_V), w_col(RET_V)], (*wz, wgo, wro)
    return pl.pallas_call(
        functools.partial(_merge_body, emit_w=emit_w),
        grid=(t // tm, d // tn),
        in_specs=acts + w_specs,
        out_specs=out_specs,
        out_shape=out_shape,
        compiler_params=_cparams("parallel", "parallel"),
        name="merge_gate" + ("_castw" if emit_w else ""),
    )(u, og, orr, *w_args)


def _out_body(h_ref, m_ref, wout_ref, o_ref, *wb_ref):
    wout = wout_ref[...].astype(BF16)
    if wb_ref:
        wb_ref[0][...] = wout
    o_ref[...] = h_ref[...] + _dot(m_ref[...], wout)


def _out_proj(h, merged, wout, *, tm=512):
    t, d = h.shape
    tm = min(tm, t)
    emit_w = wout.dtype == F32
    assert not emit_w or t == tm, "weights are emitted by a single-token-tile call"
    tn = 512 if emit_w else d
    out_shape = [jax.ShapeDtypeStruct((t, d), F32)] + ([jax.ShapeDtypeStruct((d, d), BF16)] if emit_w else [])
    out_specs = [pl.BlockSpec((tm, tn), lambda i, j: (i, j))] + ([pl.BlockSpec((d, tn), lambda i, j: (0, j))] if emit_w else [])
    return pl.pallas_call(
        _out_body,
        grid=(t // tm, d // tn),
        in_specs=[pl.BlockSpec((tm, tn), lambda i, j: (i, j)),
                  pl.BlockSpec((tm, d), lambda i, j: (i, 0)),
                  pl.BlockSpec((d, tn), lambda i, j: (0, j))],
        out_specs=out_specs,
        out_shape=out_shape,
        compiler_params=_cparams("parallel", "parallel"),
        name="out_proj" + ("_castw" if emit_w else ""),
    )(h, merged, wout)


def _gla_body(u_ref, wq_ref, wk_ref, wv_ref, wg_ref, la_ref, ng_ref,
              dq_ref, dk_ref, dv_ref, dg_ref, dla_ref, ds_ref,
              og_ref, st_ref, dog_ref, dso_ref, s_scr, *, ts, c, nt, bt):
    s = pl.program_id(2)

    @pl.when(s == 0)
    def _():
        s_scr[...] = jnp.zeros_like(s_scr)

    dec = _gla_decode_prep(dq_ref, dk_ref, dv_ref, dla_ref, ds_ref, dso_ref, nt=nt, bt=bt)
    dec.units(0, bt)
    dec.finish(dg_ref, ng_ref, dog_ref)

    u = u_ref[...]
    q_all = _dot(u, wq_ref[...]) * (GLA_DK ** -0.5)
    k_all = _dot(u, wk_ref[...])
    v_all = _dot(u, wv_ref[...]).astype(BF16)
    g_all = _dot(u, wg_ref[...])
    la_all = la_ref[...]
    la_hi = la_all.astype(BF16)
    la_lo = (la_all - la_hi.astype(F32)).astype(BF16)

    rr = lax.broadcasted_iota(jnp.int32, (c, c), 0)
    cc = lax.broadcasted_iota(jnp.int32, (c, c), 1)
    causal = rr >= cc
    tril = causal.astype(BF16)
    mid = c // 2
    rows = [slice(ci * c, (ci + 1) * c) for ci in range(ts // c)]

    b = [_dot(tril, la_hi[r]) + _dot(tril, la_lo[r]) for r in rows]
    q_rel, k_rel, q_dec, k_end, decay = [], [], [], [], []
    for r, bc in zip(rows, b):
        q, k = q_all[r], k_all[r]
        b_mid, b_last = bc[mid:mid + 1, :], bc[c - 1:c, :]
        q_rel.append((q * jnp.exp(bc - b_mid)).astype(BF16))
        k_rel.append((k * jnp.exp(b_mid - bc)).astype(BF16))
        q_dec.append((q * jnp.exp(bc)).astype(BF16))
        k_end.append((k * jnp.exp(b_last - bc)).astype(BF16))
        dcol = _lane_replicated_column(jnp.exp(b_last))
        decay.append(jnp.concatenate([dcol] * (GLA_DV // LANES), axis=1))
    att = [jnp.where(causal, _dot_tb(qr, kr), 0.0).astype(BF16) for qr, kr in zip(q_rel, k_rel)]
    o_intra = [_dot(a, v_all[r]) for a, r in zip(att, rows)]
    upd = [_dot_ta(ke, v_all[r]) for ke, r in zip(k_end, rows)]
    st = s_scr[...]
    for ci, r in enumerate(rows):
        o = o_intra[ci] + _dot(q_dec[ci], st.astype(BF16))
        og_ref[r, :] = (_rms(o, ng_ref[...]) * _silu(g_all[r])).astype(BF16)
        st = decay[ci] * st + upd[ci]
    s_scr[...] = st

    @pl.when(s == pl.num_programs(2) - 1)
    def _():
        st_ref[...] = st


def _decode_tile(batch, ns, nb):
    bt = nb // (batch * ns)
    assert bt * batch * ns == nb and bt % SUBLANES == 0
    return bt


def _gla_mixer(u, la_p, batch, seq, w_a, ng, pg, la, state):
    ts, c = SEQ_TILE, GLA_CHUNK
    ns = seq // ts
    d = u.shape[1]
    nt, nb, _ = pg.shape
    bt = _decode_tile(batch, ns, nb)
    kq, kk, kv, kg = WA_Q // GLA_DK, WA_K // GLA_DK, WA_V // GLA_DV, WA_G // GLA_DV
    dec = lambda width, col0: pl.BlockSpec((nt, bt, width), lambda b, h, s: (0, b * ns + s, col0 + h))
    st_spec = pl.BlockSpec((bt, None, GLA_DK, GLA_DV), lambda b, h, s: (b * ns + s, h, 0, 0))
    return pl.pallas_call(
        functools.partial(_gla_body, ts=ts, c=c, nt=nt, bt=bt),
        grid=(batch, GLA_HEADS, ns),
        in_specs=[pl.BlockSpec((ts, d), lambda b, h, s: (b * ns + s, 0)),
                  pl.BlockSpec((d, GLA_DK), lambda b, h, s: (0, kq + h)),
                  pl.BlockSpec((d, GLA_DK), lambda b, h, s: (0, kk + h)),
                  pl.BlockSpec((d, GLA_DV), lambda b, h, s: (0, kv + h)),
                  pl.BlockSpec((d, GLA_DV), lambda b, h, s: (0, kg + h)),
                  pl.BlockSpec((ts, GLA_DK), lambda b, h, s: (b * ns + s, h)),
                  pl.BlockSpec((None, 1, GLA_DV), lambda b, h, s: (h, 0, 0)),
                  dec(GLA_DK, kq), dec(GLA_DK, kk), dec(GLA_DV, kv), dec(GLA_DV, kg), dec(GLA_DK, 0), st_spec],
        out_specs=(pl.BlockSpec((ts, GLA_DV), lambda b, h, s: (b * ns + s, h)),
                   pl.BlockSpec((None, None, GLA_DK, GLA_DV), lambda b, h, s: (b, h, 0, 0)),
                   dec(GLA_DV, 0), st_spec),
        out_shape=(jax.ShapeDtypeStruct((batch * seq, GLA_V), BF16),
                   jax.ShapeDtypeStruct((batch, GLA_HEADS, GLA_DK, GLA_DV), F32),
                   jax.ShapeDtypeStruct((nt, nb, GLA_V), F32),
                   jax.ShapeDtypeStruct(state.shape, F32)),
        scratch_shapes=[pltpu.VMEM((GLA_DK, GLA_DV), F32)],
        compiler_params=_cparams("parallel", "parallel", "arbitrary"),
        name="gla_mixer",
    )(u, w_a, w_a, w_a, w_a, la_p, ng, pg, pg, pg, pg, la, state)


def _rope(x, cos, sin):
    half = x.shape[-1] // 2
    x1, x2 = x[:, :half], x[:, half:]
    return jnp.concatenate([x1 * cos - x2 * sin, x2 * cos + x1 * sin], axis=-1)


def _ret_body(u_ref, wq_ref, wk_ref, wv_ref, wg_ref, cos_ref, sin_ref, dm_ref, qd_ref, kd_ref, cd_ref, ng_ref,
              pw_ref, dq_ref, dk_ref, dv_ref, dg_ref, dcos_ref, dsin_ref, ds_ref,
              or_ref, st_ref, dor_ref, dso_ref, s_scr, *, ts, c, nt, bt):
    s = pl.program_id(2)

    @pl.when(s == 0)
    def _():
        s_scr[...] = jnp.zeros_like(s_scr)

    dec = _ret_decode_prep(pw_ref, dq_ref, dk_ref, dv_ref, dcos_ref, dsin_ref, ds_ref, dso_ref, nt=nt, bt=bt)
    cut = [bt * n // 4 for n in range(5)]

    u = u_ref[...]
    tok = pl.ds(pl.multiple_of(s * ts, ts), ts)
    cos, sin = cos_ref[tok, :], sin_ref[tok, :]
    q_all = _rope(_dot(u, wq_ref[...]), cos, sin)
    dec.units(cut[0], cut[1])
    k_all = _rope(_dot(u, wk_ref[...]), cos, sin) * (RET_DK ** -0.5)
    dec.units(cut[1], cut[2])
    v_all = _dot(u, wv_ref[...]).astype(BF16)
    dec.units(cut[2], cut[3])
    g_all = _dot(u, wg_ref[...])
    dec.units(cut[3], cut[4])
    dec.finish(dg_ref, ng_ref, dor_ref)

    rows = [slice(ci * c, (ci + 1) * c) for ci in range(ts // c)]
    att = [(_dot_tb(q_all[r].astype(BF16), k_all[r].astype(BF16)) * dm_ref[...]).astype(BF16) for r in rows]
    o_intra = [_dot(a, v_all[r]) for a, r in zip(att, rows)]
    upd = [_dot_ta((k_all[r] * kd_ref[...]).astype(BF16), v_all[r]) for r in rows]
    q_dec = [(q_all[r] * qd_ref[...]).astype(BF16) for r in rows]
    st = s_scr[...]
    for ci, r in enumerate(rows):
        o = o_intra[ci] + _dot(q_dec[ci], st.astype(BF16))
        or_ref[r, :] = (_rms(o, ng_ref[...]) * _silu(g_all[r])).astype(BF16)
        st = cd_ref[...] * st + upd[ci]
    s_scr[...] = st

    @pl.when(s == pl.num_programs(2) - 1)
    def _():
        st_ref[...] = st


def _ret_log_gamma():
    return jnp.log1p(-jnp.exp2(-5.0 - jnp.arange(RET_HEADS, dtype=F32)))


def _rope_tables(pos):
    half = RET_DK // 2
    freqs = ROPE_BASE ** (-jnp.arange(half, dtype=F32) / half)
    ang = pos[:, None] * freqs[None, :]
    return jnp.cos(ang), jnp.sin(ang)


def _ret_mixer(u, batch, seq, w_b, ng, pr, state):
    ts, c = SEQ_TILE, RET_CHUNK
    ns = seq // ts
    d = u.shape[1]
    nt, nb, _ = pr.shape
    bt = _decode_tile(batch, ns, nb)
    kq, kk, kv, kg = WB_Q // RET_DK, WB_K // RET_DK, WB_V // RET_DV, WB_G // RET_DV
    cos, sin = _rope_tables(jnp.arange(seq, dtype=F32))
    dcos, dsin = _rope_tables(PAST_LEN + jnp.arange(nt, dtype=F32))
    pw = jnp.exp(_ret_log_gamma()[:, None] * jnp.arange(nt + 1, dtype=F32)[None, :])
    dec = lambda width, col0: pl.BlockSpec((nt, bt, width), lambda b, h, s: (0, b * ns + s, col0 + h))
    rope_spec = pl.BlockSpec((nt, RET_DK // 2), lambda b, h, s: (0, 0))
    st_spec = pl.BlockSpec((bt, None, RET_DK, RET_DV), lambda b, h, s: (b * ns + s, h, 0, 0))
    lg = _ret_log_gamma()
    idx = jnp.arange(c, dtype=F32)
    diff = idx[:, None] - idx[None, :]
    dmat = jnp.where(diff >= 0, jnp.exp(lg[:, None, None] * jnp.maximum(diff, 0.0)), 0.0)
    qdec = jnp.broadcast_to(jnp.exp(lg[:, None] * (idx + 1.0)[None, :])[:, :, None], (RET_HEADS, c, RET_DK))
    kdec = jnp.broadcast_to(jnp.exp(lg[:, None] * (c - 1.0 - idx)[None, :])[:, :, None], (RET_HEADS, c, RET_DK))
    cdec = jnp.broadcast_to(jnp.exp(lg * c)[:, None, None], (RET_HEADS, 1, RET_DV))
    return pl.pallas_call(
        functools.partial(_ret_body, ts=ts, c=c, nt=nt, bt=bt),
        grid=(batch, RET_HEADS, ns),
        in_specs=[pl.BlockSpec((ts, d), lambda b, h, s: (b * ns + s, 0)),
                  pl.BlockSpec((d, RET_DK), lambda b, h, s: (0, kq + h)),
                  pl.BlockSpec((d, RET_DK), lambda b, h, s: (0, kk + h)),
                  pl.BlockSpec((d, RET_DV), lambda b, h, s: (0, kv + h)),
                  pl.BlockSpec((d, RET_DV), lambda b, h, s: (0, kg + h)),
                  pl.BlockSpec((seq, RET_DK // 2), lambda b, h, s: (0, 0)),
                  pl.BlockSpec((seq, RET_DK // 2), lambda b, h, s: (0, 0)),
                  pl.BlockSpec((None, c, c), lambda b, h, s: (h, 0, 0)),
                  pl.BlockSpec((None, c, RET_DK), lambda b, h, s: (h, 0, 0)),
                  pl.BlockSpec((None, c, RET_DK), lambda b, h, s: (h, 0, 0)),
                  pl.BlockSpec((None, 1, RET_DV), lambda b, h, s: (h, 0, 0)),
                  pl.BlockSpec((None, 1, RET_DV), lambda b, h, s: (h, 0, 0)),
                  pl.BlockSpec(memory_space=pltpu.SMEM),
                  dec(RET_DK, kq), dec(RET_DK, kk), dec(RET_DV, kv), dec(RET_DV, kg), rope_spec, rope_spec, st_spec],
        out_specs=(pl.BlockSpec((ts, RET_DV), lambda b, h, s: (b * ns + s, h)),
                   pl.BlockSpec((None, None, RET_DK, RET_DV), lambda b, h, s: (b, h, 0, 0)),
                   dec(RET_DV, 0), st_spec),
        out_shape=(jax.ShapeDtypeStruct((batch * seq, RET_V), BF16),
                   jax.ShapeDtypeStruct((batch, RET_HEADS, RET_DK, RET_DV), F32),
                   jax.ShapeDtypeStruct((nt, nb, RET_V), F32),
                   jax.ShapeDtypeStruct(state.shape, F32)),
        scratch_shapes=[pltpu.VMEM((RET_DK, RET_DV), F32)],
        compiler_params=_cparams("parallel", "parallel", "arbitrary"),
        name="ret_mixer",
    )(u, w_b, w_b, w_b, w_b, cos, sin, dmat, qdec, kdec, cdec, ng, pw, pr, pr, pr, pr, dcos, dsin, state)


def _proj_body(u_ref, *refs):
    *w_refs, o_ref, wb_ref = refs
    w = _w_cols(w_refs[0], w_refs[1] if len(w_refs) > 1 else None)
    wb_ref[...] = w
    o_ref[...] = _dot(u_ref[...], w)


def _proj(u, w_t, start, n, *, tn=1024):
    t, d = u.shape
    w_specs = _w_cols_specs(d, tn, start, lambda j: j)
    return pl.pallas_call(
        _proj_body,
        grid=(n // tn,),
        in_specs=[pl.BlockSpec((t, d), lambda j: (0, 0))] + w_specs,
        out_specs=[pl.BlockSpec((t, tn), lambda j: (0, j)), pl.BlockSpec((d, tn), lambda j: (0, j))],
        out_shape=[jax.ShapeDtypeStruct((t, n), F32), jax.ShapeDtypeStruct((d, n), BF16)],
        compiler_params=_cparams("parallel"),
        name="sample_proj_castw",
    )(u, *([w_t] * len(w_specs)))


def _gate_body(u_ref, wga_ref, wgu_ref, bg_ref, o_ref):
    ga = _dot(u_ref[...], wga_ref[...])
    logit = _dot(ga.astype(BF16), wgu_ref[...]) + bg_ref[...]
    o_ref[...] = _log_sigmoid(logit) / GLA_GATE_TAU


def _gate(u, wga, wgu, bg, *, tm=512):
    t, d = u.shape
    return pl.pallas_call(
        _gate_body,
        grid=(t // tm,),
        in_specs=[pl.BlockSpec((tm, d), lambda i: (i, 0)),
                  pl.BlockSpec(wga.shape, lambda i: (0, 0)),
                  pl.BlockSpec(wgu.shape, lambda i: (0, 0)),
                  pl.BlockSpec(bg.shape, lambda i: (0, 0))],
        out_specs=pl.BlockSpec((tm, GLA_QK), lambda i: (i, 0)),
        out_shape=jax.ShapeDtypeStruct((t, GLA_QK), F32),
        compiler_params=_cparams("parallel"),
        name="log_gate",
    )(u, wga, wgu, bg)


def _row_group_mask(rows, cols, bt, i):
    return (lax.broadcasted_iota(jnp.int32, (rows, cols), 0) % bt) == i


class _Decode:
    def __init__(self, intra, q_dec, k_end, v_all, new_state, s_ref, so_ref, nt, bt):
        self.intra, self.q_dec, self.k_end, self.v_all = intra, q_dec, k_end, v_all
        self.new_state, self.s_ref, self.so_ref, self.nt, self.bt = new_state, s_ref, so_ref, nt, bt
        self.inter = jnp.zeros((nt * bt, v_all.shape[1]), F32)

    def units(self, lo, hi):
        rows, bt = self.nt * self.bt, self.bt
        for i in range(lo, hi):
            st = self.s_ref[i]
            qs = _dot(self.q_dec, st.astype(BF16))
            self.inter = self.inter + jnp.where(_row_group_mask(rows, qs.shape[1], bt, i), qs, 0.0)
            k_i = jnp.where(_row_group_mask(rows, self.k_end.shape[1], bt, i), self.k_end, 0.0).astype(BF16)
            self.so_ref[i] = self.new_state(i, st, _dot_ta(k_i, self.v_all))

    def finish(self, g_ref, ng_ref, o_ref):
        bt = self.bt
        for t in range(self.nt):
            o = self.intra[t] + self.inter[t * bt:(t + 1) * bt, :]
            o_ref[t] = _rms(o, ng_ref[...]) * _silu(g_ref[t])


def _gla_decode_prep(q_ref, k_ref, v_ref, la_ref, s_ref, so_ref, *, nt, bt):
    q = [q_ref[t] * (GLA_DK ** -0.5) for t in range(nt)]
    k = [k_ref[t] for t in range(nt)]
    v = [v_ref[t] for t in range(nt)]
    b = [la_ref[0]]
    for t in range(1, nt):
        b.append(b[-1] + la_ref[t])
    b_mid, b_last = b[nt // 2], b[nt - 1]
    q_rel = [q[t] * jnp.exp(b[t] - b_mid) for t in range(nt)]
    k_rel = [k[t] * jnp.exp(b_mid - b[t]) for t in range(nt)]
    intra = []
    for t in range(nt):
        acc = None
        for s in range(t + 1):
            term = jnp.sum(q_rel[t] * k_rel[s], axis=-1, keepdims=True) * v[s]
            acc = term if acc is None else acc + term
        intra.append(acc)
    q_dec = jnp.concatenate([q[t] * jnp.exp(b[t]) for t in range(nt)], axis=0).astype(BF16)
    k_end = jnp.concatenate([k[t] * jnp.exp(b_last - b[t]) for t in range(nt)], axis=0)
    v_all = jnp.concatenate(v, axis=0).astype(BF16)
    decay = jnp.exp(b_last)

    def new_state(i, st, upd):
        dcol = _lane_replicated_column(decay[i:i + 1, :])
        return jnp.concatenate([dcol] * (GLA_DV // LANES), axis=1) * st + upd

    return _Decode(intra, q_dec, k_end, v_all, new_state, s_ref, so_ref, nt, bt)


def _ret_decode_prep(pw_ref, q_ref, k_ref, v_ref, cos_ref, sin_ref, s_ref, so_ref, *, nt, bt):
    h = pl.program_id(1)
    pw = [pw_ref[h, n] for n in range(nt + 1)]
    q = [_rope(q_ref[t], cos_ref[t:t + 1, :], sin_ref[t:t + 1, :]) for t in range(nt)]
    k = [_rope(k_ref[t], cos_ref[t:t + 1, :], sin_ref[t:t + 1, :]) * (RET_DK ** -0.5) for t in range(nt)]
    v = [v_ref[t] for t in range(nt)]
    intra = []
    for t in range(nt):
        acc = None
        for s in range(t + 1):
            term = (jnp.sum(q[t] * k[s], axis=-1, keepdims=True) * pw[t - s]) * v[s]
            acc = term if acc is None else acc + term
        intra.append(acc)
    q_dec = jnp.concatenate([q[t] * pw[t + 1] for t in range(nt)], axis=0).astype(BF16)
    k_end = jnp.concatenate([k[t] * pw[nt - 1 - t] for t in range(nt)], axis=0)
    v_all = jnp.concatenate(v, axis=0).astype(BF16)
    return _Decode(intra, q_dec, k_end, v_all, lambda i, st, upd: pw[nt] * st + upd, s_ref, so_ref, nt, bt)


def kernel(x_prompt, x_sample, state_gla, state_ret, ffn1_norm, ffn1_w1, ffn1_w3, ffn1_w2, mix_norm, w_in, w_gate_up, b_gate, gla_norm, w_gla_o, ret_norm, w_ret_o, w_out, ffn2_norm, ffn2_w1, ffn2_w3, ffn2_w2, final_norm):
    depth = w_in.shape[0]
    batch, seq, d = x_prompt.shape
    nb, nt, _ = x_sample.shape

    hp = x_prompt.reshape(batch * seq, d)
    hs = x_sample.transpose(1, 0, 2).reshape(nt * nb, d)
    gla_p, ret_p, gla_s, ret_s = [], [], [], []
    for l in range(depth):
        last = l == depth - 1
        wl = w_in[l].T
        w_ga = w_in[l][:, WA_WIDTH:WB_START].astype(BF16)
        wgu = w_gate_up[l].astype(BF16)
        bg = b_gate[l].reshape(1, GLA_QK)
        ng = gla_norm[l].reshape(GLA_HEADS, 1, GLA_DV)
        nr = ret_norm[l].reshape(RET_HEADS, 1, RET_DV)
        g_next = final_norm if last else ffn1_norm[l + 1]

        hs, us, *f1 = _ffn(hs, ffn1_norm[l], ffn1_w1[l], ffn1_w3[l], ffn1_w2[l], mix_norm[l], final=False)
        hp, up = _ffn(hp, ffn1_norm[l], *f1, mix_norm[l], final=False)

        pg, w_a = _proj(us, wl, 0, WA_WIDTH)
        pr, w_b = _proj(us, wl, WB_START, WB_ZA)
        la = _gate(us, w_ga, wgu, bg)
        la_p = _gate(up, w_ga, wgu, bg)
        og, sg, og_s, sg_s = _gla_mixer(up, la_p, batch, seq, w_a, ng,
                                        pg.reshape(nt, nb, -1), la.reshape(nt, nb, -1), state_gla[l])
        orr, sr, or_s, sr_s = _ret_mixer(up, batch, seq, w_b, nr, pr.reshape(nt, nb, -1), state_ret[l])
        gla_s.append(sg_s)
        ret_s.append(sr_s)
        gla_p.append(sg)
        ret_p.append(sr)

        ms, wza, wzb, wgo, wro = _merge_gate(us, og_s.reshape(nt * nb, GLA_V), or_s.reshape(nt * nb, RET_V),
                                             wl, w_gla_o[l], w_ret_o[l])
        mp, = _merge_gate(up, og, orr, (wza, wzb), wgo, wro)
        hs, wout = _out_proj(hs, ms, w_out[l])
        hp, = _out_proj(hp, mp, wout)

        hs, *f2 = _ffn(hs, ffn2_norm[l], ffn2_w1[l], ffn2_w3[l], ffn2_w2[l], g_next, final=last)
        hp, *_ = _ffn(hp, ffn2_norm[l], *f2[-3:], g_next, final=last)

    y_prompt = hp.reshape(batch, seq, d)
    y_sample = hs.reshape(nt, nb, d).transpose(1, 0, 2)
    return (y_prompt, y_sample, jnp.stack(gla_p), jnp.stack(ret_p), jnp.stack(gla_s), jnp.stack(ret_s))
```

```python
import functools

import jax
import jax.numpy as jnp
from jax import lax
from jax.experimental import pallas as pl
from jax.experimental.pallas import tpu as pltpu

F32, BF16 = jnp.float32, jnp.bfloat16

D_MODEL = 2048
PAST_LEN = 16384
GLA_HEADS = 4
GLA_DK = D_MODEL // (2 * GLA_HEADS)
GLA_DV = D_MODEL // GLA_HEADS
GLA_GATE_RANK = 16
GLA_GATE_TAU = 16.0
RET_HEADS = 8
RET_DK = D_MODEL // RET_HEADS
RET_DV = 2 * D_MODEL // RET_HEADS
ROPE_BASE = 10000.0
EPS = 1e-6
GLA_QK = GLA_HEADS * GLA_DK
GLA_V = GLA_HEADS * GLA_DV
RET_QK = RET_HEADS * RET_DK
RET_V = RET_HEADS * RET_DV

WA_Q, WA_K, WA_V, WA_G = 0, GLA_QK, 2 * GLA_QK, 2 * GLA_QK + GLA_V
WA_WIDTH = 2 * GLA_QK + 2 * GLA_V
WB_Q, WB_K, WB_V, WB_G = 0, RET_QK, 2 * RET_QK, 2 * RET_QK + RET_V
WB_ZA = 2 * RET_QK + 2 * RET_V
WB_ZB = WB_ZA + D_MODEL
WB_START = WA_WIDTH + GLA_GATE_RANK

LANES = 128
SUBLANES = 8
VMEM_LIMIT = 58 * 2**20

GLA_CHUNK = 64
RET_CHUNK = 256
SEQ_TILE = 512


def _cparams(*sem):
    return pltpu.CompilerParams(dimension_semantics=sem, vmem_limit_bytes=VMEM_LIMIT)


def _dot(a, b):
    return jnp.dot(a, b, preferred_element_type=F32)


def _dot_tb(a, b):
    return lax.dot_general(a, b, (((1,), (1,)), ((), ())), preferred_element_type=F32)


def _dot_ta(a, b):
    return lax.dot_general(a, b, (((0,), (0,)), ((), ())), preferred_element_type=F32)


def _rms(x, g):
    return x * lax.rsqrt(jnp.mean(x * x, axis=-1, keepdims=True) + EPS) * g


def _sigmoid(x):
    return 1.0 / (1.0 + jnp.exp(-x))


def _silu(x):
    return x * _sigmoid(x)


def _log_sigmoid(x):
    return jnp.minimum(x, 0.0) - jnp.log1p(jnp.exp(-jnp.abs(x)))


def _lane_replicated_column(row):
    return jnp.broadcast_to(row, (LANES, row.shape[-1])).T


def _ffn_body(x_ref, g_ref, w1_ref, w3_ref, w2_ref, g2_ref, *refs, tm, rows, final, emit_w):
    refs = list(refs)
    u_scr = refs.pop()
    wb_refs = [refs.pop() for _ in range(3)][::-1] if emit_w else None
    if final:
        y_ref, = refs
        acc_ref = y_ref
    else:
        h_ref, un_ref = refs
        acc_ref = h_ref
    j = pl.program_id(1)

    @pl.when(j == 0)
    def _():
        for r in range(0, tm, rows):
            u_scr[r:r + rows, :] = _rms(x_ref[r:r + rows, :], g_ref[...]).astype(BF16)
            acc_ref[r:r + rows, :] = jnp.zeros((rows, acc_ref.shape[1]), F32)

    w1, w3, w2 = w1_ref[...].astype(BF16), w3_ref[...].astype(BF16), w2_ref[...].astype(BF16)
    if emit_w:
        for ref, w in zip(wb_refs, (w1, w3, w2)):
            ref[...] = w
    u = u_scr[...]
    a = _dot(u, w1)
    b = _dot(u, w3)
    acc_ref[...] += _dot((_silu(a) * b).astype(BF16), w2)

    @pl.when(j == pl.num_programs(1) - 1)
    def _():
        for r in range(0, tm, rows):
            h = x_ref[r:r + rows, :] + 0.5 * acc_ref[r:r + rows, :]
            if final:
                y_ref[r:r + rows, :] = _rms(h, g2_ref[...])
            else:
                h_ref[r:r + rows, :] = h
                un_ref[r:r + rows, :] = _rms(h, g2_ref[...]).astype(BF16)


def _ffn(x, g, w1, w3, w2, g2, *, final, tm=512):
    t, d = x.shape
    dff = w1.shape[1]
    tm = min(tm, t)
    emit_w = w1.dtype == F32
    assert not emit_w or t == tm, "weights are emitted by a single-token-tile call"
    tf = 256 if emit_w else 512
    grid = (t // tm, dff // tf)
    row = pl.BlockSpec((tm, d), lambda i, j: (i, 0))
    vec = pl.BlockSpec((1, d), lambda i, j: (0, 0))
    w_up = pl.BlockSpec((d, tf), lambda i, j: (0, j))
    w_dn = pl.BlockSpec((tf, d), lambda i, j: (j, 0))
    out_shape = [jax.ShapeDtypeStruct((t, d), F32)] + ([] if final else [jax.ShapeDtypeStruct((t, d), BF16)])
    out_specs = [row] * len(out_shape)
    if emit_w:
        out_shape += [jax.ShapeDtypeStruct(w.shape, BF16) for w in (w1, w3, w2)]
        out_specs += [w_up, w_up, w_dn]
    return pl.pallas_call(
        functools.partial(_ffn_body, tm=tm, rows=min(256, tm), final=final, emit_w=emit_w),
        grid=grid,
        in_specs=[row, vec, w_up, w_up, w_dn, vec],
        out_specs=out_specs,
        out_shape=out_shape,
        scratch_shapes=[pltpu.VMEM((tm, d), BF16)],
        compiler_params=_cparams("parallel", "arbitrary"),
        name=("ffn_final" if final else "ffn_mid") + ("_castw" if emit_w else ""),
    )(x, g.reshape(1, d), w1, w3, w2, g2.reshape(1, d))


def _w_cols(a_ref, b_ref):
    rows = a_ref[...] if b_ref is None else jnp.concatenate([a_ref[b_ref.shape[0]:, :], b_ref[...]], axis=0)
    return rows.T.astype(BF16)


def _w_cols_specs(width, tn, start, col_of):
    shift = start % tn
    base = start - shift
    if shift == 0:
        return [pl.BlockSpec((tn, width), lambda *g: (base // tn + col_of(*g), 0))]
    assert shift % SUBLANES == 0 and tn % shift == 0 and base % shift == 0
    return [pl.BlockSpec((tn, width), lambda *g: (base // tn + col_of(*g), 0)),
            pl.BlockSpec((shift, width), lambda *g: ((base + tn * (col_of(*g) + 1)) // shift, 0))]


def _merge_body(u_ref, og_ref, or_ref, *refs, emit_w):
    u = u_ref[...]
    if not emit_w:
        wza_ref, wzb_ref, wgo_ref, wro_ref, m_ref = refs
        wza, wzb, wgo, wro = wza_ref[...], wzb_ref[...], wgo_ref[...], wro_ref[...]
    else:
        za_a, za_b, zb_a, zb_b, wgo_ref, wro_ref, m_ref, wza_o, wzb_o, wgo_o, wro_o = refs
        wza = _w_cols(za_a, za_b)
        wzb = _w_cols(zb_a, zb_b)
        wgo, wro = wgo_ref[...].astype(BF16), wro_ref[...].astype(BF16)
        wza_o[...], wzb_o[...], wgo_o[...], wro_o[...] = wza, wzb, wgo, wro
    za = _dot(u, wza)
    zb = _dot(u, wzb)
    branch_a = _dot(og_ref[...].astype(BF16), wgo)
    branch_b = _dot(or_ref[...].astype(BF16), wro)
    m_ref[...] = (_sigmoid(za) * branch_a + _sigmoid(zb) * branch_b).astype(BF16)


def _merge_gate(u, og, orr, wz, wgo, wro, *, tm=512):
    t, d = u.shape
    tm = min(tm, t)
    emit_w = wgo.dtype == F32
    assert not emit_w or t == tm, "weights are emitted by a single-token-tile call"
    tn = 256 if emit_w else 512
    col = lambda i, j: j
    acts = [pl.BlockSpec((tm, d), lambda i, j: (i, 0)),
            pl.BlockSpec((tm, GLA_V), lambda i, j: (i, 0)),
            pl.BlockSpec((tm, RET_V), lambda i, j: (i, 0))]
    w_col = lambda rows: pl.BlockSpec((rows, tn), lambda i, j: (0, j))
    out_shape = [jax.ShapeDtypeStruct((t, d), BF16)]
    out_specs = [pl.BlockSpec((tm, tn), lambda i, j: (i, j))]
    if emit_w:
        w_specs = (_w_cols_specs(d, tn, WB_START + WB_ZA, col) + _w_cols_specs(d, tn, WB_START + WB_ZB, col)
                   + [w_col(GLA_V), w_col(RET_V)])
        w_args = (wz, wz, wz, wz, wgo, wro)
        out_shape += [jax.ShapeDtypeStruct((d, d), BF16), jax.ShapeDtypeStruct((d, d), BF16),
                      jax.ShapeDtypeStruct(wgo.shape, BF16), jax.ShapeDtypeStruct(wro.shape, BF16)]
        out_specs += [w_col(d), w_col(d), w_col(GLA_V), w_col(RET_V)]
    else:
        w_specs, w_args = [w_col(d), w_col(d), w_col(GLA_V), w_col(RET_V)], (*wz, wgo, wro)
    return pl.pallas_call(
        functools.partial(_merge_body, emit_w=emit_w),
        grid=(t // tm, d // tn),
        in_specs=acts + w_specs,
        out_specs=out_specs,
        out_shape=out_shape,
        compiler_params=_cparams("parallel", "parallel"),
        name="merge_gate" + ("_castw" if emit_w else ""),
    )(u, og, orr, *w_args)


def _out_body(h_ref, m_ref, wout_ref, o_ref, *wb_ref):
    wout = wout_ref[...].astype(BF16)
    if wb_ref:
        wb_ref[0][...] = wout
    o_ref[...] = h_ref[...] + _dot(m_ref[...], wout)


def _out_proj(h, merged, wout, *, tm=512):
    t, d = h.shape
    tm = min(tm, t)
    emit_w = wout.dtype == F32
    assert not emit_w or t == tm, "weights are emitted by a single-token-tile call"
    tn = 512 if emit_w else d
    out_shape = [jax.ShapeDtypeStruct((t, d), F32)] + ([jax.ShapeDtypeStruct((d, d), BF16)] if emit_w else [])
    out_specs = [pl.BlockSpec((tm, tn), lambda i, j: (i, j))] + ([pl.BlockSpec((d, tn), lambda i, j: (0, j))] if emit_w else [])
    return pl.pallas_call(
        _out_body,
        grid=(t // tm, d // tn),
        in_specs=[pl.BlockSpec((tm, tn), lambda i, j: (i, j)),
                  pl.BlockSpec((tm, d), lambda i, j: (i, 0)),
                  pl.BlockSpec((d, tn), lambda i, j: (0, j))],
        out_specs=out_specs,
        out_shape=out_shape,
        compiler_params=_cparams("parallel", "parallel"),
        name="out_proj" + ("_castw" if emit_w else ""),
    )(h, merged, wout)


def _gla_body(u_ref, wq_ref, wk_ref, wv_ref, wg_ref, la_ref, ng_ref,
              dq_ref, dk_ref, dv_ref, dg_ref, dla_ref, ds_ref,
              og_ref, st_ref, dog_ref, dso_ref, s_scr, *, ts, c, nt, bt):
    s = pl.program_id(2)

    @pl.when(s == 0)
    def _():
        s_scr[...] = jnp.zeros_like(s_scr)

    dec = _gla_decode_prep(dq_ref, dk_ref, dv_ref, dla_ref, ds_ref, dso_ref, nt=nt, bt=bt)
    dec.units(0, bt)
    dec.finish(dg_ref, ng_ref, dog_ref)

    u = u_ref[...]
    q_all = _dot(u, wq_ref[...]) * (GLA_DK ** -0.5)
    k_all = _dot(u, wk_ref[...])
    v_all = _dot(u, wv_ref[...]).astype(BF16)
    g_all = _dot(u, wg_ref[...])
    la_all = la_ref[...]
    la_hi = la_all.astype(BF16)
    la_lo = (la_all - la_hi.astype(F32)).astype(BF16)

    rr = lax.broadcasted_iota(jnp.int32, (c, c), 0)
    cc = lax.broadcasted_iota(jnp.int32, (c, c), 1)
    causal = rr >= cc
    tril = causal.astype(BF16)
    mid = c // 2
    rows = [slice(ci * c, (ci + 1) * c) for ci in range(ts // c)]

    b = [_dot(tril, la_hi[r]) + _dot(tril, la_lo[r]) for r in rows]
    q_rel, k_rel, q_dec, k_end, decay = [], [], [], [], []
    for r, bc in zip(rows, b):
        q, k = q_all[r], k_all[r]
        b_mid, b_last = bc[mid:mid + 1, :], bc[c - 1:c, :]
        q_rel.append((q * jnp.exp(bc - b_mid)).astype(BF16))
        k_rel.append((k * jnp.exp(b_mid - bc)).astype(BF16))
        q_dec.append((q * jnp.exp(bc)).astype(BF16))
        k_end.append((k * jnp.exp(b_last - bc)).astype(BF16))
        dcol = _lane_replicated_column(jnp.exp(b_last))
        decay.append(jnp.concatenate([dcol] * (GLA_DV // LANES), axis=1))
    att = [jnp.where(causal, _dot_tb(qr, kr), 0.0).astype(BF16) for qr, kr in zip(q_rel, k_rel)]
    o_intra = [_dot(a, v_all[r]) for a, r in zip(att, rows)]
    upd = [_dot_ta(ke, v_all[r]) for ke, r in zip(k_end, rows)]
    st = s_scr[...]
    for ci, r in enumerate(rows):
        o = o_intra[ci] + _dot(q_dec[ci], st.astype(BF16))
        og_ref[r, :] = (_rms(o, ng_ref[...]) * _silu(g_all[r])).astype(BF16)
        st = decay[ci] * st + upd[ci]
    s_scr[...] = st

    @pl.when(s == pl.num_programs(2) - 1)
    def _():
        st_ref[...] = st


def _decode_tile(batch, ns, nb):
    bt = nb // (batch * ns)
    assert bt * batch * ns == nb and bt % SUBLANES == 0
    return bt


def _gla_mixer(u, la_p, batch, seq, w_a, ng, pg, la, state):
    ts, c = SEQ_TILE, GLA_CHUNK
    ns = seq // ts
    d = u.shape[1]
    nt, nb, _ = pg.shape
    bt = _decode_tile(batch, ns, nb)
    kq, kk, kv, kg = WA_Q // GLA_DK, WA_K // GLA_DK, WA_V // GLA_DV, WA_G // GLA_DV
    dec = lambda width, col0: pl.BlockSpec((nt, bt, width), lambda b, h, s: (0, b * ns + s, col0 + h))
    st_spec = pl.BlockSpec((bt, None, GLA_DK, GLA_DV), lambda b, h, s: (b * ns + s, h, 0, 0))
    return pl.pallas_call(
        functools.partial(_gla_body, ts=ts, c=c, nt=nt, bt=bt),
        grid=(batch, GLA_HEADS, ns),
        in_specs=[pl.BlockSpec((ts, d), lambda b, h, s: (b * ns + s, 0)),
                  pl.BlockSpec((d, GLA_DK), lambda b, h, s: (0, kq + h)),
                  pl.BlockSpec((d, GLA_DK), lambda b, h, s: (0, kk + h)),
                  pl.BlockSpec((d, GLA_DV), lambda b, h, s: (0, kv + h)),
                  pl.BlockSpec((d, GLA_DV), lambda b, h, s: (0, kg + h)),
                  pl.BlockSpec((ts, GLA_DK), lambda b, h, s: (b * ns + s, h)),
                  pl.BlockSpec((None, 1, GLA_DV), lambda b, h, s: (h, 0, 0)),
                  dec(GLA_DK, kq), dec(GLA_DK, kk), dec(GLA_DV, kv), dec(GLA_DV, kg), dec(GLA_DK, 0), st_spec],
        out_specs=(pl.BlockSpec((ts, GLA_DV), lambda b, h, s: (b * ns + s, h)),
                   pl.BlockSpec((None, None, GLA_DK, GLA_DV), lambda b, h, s: (b, h, 0, 0)),
                   dec(GLA_DV, 0), st_spec),
        out_shape=(jax.ShapeDtypeStruct((batch * seq, GLA_V), BF16),
                   jax.ShapeDtypeStruct((batch, GLA_HEADS, GLA_DK, GLA_DV), F32),
                   jax.ShapeDtypeStruct((nt, nb, GLA_V), F32),
                   jax.ShapeDtypeStruct(state.shape, F32)),
        scratch_shapes=[pltpu.VMEM((GLA_DK, GLA_DV), F32)],
        compiler_params=_cparams("parallel", "parallel", "arbitrary"),
        name="gla_mixer",
    )(u, w_a, w_a, w_a, w_a, la_p, ng, pg, pg, pg, pg, la, state)


def _rope(x, cos, sin):
    half = x.shape[-1] // 2
    x1, x2 = x[:, :half], x[:, half:]
    return jnp.concatenate([x1 * cos - x2 * sin, x2 * cos + x1 * sin], axis=-1)


def _ret_body(u_ref, wq_ref, wk_ref, wv_ref, wg_ref, cos_ref, sin_ref, dm_ref, qd_ref, kd_ref, cd_ref, ng_ref,
              pw_ref, dq_ref, dk_ref, dv_ref, dg_ref, dcos_ref, dsin_ref, ds_ref,
              or_ref, st_ref, dor_ref, dso_ref, s_scr, *, ts, c, nt, bt):
    s = pl.program_id(2)

    @pl.when(s == 0)
    def _():
        s_scr[...] = jnp.zeros_like(s_scr)

    dec = _ret_decode_prep(pw_ref, dq_ref, dk_ref, dv_ref, dcos_ref, dsin_ref, ds_ref, dso_ref, nt=nt, bt=bt)
    cut = [bt * n // 4 for n in range(5)]

    u = u_ref[...]
    tok = pl.ds(pl.multiple_of(s * ts, ts), ts)
    cos, sin = cos_ref[tok, :], sin_ref[tok, :]
    q_all = _rope(_dot(u, wq_ref[...]), cos, sin)
    dec.units(cut[0], cut[1])
    k_all = _rope(_dot(u, wk_ref[...]), cos, sin) * (RET_DK ** -0.5)
    dec.units(cut[1], cut[2])
    v_all = _dot(u, wv_ref[...]).astype(BF16)
    dec.units(cut[2], cut[3])
    g_all = _dot(u, wg_ref[...])
    dec.units(cut[3], cut[4])
    dec.finish(dg_ref, ng_ref, dor_ref)

    rows = [slice(ci * c, (ci + 1) * c) for ci in range(ts // c)]
    att = [(_dot_tb(q_all[r].astype(BF16), k_all[r].astype(BF16)) * dm_ref[...]).astype(BF16) for r in rows]
    o_intra = [_dot(a, v_all[r]) for a, r in zip(att, rows)]
    upd = [_dot_ta((k_all[r] * kd_ref[...]).astype(BF16), v_all[r]) for r in rows]
    q_dec = [(q_all[r] * qd_ref[...]).astype(BF16) for r in rows]
    st = s_scr[...]
    for ci, r in enumerate(rows):
        o = o_intra[ci] + _dot(q_dec[ci], st.astype(BF16))
        or_ref[r, :] = (_rms(o, ng_ref[...]) * _silu(g_all[r])).astype(BF16)
        st = cd_ref[...] * st + upd[ci]
    s_scr[...] = st

    @pl.when(s == pl.num_programs(2) - 1)
    def _():
        st_ref[...] = st


def _ret_log_gamma():
    return jnp.log1p(-jnp.exp2(-5.0 - jnp.arange(RET_HEADS, dtype=F32)))


def _rope_tables(pos):
    half = RET_DK // 2
    freqs = ROPE_BASE ** (-jnp.arange(half, dtype=F32) / half)
    ang = pos[:, None] * freqs[None, :]
    return jnp.cos(ang), jnp.sin(ang)


def _ret_mixer(u, batch, seq, w_b, ng, pr, state):
    ts, c = SEQ_TILE, RET_CHUNK
    ns = seq // ts
    d = u.shape[1]
    nt, nb, _ = pr.shape
    bt = _decode_tile(batch, ns, nb)
    kq, kk, kv, kg = WB_Q // RET_DK, WB_K // RET_DK, WB_V // RET_DV, WB_G // RET_DV
    cos, sin = _rope_tables(jnp.arange(seq, dtype=F32))
    dcos, dsin = _rope_tables(PAST_LEN + jnp.arange(nt, dtype=F32))
    pw = jnp.exp(_ret_log_gamma()[:, None] * jnp.arange(nt + 1, dtype=F32)[None, :])
    dec = lambda width, col0: pl.BlockSpec((nt, bt, width), lambda b, h, s: (0, b * ns + s, col0 + h))
    rope_spec = pl.BlockSpec((nt, RET_DK // 2), lambda b, h, s: (0, 0))
    st_spec = pl.BlockSpec((bt, None, RET_DK, RET_DV), lambda b, h, s: (b * ns + s, h, 0, 0))
    lg = _ret_log_gamma()
    idx = jnp.arange(c, dtype=F32)
    diff = idx[:, None] - idx[None, :]
    dmat = jnp.where(diff >= 0, jnp.exp(lg[:, None, None] * jnp.maximum(diff, 0.0)), 0.0)
    qdec = jnp.broadcast_to(jnp.exp(lg[:, None] * (idx + 1.0)[None, :])[:, :, None], (RET_HEADS, c, RET_DK))
    kdec = jnp.broadcast_to(jnp.exp(lg[:, None] * (c - 1.0 - idx)[None, :])[:, :, None], (RET_HEADS, c, RET_DK))
    cdec = jnp.broadcast_to(jnp.exp(lg * c)[:, None, None], (RET_HEADS, 1, RET_DV))
    return pl.pallas_call(
        functools.partial(_ret_body, ts=ts, c=c, nt=nt, bt=bt),
        grid=(batch, RET_HEADS, ns),
        in_specs=[pl.BlockSpec((ts, d), lambda b, h, s: (b * ns + s, 0)),
                  pl.BlockSpec((d, RET_DK), lambda b, h, s: (0, kq + h)),
                  pl.BlockSpec((d, RET_DK), lambda b, h, s: (0, kk + h)),
                  pl.BlockSpec((d, RET_DV), lambda b, h, s: (0, kv + h)),
                  pl.BlockSpec((d, RET_DV), lambda b, h, s: (0, kg + h)),
                  pl.BlockSpec((seq, RET_DK // 2), lambda b, h, s: (0, 0)),
                  pl.BlockSpec((seq, RET_DK // 2), lambda b, h, s: (0, 0)),
                  pl.BlockSpec((None, c, c), lambda b, h, s: (h, 0, 0)),
                  pl.BlockSpec((None, c, RET_DK), lambda b, h, s: (h, 0, 0)),
                  pl.BlockSpec((None, c, RET_DK), lambda b, h, s: (h, 0, 0)),
                  pl.BlockSpec((None, 1, RET_DV), lambda b, h, s: (h, 0, 0)),
                  pl.BlockSpec((None, 1, RET_DV), lambda b, h, s: (h, 0, 0)),
                  pl.BlockSpec(memory_space=pltpu.SMEM),
                  dec(RET_DK, kq), dec(RET_DK, kk), dec(RET_DV, kv), dec(RET_DV, kg), rope_spec, rope_spec, st_spec],
        out_specs=(pl.BlockSpec((ts, RET_DV), lambda b, h, s: (b * ns + s, h)),
                   pl.BlockSpec((None, None, RET_DK, RET_DV), lambda b, h, s: (b, h, 0, 0)),
                   dec(RET_DV, 0), st_spec),
        out_shape=(jax.ShapeDtypeStruct((batch * seq, RET_V), BF16),
                   jax.ShapeDtypeStruct((batch, RET_HEADS, RET_DK, RET_DV), F32),
                   jax.ShapeDtypeStruct((nt, nb, RET_V), F32),
                   jax.ShapeDtypeStruct(state.shape, F32)),
        scratch_shapes=[pltpu.VMEM((RET_DK, RET_DV), F32)],
        compiler_params=_cparams("parallel", "parallel", "arbitrary"),
        name="ret_mixer",
    )(u, w_b, w_b, w_b, w_b, cos, sin, dmat, qdec, kdec, cdec, ng, pw, pr, pr, pr, pr, dcos, dsin, state)


def _proj_body(u_ref, *refs):
    *w_refs, o_ref, wb_ref = refs
    w = _w_cols(w_refs[0], w_refs[1] if len(w_refs) > 1 else None)
    wb_ref[...] = w
    o_ref[...] = _dot(u_ref[...], w)


def _proj(u, w_t, start, n, *, tn=1024):
    t, d = u.shape
    w_specs = _w_cols_specs(d, tn, start, lambda j: j)
    return pl.pallas_call(
        _proj_body,
        grid=(n // tn,),
        in_specs=[pl.BlockSpec((t, d), lambda j: (0, 0))] + w_specs,
        out_specs=[pl.BlockSpec((t, tn), lambda j: (0, j)), pl.BlockSpec((d, tn), lambda j: (0, j))],
        out_shape=[jax.ShapeDtypeStruct((t, n), F32), jax.ShapeDtypeStruct((d, n), BF16)],
        compiler_params=_cparams("parallel"),
        name="sample_proj_castw",
    )(u, *([w_t] * len(w_specs)))


def _gate_body(u_ref, wga_ref, wgu_ref, bg_ref, o_ref):
    ga = _dot(u_ref[...], wga_ref[...])
    logit = _dot(ga.astype(BF16), wgu_ref[...]) + bg_ref[...]
    o_ref[...] = _log_sigmoid(logit) / GLA_GATE_TAU


def _gate(u, wga, wgu, bg, *, tm=512):
    t, d = u.shape
    return pl.pallas_call(
        _gate_body,
        grid=(t // tm,),
        in_specs=[pl.BlockSpec((tm, d), lambda i: (i, 0)),
                  pl.BlockSpec(wga.shape, lambda i: (0, 0)),
                  pl.BlockSpec(wgu.shape, lambda i: (0, 0)),
                  pl.BlockSpec(bg.shape, lambda i: (0, 0))],
        out_specs=pl.BlockSpec((tm, GLA_QK), lambda i: (i, 0)),
        out_shape=jax.ShapeDtypeStruct((t, GLA_QK), F32),
        compiler_params=_cparams("parallel"),
        name="log_gate",
    )(u, wga, wgu, bg)


def _row_group_mask(rows, cols, bt, i):
    return (lax.broadcasted_iota(jnp.int32, (rows, cols), 0) % bt) == i


class _Decode:
    def __init__(self, intra, q_dec, k_end, v_all, new_state, s_ref, so_ref, nt, bt):
        self.intra, self.q_dec, self.k_end, self.v_all = intra, q_dec, k_end, v_all
        self.new_state, self.s_ref, self.so_ref, self.nt, self.bt = new_state, s_ref, so_ref, nt, bt
        self.inter = jnp.zeros((nt * bt, v_all.shape[1]), F32)

    def units(self, lo, hi):
        rows, bt = self.nt * self.bt, self.bt
        for i in range(lo, hi):
            st = self.s_ref[i]
            qs = _dot(self.q_dec, st.astype(BF16))
            self.inter = self.inter + jnp.where(_row_group_mask(rows, qs.shape[1], bt, i), qs, 0.0)
            k_i = jnp.where(_row_group_mask(rows, self.k_end.shape[1], bt, i), self.k_end, 0.0).astype(BF16)
            self.so_ref[i] = self.new_state(i, st, _dot_ta(k_i, self.v_all))

    def finish(self, g_ref, ng_ref, o_ref):
        bt = self.bt
        for t in range(self.nt):
            o = self.intra[t] + self.inter[t * bt:(t + 1) * bt, :]
            o_ref[t] = _rms(o, ng_ref[...]) * _silu(g_ref[t])


def _gla_decode_prep(q_ref, k_ref, v_ref, la_ref, s_ref, so_ref, *, nt, bt):
    q = [q_ref[t] * (GLA_DK ** -0.5) for t in range(nt)]
    k = [k_ref[t] for t in range(nt)]
    v = [v_ref[t] for t in range(nt)]
    b = [la_ref[0]]
    for t in range(1, nt):
        b.append(b[-1] + la_ref[t])
    b_mid, b_last = b[nt // 2], b[nt - 1]
    q_rel = [q[t] * jnp.exp(b[t] - b_mid) for t in range(nt)]
    k_rel = [k[t] * jnp.exp(b_mid - b[t]) for t in range(nt)]
    intra = []
    for t in range(nt):
        acc = None
        for s in range(t + 1):
            term = jnp.sum(q_rel[t] * k_rel[s], axis=-1, keepdims=True) * v[s]
            acc = term if acc is None else acc + term
        intra.append(acc)
    q_dec = jnp.concatenate([q[t] * jnp.exp(b[t]) for t in range(nt)], axis=0).astype(BF16)
    k_end = jnp.concatenate([k[t] * jnp.exp(b_last - b[t]) for t in range(nt)], axis=0)
    v_all = jnp.concatenate(v, axis=0).astype(BF16)
    decay = jnp.exp(b_last)

    def new_state(i, st, upd):
        dcol = _lane_replicated_column(decay[i:i + 1, :])
        return jnp.concatenate([dcol] * (GLA_DV // LANES), axis=1) * st + upd

    return _Decode(intra, q_dec, k_end, v_all, new_state, s_ref, so_ref, nt, bt)


def _ret_decode_prep(pw_ref, q_ref, k_ref, v_ref, cos_ref, sin_ref, s_ref, so_ref, *, nt, bt):
    h = pl.program_id(1)
    pw = [pw_ref[h, n] for n in range(nt + 1)]
    q = [_rope(q_ref[t], cos_ref[t:t + 1, :], sin_ref[t:t + 1, :]) for t in range(nt)]
    k = [_rope(k_ref[t], cos_ref[t:t + 1, :], sin_ref[t:t + 1, :]) * (RET_DK ** -0.5) for t in range(nt)]
    v = [v_ref[t] for t in range(nt)]
    intra = []
    for t in range(nt):
        acc = None
        for s in range(t + 1):
            term = (jnp.sum(q[t] * k[s], axis=-1, keepdims=True) * pw[t - s]) * v[s]
            acc = term if acc is None else acc + term
        intra.append(acc)
    q_dec = jnp.concatenate([q[t] * pw[t + 1] for t in range(nt)], axis=0).astype(BF16)
    k_end = jnp.concatenate([k[t] * pw[nt - 1 - t] for t in range(nt)], axis=0)
    v_all = jnp.concatenate(v, axis=0).astype(BF16)
    return _Decode(intra, q_dec, k_end, v_all, lambda i, st, upd: pw[nt] * st + upd, s_ref, so_ref, nt, bt)


def kernel(x_prompt, x_sample, state_gla, state_ret, ffn1_norm, ffn1_w1, ffn1_w3, ffn1_w2, mix_norm, w_in, w_gate_up, b_gate, gla_norm, w_gla_o, ret_norm, w_ret_o, w_out, ffn2_norm, ffn2_w1, ffn2_w3, ffn2_w2, final_norm):
    depth = w_in.shape[0]
    batch, seq, d = x_prompt.shape
    nb, nt, _ = x_sample.shape

    hp = x_prompt.reshape(batch * seq, d)
    hs = x_sample.transpose(1, 0, 2).reshape(nt * nb, d)
    gla_p, ret_p, gla_s, ret_s = [], [], [], []
    for l in range(depth):
        last = l == depth - 1
        wl = w_in[l].T
        w_ga = w_in[l][:, WA_WIDTH:WB_START].astype(BF16)
        wgu = w_gate_up[l].astype(BF16)
        bg = b_gate[l].reshape(1, GLA_QK)
        ng = gla_norm[l].reshape(GLA_HEADS, 1, GLA_DV)
        nr = ret_norm[l].reshape(RET_HEADS, 1, RET_DV)
        g_next = final_norm if last else ffn1_norm[l + 1]

        hs, us, *f1 = _ffn(hs, ffn1_norm[l], ffn1_w1[l], ffn1_w3[l], ffn1_w2[l], mix_norm[l], final=False)
        hp, up = _ffn(hp, ffn1_norm[l], *f1, mix_norm[l], final=False)

        pg, w_a = _proj(us, wl, 0, WA_WIDTH)
        pr, w_b = _proj(us, wl, WB_START, WB_ZA)
        la = _gate(us, w_ga, wgu, bg)
        la_p = _gate(up, w_ga, wgu, bg)
        og, sg, og_s, sg_s = _gla_mixer(up, la_p, batch, seq, w_a, ng,
                                        pg.reshape(nt, nb, -1), la.reshape(nt, nb, -1), state_gla[l])
        orr, sr, or_s, sr_s = _ret_mixer(up, batch, seq, w_b, nr, pr.reshape(nt, nb, -1), state_ret[l])
        gla_s.append(sg_s)
        ret_s.append(sr_s)
        gla_p.append(sg)
        ret_p.append(sr)

        ms, wza, wzb, wgo, wro = _merge_gate(us, og_s.reshape(nt * nb, GLA_V), or_s.reshape(nt * nb, RET_V),
                                             wl, w_gla_o[l], w_ret_o[l])
        mp, = _merge_gate(up, og, orr, (wza, wzb), wgo, wro)
        hs, wout = _out_proj(hs, ms, w_out[l])
        hp, = _out_proj(hp, mp, wout)

        hs, *f2 = _ffn(hs, ffn2_norm[l], ffn2_w1[l], ffn2_w3[l], ffn2_w2[l], g_next, final=last)
        hp, *_ = _ffn(hp, ffn2_norm[l], *f2[-3:], g_next, final=last)

    y_prompt = hp.reshape(batch, seq, d)
    y_sample = hs.reshape(nt, nb, d).transpose(1, 0, 2)
    return (y_prompt, y_sample, jnp.stack(gla_p), jnp.stack(ret_p), jnp.stack(gla_s), jnp.stack(ret_s))
```

```python
import functools

import jax
import jax.numpy as jnp
from jax import lax
from jax.experimental import pallas as pl
from jax.experimental.pallas import tpu as pltpu

F32, BF16 = jnp.float32, jnp.bfloat16

D_MODEL = 2048
PAST_LEN = 16384
GLA_HEADS = 4
GLA_DK = D_MODEL // (2 * GLA_HEADS)
GLA_DV = D_MODEL // GLA_HEADS
GLA_GATE_RANK = 16
GLA_GATE_TAU = 16.0
RET_HEADS = 8
RET_DK = D_MODEL // RET_HEADS
RET_DV = 2 * D_MODEL // RET_HEADS
ROPE_BASE = 10000.0
EPS = 1e-6
GLA_QK = GLA_HEADS * GLA_DK
GLA_V = GLA_HEADS * GLA_DV
RET_QK = RET_HEADS * RET_DK
RET_V = RET_HEADS * RET_DV

WA_Q, WA_K, WA_V, WA_G = 0, GLA_QK, 2 * GLA_QK, 2 * GLA_QK + GLA_V
WA_WIDTH = 2 * GLA_QK + 2 * GLA_V
WB_Q, WB_K, WB_V, WB_G = 0, RET_QK, 2 * RET_QK, 2 * RET_QK + RET_V
WB_ZA = 2 * RET_QK + 2 * RET_V
WB_ZB = WB_ZA + D_MODEL
WB_START = WA_WIDTH + GLA_GATE_RANK

LANES = 128
SUBLANES = 8
VMEM_LIMIT = 58 * 2**20

GLA_CHUNK = 64
RET_CHUNK = 256
SEQ_TILE = 512


def _cparams(*sem):
    return pltpu.CompilerParams(dimension_semantics=sem, vmem_limit_bytes=VMEM_LIMIT)


def _dot(a, b):
    return jnp.dot(a, b, preferred_element_type=F32)


def _dot_tb(a, b):
    return lax.dot_general(a, b, (((1,), (1,)), ((), ())), preferred_element_type=F32)


def _dot_ta(a, b):
    return lax.dot_general(a, b, (((0,), (0,)), ((), ())), preferred_element_type=F32)


def _rms(x, g):
    return x * lax.rsqrt(jnp.mean(x * x, axis=-1, keepdims=True) + EPS) * g


def _sigmoid(x):
    return 1.0 / (1.0 + jnp.exp(-x))


def _silu(x):
    return x * _sigmoid(x)


def _log_sigmoid(x):
    return jnp.minimum(x, 0.0) - jnp.log(1.0 + jnp.exp(-jnp.abs(x)))


def _lane_replicated_column(row):
    return jnp.broadcast_to(row, (LANES, row.shape[-1])).T


def _ffn_body(x_ref, g_ref, w1_ref, w3_ref, w2_ref, g2_ref, *refs, tm, rows, final, emit_w):
    refs = list(refs)
    u_scr = refs.pop()
    wb_refs = [refs.pop() for _ in range(3)][::-1] if emit_w else None
    if final:
        y_ref, = refs
        acc_ref = y_ref
    else:
        h_ref, un_ref = refs
        acc_ref = h_ref
    j = pl.program_id(1)

    @pl.when(j == 0)
    def _():
        for r in range(0, tm, rows):
            u_scr[r:r + rows, :] = _rms(x_ref[r:r + rows, :], g_ref[...]).astype(BF16)
            acc_ref[r:r + rows, :] = jnp.zeros((rows, acc_ref.shape[1]), F32)

    w1, w3, w2 = w1_ref[...].astype(BF16), w3_ref[...].astype(BF16), w2_ref[...].astype(BF16)
    if emit_w:
        for ref, w in zip(wb_refs, (w1, w3, w2)):
            ref[...] = w
    u = u_scr[...]
    a = _dot(u, w1)
    b = _dot(u, w3)
    acc_ref[...] += _dot((_silu(a) * b).astype(BF16), w2)

    @pl.when(j == pl.num_programs(1) - 1)
    def _():
        for r in range(0, tm, rows):
            h = x_ref[r:r + rows, :] + 0.5 * acc_ref[r:r + rows, :]
            if final:
                y_ref[r:r + rows, :] = _rms(h, g2_ref[...])
            else:
                h_ref[r:r + rows, :] = h
                un_ref[r:r + rows, :] = _rms(h, g2_ref[...]).astype(BF16)


def _ffn(x, g, w1, w3, w2, g2, *, final, tm=512):
    t, d = x.shape
    dff = w1.shape[1]
    tm = min(tm, t)
    emit_w = w1.dtype == F32
    assert not emit_w or t == tm, "weights are emitted by a single-token-tile call"
    tf = 256 if emit_w else 512
    grid = (t // tm, dff // tf)
    row = pl.BlockSpec((tm, d), lambda i, j: (i, 0))
    vec = pl.BlockSpec((1, d), lambda i, j: (0, 0))
    w_up = pl.BlockSpec((d, tf), lambda i, j: (0, j))
    w_dn = pl.BlockSpec((tf, d), lambda i, j: (j, 0))
    out_shape = [jax.ShapeDtypeStruct((t, d), F32)] + ([] if final else [jax.ShapeDtypeStruct((t, d), BF16)])
    out_specs = [row] * len(out_shape)
    if emit_w:
        out_shape += [jax.ShapeDtypeStruct(w.shape, BF16) for w in (w1, w3, w2)]
        out_specs += [w_up, w_up, w_dn]
    return pl.pallas_call(
        functools.partial(_ffn_body, tm=tm, rows=min(256, tm), final=final, emit_w=emit_w),
        grid=grid,
        in_specs=[row, vec, w_up, w_up, w_dn, vec],
        out_specs=out_specs,
        out_shape=out_shape,
        scratch_shapes=[pltpu.VMEM((tm, d), BF16)],
        compiler_params=_cparams("parallel", "arbitrary"),
        name=("ffn_final" if final else "ffn_mid") + ("_castw" if emit_w else ""),
    )(x, g.reshape(1, d), w1, w3, w2, g2.reshape(1, d))


def _w_cols(a_ref, b_ref):
    rows = a_ref[...] if b_ref is None else jnp.concatenate([a_ref[b_ref.shape[0]:, :], b_ref[...]], axis=0)
    return rows.T.astype(BF16)


def _w_cols_specs(width, tn, start, col_of):
    shift = start % tn
    base = start - shift
    if shift == 0:
        return [pl.BlockSpec((tn, width), lambda *g: (base // tn + col_of(*g), 0))]
    assert shift % SUBLANES == 0 and tn % shift == 0 and base % shift == 0
    return [pl.BlockSpec((tn, width), lambda *g: (base // tn + col_of(*g), 0)),
            pl.BlockSpec((shift, width), lambda *g: ((base + tn * (col_of(*g) + 1)) // shift, 0))]


def _merge_body(u_ref, og_ref, or_ref, *refs, emit_w):
    u = u_ref[...]
    if not emit_w:
        wza_ref, wzb_ref, wgo_ref, wro_ref, m_ref = refs
        wza, wzb, wgo, wro = wza_ref[...], wzb_ref[...], wgo_ref[...], wro_ref[...]
    else:
        za_a, za_b, zb_a, zb_b, wgo_ref, wro_ref, m_ref, wza_o, wzb_o, wgo_o, wro_o = refs
        wza = _w_cols(za_a, za_b)
        wzb = _w_cols(zb_a, zb_b)
        wgo, wro = wgo_ref[...].astype(BF16), wro_ref[...].astype(BF16)
        wza_o[...], wzb_o[...], wgo_o[...], wro_o[...] = wza, wzb, wgo, wro
    za = _dot(u, wza)
    zb = _dot(u, wzb)
    branch_a = _dot(og_ref[...].astype(BF16), wgo)
    branch_b = _dot(or_ref[...].astype(BF16), wro)
    m_ref[...] = (_sigmoid(za) * branch_a + _sigmoid(zb) * branch_b).astype(BF16)


def _merge_gate(u, og, orr, wz, wgo, wro, *, tm=512):
    t, d = u.shape
    tm = min(tm, t)
    emit_w = wgo.dtype == F32
    assert not emit_w or t == tm, "weights are emitted by a single-token-tile call"
    tn = 256 if emit_w else 512
    col = lambda i, j: j
    acts = [pl.BlockSpec((tm, d), lambda i, j: (i, 0)),
            pl.BlockSpec((tm, GLA_V), lambda i, j: (i, 0)),
            pl.BlockSpec((tm, RET_V), lambda i, j: (i, 0))]
    w_col = lambda rows: pl.BlockSpec((rows, tn), lambda i, j: (0, j))
    out_shape = [jax.ShapeDtypeStruct((t, d), BF16)]
    out_specs = [pl.BlockSpec((tm, tn), lambda i, j: (i, j))]
    if emit_w:
        w_specs = (_w_cols_specs(d, tn, WB_START + WB_ZA, col) + _w_cols_specs(d, tn, WB_START + WB_ZB, col)
                   + [w_col(GLA_V), w_col(RET_V)])
        w_args = (wz, wz, wz, wz, wgo, wro)
        out_shape += [jax.ShapeDtypeStruct((d, d), BF16), jax.ShapeDtypeStruct((d, d), BF16),
                      jax.ShapeDtypeStruct(wgo.shape, BF16), jax.ShapeDtypeStruct(wro.shape, BF16)]
        out_specs += [w_col(d), w_col(d), w_col(GLA_V), w_col(RET_V)]
    else:
        w_specs, w_args = [w_col(d), w_col(d), w_col(GLA_V), w_col(RET_V)], (*wz, wgo, wro)
    return pl.pallas_call(
        functools.partial(_merge_body, emit_w=emit_w),
        grid=(t // tm, d // tn),
        in_specs=acts + w_specs,
        out_specs=out_specs,
        out_shape=out_shape,
        compiler_params=_cparams("parallel", "parallel"),
        name="merge_gate" + ("_castw" if emit_w else ""),
    )(u, og, orr, *w_args)


def _out_body(h_ref, m_ref, wout_ref, o_ref, *wb_ref):
    wout = wout_ref[...].astype(BF16)
    if wb_ref:
        wb_ref[0][...] = wout
    o_ref[...] = h_ref[...] + _dot(m_ref[...], wout)


def _out_proj(h, merged, wout, *, tm=512):
    t, d = h.shape
    tm = min(tm, t)
    emit_w = wout.dtype == F32
    assert not emit_w or t == tm, "weights are emitted by a single-token-tile call"
    tn = 512 if emit_w else d
    out_shape = [jax.ShapeDtypeStruct((t, d), F32)] + ([jax.ShapeDtypeStruct((d, d), BF16)] if emit_w else [])
    out_specs = [pl.BlockSpec((tm, tn), lambda i, j: (i, j))] + ([pl.BlockSpec((d, tn), lambda i, j: (0, j))] if emit_w else [])
    return pl.pallas_call(
        _out_body,
        grid=(t // tm, d // tn),
        in_specs=[pl.BlockSpec((tm, tn), lambda i, j: (i, j)),
                  pl.BlockSpec((tm, d), lambda i, j: (i, 0)),
                  pl.BlockSpec((d, tn), lambda i, j: (0, j))],
        out_specs=out_specs,
        out_shape=out_shape,
        compiler_params=_cparams("parallel", "parallel"),
        name="out_proj" + ("_castw" if emit_w else ""),
    )(h, merged, wout)


def _gla_body(u_ref, wq_ref, wk_ref, wv_ref, wg_ref, la_ref, ng_ref,
              dq_ref, dk_ref, dv_ref, dg_ref, dla_ref, ds_ref,
              og_ref, st_ref, dog_ref, dso_ref, s_scr, *, ts, c, nt, bt):
    s = pl.program_id(2)

    @pl.when(s == 0)
    def _():
        s_scr[...] = jnp.zeros_like(s_scr)

    dec = _gla_decode_prep(dq_ref, dk_ref, dv_ref, dla_ref, ds_ref, dso_ref, nt=nt, bt=bt)
    dec.units(0, bt)
    dec.finish(dg_ref, ng_ref, dog_ref)

    u = u_ref[...]
    q_all = _dot(u, wq_ref[...]) * (GLA_DK ** -0.5)
    k_all = _dot(u, wk_ref[...])
    v_all = _dot(u, wv_ref[...]).astype(BF16)
    g_all = _dot(u, wg_ref[...])
    la_all = la_ref[...]
    la_hi = la_all.astype(BF16)
    la_lo = (la_all - la_hi.astype(F32)).astype(BF16)

    rr = lax.broadcasted_iota(jnp.int32, (c, c), 0)
    cc = lax.broadcasted_iota(jnp.int32, (c, c), 1)
    causal = rr >= cc
    tril = causal.astype(BF16)
    mid = c // 2
    rows = [slice(ci * c, (ci + 1) * c) for ci in range(ts // c)]

    b = [_dot(tril, la_hi[r]) + _dot(tril, la_lo[r]) for r in rows]
    q_rel, k_rel, q_dec, k_end, decay = [], [], [], [], []
    for r, bc in zip(rows, b):
        q, k = q_all[r], k_all[r]
        b_mid, b_last = bc[mid:mid + 1, :], bc[c - 1:c, :]
        q_rel.append((q * jnp.exp(bc - b_mid)).astype(BF16))
        k_rel.append((k * jnp.exp(b_mid - bc)).astype(BF16))
        q_dec.append((q * jnp.exp(bc)).astype(BF16))
        k_end.append((k * jnp.exp(b_last - bc)).astype(BF16))
        dcol = _lane_replicated_column(jnp.exp(b_last))
        decay.append(jnp.concatenate([dcol] * (GLA_DV // LANES), axis=1))
    att = [jnp.where(causal, _dot_tb(qr, kr), 0.0).astype(BF16) for qr, kr in zip(q_rel, k_rel)]
    o_intra = [_dot(a, v_all[r]) for a, r in zip(att, rows)]
    upd = [_dot_ta(ke, v_all[r]) for ke, r in zip(k_end, rows)]
    st = s_scr[...]
    for ci, r in enumerate(rows):
        o = o_intra[ci] + _dot(q_dec[ci], st.astype(BF16))
        og_ref[r, :] = (_rms(o, ng_ref[...]) * _silu(g_all[r])).astype(BF16)
        st = decay[ci] * st + upd[ci]
    s_scr[...] = st

    @pl.when(s == pl.num_programs(2) - 1)
    def _():
        st_ref[...] = st


def _decode_tile(batch, ns, nb):
    bt = nb // (batch * ns)
    assert bt * batch * ns == nb and bt % SUBLANES == 0
    return bt


def _gla_mixer(u, la_p, batch, seq, w_a, ng, pg, la, state):
    ts, c = SEQ_TILE, GLA_CHUNK
    ns = seq // ts
    d = u.shape[1]
    nt, nb, _ = pg.shape
    bt = _decode_tile(batch, ns, nb)
    kq, kk, kv, kg = WA_Q // GLA_DK, WA_K // GLA_DK, WA_V // GLA_DV, WA_G // GLA_DV
    dec = lambda width, col0: pl.BlockSpec((nt, bt, width), lambda b, h, s: (0, b * ns + s, col0 + h))
    st_spec = pl.BlockSpec((bt, None, GLA_DK, GLA_DV), lambda b, h, s: (b * ns + s, h, 0, 0))
    return pl.pallas_call(
        functools.partial(_gla_body, ts=ts, c=c, nt=nt, bt=bt),
        grid=(batch, GLA_HEADS, ns),
        in_specs=[pl.BlockSpec((ts, d), lambda b, h, s: (b * ns + s, 0)),
                  pl.BlockSpec((d, GLA_DK), lambda b, h, s: (0, kq + h)),
                  pl.BlockSpec((d, GLA_DK), lambda b, h, s: (0, kk + h)),
                  pl.BlockSpec((d, GLA_DV), lambda b, h, s: (0, kv + h)),
                  pl.BlockSpec((d, GLA_DV), lambda b, h, s: (0, kg + h)),
                  pl.BlockSpec((ts, GLA_DK), lambda b, h, s: (b * ns + s, h)),
                  pl.BlockSpec((None, 1, GLA_DV), lambda b, h, s: (h, 0, 0)),
                  dec(GLA_DK, kq), dec(GLA_DK, kk), dec(GLA_DV, kv), dec(GLA_DV, kg), dec(GLA_DK, 0), st_spec],
        out_specs=(pl.BlockSpec((ts, GLA_DV), lambda b, h, s: (b * ns + s, h)),
                   pl.BlockSpec((None, None, GLA_DK, GLA_DV), lambda b, h, s: (b, h, 0, 0)),
                   dec(GLA_DV, 0), st_spec),
        out_shape=(jax.ShapeDtypeStruct((batch * seq, GLA_V), BF16),
                   jax.ShapeDtypeStruct((batch, GLA_HEADS, GLA_DK, GLA_DV), F32),
                   jax.ShapeDtypeStruct((nt, nb, GLA_V), F32),
                   jax.ShapeDtypeStruct(state.shape, F32)),
        scratch_shapes=[pltpu.VMEM((GLA_DK, GLA_DV), F32)],
        compiler_params=_cparams("parallel", "parallel", "arbitrary"),
        name="gla_mixer",
    )(u, w_a, w_a, w_a, w_a, la_p, ng, pg, pg, pg, pg, la, state)


def _rope(x, cos, sin):
    half = x.shape[-1] // 2
    x1, x2 = x[:, :half], x[:, half:]
    return jnp.concatenate([x1 * cos - x2 * sin, x2 * cos + x1 * sin], axis=-1)


def _ret_body(u_ref, wq_ref, wk_ref, wv_ref, wg_ref, cos_ref, sin_ref, dm_ref, qd_ref, kd_ref, cd_ref, ng_ref,
              pw_ref, dq_ref, dk_ref, dv_ref, dg_ref, dcos_ref, dsin_ref, ds_ref,
              or_ref, st_ref, dor_ref, dso_ref, s_scr, *, ts, c, nt, bt):
    s = pl.program_id(2)

    @pl.when(s == 0)
    def _():
        s_scr[...] = jnp.zeros_like(s_scr)

    dec = _ret_decode_prep(pw_ref, dq_ref, dk_ref, dv_ref, dcos_ref, dsin_ref, ds_ref, dso_ref, nt=nt, bt=bt)
    cut = [bt * n // 4 for n in range(5)]

    u = u_ref[...]
    tok = pl.ds(pl.multiple_of(s * ts, ts), ts)
    cos, sin = cos_ref[tok, :], sin_ref[tok, :]
    q_all = _rope(_dot(u, wq_ref[...]), cos, sin)
    dec.units(cut[0], cut[1])
    k_all = _rope(_dot(u, wk_ref[...]), cos, sin) * (RET_DK ** -0.5)
    dec.units(cut[1], cut[2])
    v_all = _dot(u, wv_ref[...]).astype(BF16)
    dec.units(cut[2], cut[3])
    g_all = _dot(u, wg_ref[...])
    dec.units(cut[3], cut[4])
    dec.finish(dg_ref, ng_ref, dor_ref)

    rows = [slice(ci * c, (ci + 1) * c) for ci in range(ts // c)]
    att = [(_dot_tb(q_all[r].astype(BF16), k_all[r].astype(BF16)) * dm_ref[...]).astype(BF16) for r in rows]
    o_intra = [_dot(a, v_all[r]) for a, r in zip(att, rows)]
    upd = [_dot_ta((k_all[r] * kd_ref[...]).astype(BF16), v_all[r]) for r in rows]
    q_dec = [(q_all[r] * qd_ref[...]).astype(BF16) for r in rows]
    st = s_scr[...]
    for ci, r in enumerate(rows):
        o = o_intra[ci] + _dot(q_dec[ci], st.astype(BF16))
        or_ref[r, :] = (_rms(o, ng_ref[...]) * _silu(g_all[r])).astype(BF16)
        st = cd_ref[...] * st + upd[ci]
    s_scr[...] = st

    @pl.when(s == pl.num_programs(2) - 1)
    def _():
        st_ref[...] = st


def _ret_log_gamma():
    return jnp.log1p(-jnp.exp2(-5.0 - jnp.arange(RET_HEADS, dtype=F32)))


def _rope_tables(pos):
    half = RET_DK // 2
    freqs = ROPE_BASE ** (-jnp.arange(half, dtype=F32) / half)
    ang = pos[:, None] * freqs[None, :]
    return jnp.cos(ang), jnp.sin(ang)


def _ret_mixer(u, batch, seq, w_b, ng, pr, state):
    ts, c = SEQ_TILE, RET_CHUNK
    ns = seq // ts
    d = u.shape[1]
    nt, nb, _ = pr.shape
    bt = _decode_tile(batch, ns, nb)
    kq, kk, kv, kg = WB_Q // RET_DK, WB_K // RET_DK, WB_V // RET_DV, WB_G // RET_DV
    cos, sin = _rope_tables(jnp.arange(seq, dtype=F32))
    dcos, dsin = _rope_tables(PAST_LEN + jnp.arange(nt, dtype=F32))
    pw = jnp.exp(_ret_log_gamma()[:, None] * jnp.arange(nt + 1, dtype=F32)[None, :])
    dec = lambda width, col0: pl.BlockSpec((nt, bt, width), lambda b, h, s: (0, b * ns + s, col0 + h))
    rope_spec = pl.BlockSpec((nt, RET_DK // 2), lambda b, h, s: (0, 0))
    st_spec = pl.BlockSpec((bt, None, RET_DK, RET_DV), lambda b, h, s: (b * ns + s, h, 0, 0))
    lg = _ret_log_gamma()
    idx = jnp.arange(c, dtype=F32)
    diff = idx[:, None] - idx[None, :]
    dmat = jnp.where(diff >= 0, jnp.exp(lg[:, None, None] * jnp.maximum(diff, 0.0)), 0.0)
    qdec = jnp.broadcast_to(jnp.exp(lg[:, None] * (idx + 1.0)[None, :])[:, :, None], (RET_HEADS, c, RET_DK))
    kdec = jnp.broadcast_to(jnp.exp(lg[:, None] * (c - 1.0 - idx)[None, :])[:, :, None], (RET_HEADS, c, RET_DK))
    cdec = jnp.broadcast_to(jnp.exp(lg * c)[:, None, None], (RET_HEADS, 1, RET_DV))
    return pl.pallas_call(
        functools.partial(_ret_body, ts=ts, c=c, nt=nt, bt=bt),
        grid=(batch, RET_HEADS, ns),
        in_specs=[pl.BlockSpec((ts, d), lambda b, h, s: (b * ns + s, 0)),
                  pl.BlockSpec((d, RET_DK), lambda b, h, s: (0, kq + h)),
                  pl.BlockSpec((d, RET_DK), lambda b, h, s: (0, kk + h)),
                  pl.BlockSpec((d, RET_DV), lambda b, h, s: (0, kv + h)),
                  pl.BlockSpec((d, RET_DV), lambda b, h, s: (0, kg + h)),
                  pl.BlockSpec((seq, RET_DK // 2), lambda b, h, s: (0, 0)),
                  pl.BlockSpec((seq, RET_DK // 2), lambda b, h, s: (0, 0)),
                  pl.BlockSpec((None, c, c), lambda b, h, s: (h, 0, 0)),
                  pl.BlockSpec((None, c, RET_DK), lambda b, h, s: (h, 0, 0)),
                  pl.BlockSpec((None, c, RET_DK), lambda b, h, s: (h, 0, 0)),
                  pl.BlockSpec((None, 1, RET_DV), lambda b, h, s: (h, 0, 0)),
                  pl.BlockSpec((None, 1, RET_DV), lambda b, h, s: (h, 0, 0)),
                  pl.BlockSpec(memory_space=pltpu.SMEM),
                  dec(RET_DK, kq), dec(RET_DK, kk), dec(RET_DV, kv), dec(RET_DV, kg), rope_spec, rope_spec, st_spec],
        out_specs=(pl.BlockSpec((ts, RET_DV), lambda b, h, s: (b * ns + s, h)),
                   pl.BlockSpec((None, None, RET_DK, RET_DV), lambda b, h, s: (b, h, 0, 0)),
                   dec(RET_DV, 0), st_spec),
        out_shape=(jax.ShapeDtypeStruct((batch * seq, RET_V), BF16),
                   jax.ShapeDtypeStruct((batch, RET_HEADS, RET_DK, RET_DV), F32),
                   jax.ShapeDtypeStruct((nt, nb, RET_V), F32),
                   jax.ShapeDtypeStruct(state.shape, F32)),
        scratch_shapes=[pltpu.VMEM((RET_DK, RET_DV), F32)],
        compiler_params=_cparams("parallel", "parallel", "arbitrary"),
        name="ret_mixer",
    )(u, w_b, w_b, w_b, w_b, cos, sin, dmat, qdec, kdec, cdec, ng, pw, pr, pr, pr, pr, dcos, dsin, state)


def _proj_body(u_ref, *refs):
    *w_refs, o_ref, wb_ref = refs
    w = _w_cols(w_refs[0], w_refs[1] if len(w_refs) > 1 else None)
    wb_ref[...] = w
    o_ref[...] = _dot(u_ref[...], w)


def _proj(u, w_t, start, n, *, tn=1024):
    t, d = u.shape
    w_specs = _w_cols_specs(d, tn, start, lambda j: j)
    return pl.pallas_call(
        _proj_body,
        grid=(n // tn,),
        in_specs=[pl.BlockSpec((t, d), lambda j: (0, 0))] + w_specs,
        out_specs=[pl.BlockSpec((t, tn), lambda j: (0, j)), pl.BlockSpec((d, tn), lambda j: (0, j))],
        out_shape=[jax.ShapeDtypeStruct((t, n), F32), jax.ShapeDtypeStruct((d, n), BF16)],
        compiler_params=_cparams("parallel"),
        name="sample_proj_castw",
    )(u, *([w_t] * len(w_specs)))


def _gate_body(u_ref, wga_ref, wgu_ref, bg_ref, o_ref):
    ga = _dot(u_ref[...], wga_ref[...])
    logit = _dot(ga.astype(BF16), wgu_ref[...]) + bg_ref[...]
    o_ref[...] = _log_sigmoid(logit) / GLA_GATE_TAU


def _gate(u, wga, wgu, bg, *, tm=1024):
    t, d = u.shape
    tm = min(tm, t)
    return pl.pallas_call(
        _gate_body,
        grid=(t // tm,),
        in_specs=[pl.BlockSpec((tm, d), lambda i: (i, 0)),
                  pl.BlockSpec(wga.shape, lambda i: (0, 0)),
                  pl.BlockSpec(wgu.shape, lambda i: (0, 0)),
                  pl.BlockSpec(bg.shape, lambda i: (0, 0))],
        out_specs=pl.BlockSpec((tm, GLA_QK), lambda i: (i, 0)),
        out_shape=jax.ShapeDtypeStruct((t, GLA_QK), F32),
        compiler_params=_cparams("parallel"),
        name="log_gate",
    )(u, wga, wgu, bg)


def _row_group_mask(rows, cols, bt, i):
    return (lax.broadcasted_iota(jnp.int32, (rows, cols), 0) % bt) == i


class _Decode:
    def __init__(self, intra, q_dec, k_end, v_all, new_state, s_ref, so_ref, nt, bt):
        self.intra, self.q_dec, self.k_end, self.v_all = intra, q_dec, k_end, v_all
        self.new_state, self.s_ref, self.so_ref, self.nt, self.bt = new_state, s_ref, so_ref, nt, bt
        self.inter = jnp.zeros((nt * bt, v_all.shape[1]), F32)

    def units(self, lo, hi):
        rows, bt = self.nt * self.bt, self.bt
        for i in range(lo, hi):
            st = self.s_ref[i]
            qs = _dot(self.q_dec, st.astype(BF16))
            self.inter = self.inter + jnp.where(_row_group_mask(rows, qs.shape[1], bt, i), qs, 0.0)
            k_i = jnp.where(_row_group_mask(rows, self.k_end.shape[1], bt, i), self.k_end, 0.0).astype(BF16)
            self.so_ref[i] = self.new_state(i, st, _dot_ta(k_i, self.v_all))

    def finish(self, g_ref, ng_ref, o_ref):
        bt = self.bt
        for t in range(self.nt):
            o = self.intra[t] + self.inter[t * bt:(t + 1) * bt, :]
            o_ref[t] = _rms(o, ng_ref[...]) * _silu(g_ref[t])


def _gla_decode_prep(q_ref, k_ref, v_ref, la_ref, s_ref, so_ref, *, nt, bt):
    q = [q_ref[t] * (GLA_DK ** -0.5) for t in range(nt)]
    k = [k_ref[t] for t in range(nt)]
    v = [v_ref[t] for t in range(nt)]
    b = [la_ref[0]]
    for t in range(1, nt):
        b.append(b[-1] + la_ref[t])
    b_mid, b_last = b[nt // 2], b[nt - 1]
    q_rel = [q[t] * jnp.exp(b[t] - b_mid) for t in range(nt)]
    k_rel = [k[t] * jnp.exp(b_mid - b[t]) for t in range(nt)]
    intra = []
    for t in range(nt):
        acc = None
        for s in range(t + 1):
            term = jnp.sum(q_rel[t] * k_rel[s], axis=-1, keepdims=True) * v[s]
            acc = term if acc is None else acc + term
        intra.append(acc)
    q_dec = jnp.concatenate([q[t] * jnp.exp(b[t]) for t in range(nt)], axis=0).astype(BF16)
    k_end = jnp.concatenate([k[t] * jnp.exp(b_last - b[t]) for t in range(nt)], axis=0)
    v_all = jnp.concatenate(v, axis=0).astype(BF16)
    decay = jnp.exp(b_last)

    def new_state(i, st, upd):
        dcol = _lane_replicated_column(decay[i:i + 1, :])
        return jnp.concatenate([dcol] * (GLA_DV // LANES), axis=1) * st + upd

    return _Decode(intra, q_dec, k_end, v_all, new_state, s_ref, so_ref, nt, bt)


def _ret_decode_prep(pw_ref, q_ref, k_ref, v_ref, cos_ref, sin_ref, s_ref, so_ref, *, nt, bt):
    h = pl.program_id(1)
    pw = [pw_ref[h, n] for n in range(nt + 1)]
    q = [_rope(q_ref[t], cos_ref[t:t + 1, :], sin_ref[t:t + 1, :]) for t in range(nt)]
    k = [_rope(k_ref[t], cos_ref[t:t + 1, :], sin_ref[t:t + 1, :]) * (RET_DK ** -0.5) for t in range(nt)]
    v = [v_ref[t] for t in range(nt)]
    intra = []
    for t in range(nt):
        acc = None
        for s in range(t + 1):
            term = (jnp.sum(q[t] * k[s], axis=-1, keepdims=True) * pw[t - s]) * v[s]
            acc = term if acc is None else acc + term
        intra.append(acc)
    q_dec = jnp.concatenate([q[t] * pw[t + 1] for t in range(nt)], axis=0).astype(BF16)
    k_end = jnp.concatenate([k[t] * pw[nt - 1 - t] for t in range(nt)], axis=0)
    v_all = jnp.concatenate(v, axis=0).astype(BF16)
    return _Decode(intra, q_dec, k_end, v_all, lambda i, st, upd: pw[nt] * st + upd, s_ref, so_ref, nt, bt)


def kernel(x_prompt, x_sample, state_gla, state_ret, ffn1_norm, ffn1_w1, ffn1_w3, ffn1_w2, mix_norm, w_in, w_gate_up, b_gate, gla_norm, w_gla_o, ret_norm, w_ret_o, w_out, ffn2_norm, ffn2_w1, ffn2_w3, ffn2_w2, final_norm):
    depth = w_in.shape[0]
    batch, seq, d = x_prompt.shape
    nb, nt, _ = x_sample.shape

    hp = x_prompt.reshape(batch * seq, d)
    hs = x_sample.transpose(1, 0, 2).reshape(nt * nb, d)
    gla_p, ret_p, gla_s, ret_s = [], [], [], []
    for l in range(depth):
        last = l == depth - 1
        wl = w_in[l].T
        w_ga = w_in[l][:, WA_WIDTH:WB_START].astype(BF16)
        wgu = w_gate_up[l].astype(BF16)
        bg = b_gate[l].reshape(1, GLA_QK)
        ng = gla_norm[l].reshape(GLA_HEADS, 1, GLA_DV)
        nr = ret_norm[l].reshape(RET_HEADS, 1, RET_DV)
        g_next = final_norm if last else ffn1_norm[l + 1]

        hs, us, *f1 = _ffn(hs, ffn1_norm[l], ffn1_w1[l], ffn1_w3[l], ffn1_w2[l], mix_norm[l], final=False)
        hp, up = _ffn(hp, ffn1_norm[l], *f1, mix_norm[l], final=False)

        pg, w_a = _proj(us, wl, 0, WA_WIDTH)
        pr, w_b = _proj(us, wl, WB_START, WB_ZA)
        la = _gate(us, w_ga, wgu, bg)
        la_p = _gate(up, w_ga, wgu, bg)
        og, sg, og_s, sg_s = _gla_mixer(up, la_p, batch, seq, w_a, ng,
                                        pg.reshape(nt, nb, -1), la.reshape(nt, nb, -1), state_gla[l])
        orr, sr, or_s, sr_s = _ret_mixer(up, batch, seq, w_b, nr, pr.reshape(nt, nb, -1), state_ret[l])
        gla_s.append(sg_s)
        ret_s.append(sr_s)
        gla_p.append(sg)
        ret_p.append(sr)

        ms, wza, wzb, wgo, wro = _merge_gate(us, og_s.reshape(nt * nb, GLA_V), or_s.reshape(nt * nb, RET_V),
                                             wl, w_gla_o[l], w_ret_o[l])
        mp, = _merge_gate(up, og, orr, (wza, wzb), wgo, wro)
        hs, wout = _out_proj(hs, ms, w_out[l])
        hp, = _out_proj(hp, mp, wout)

        hs, *f2 = _ffn(hs, ffn2_norm[l], ffn2_w1[l], ffn2_w3[l], ffn2_w2[l], g_next, final=last)
        hp, *_ = _ffn(hp, ffn2_norm[l], *f2[-3:], g_next, final=last)

    y_prompt = hp.reshape(batch, seq, d)
    y_sample = hs.reshape(nt, nb, d).transpose(1, 0, 2)
    return (y_prompt, y_sample, jnp.stack(gla_p), jnp.stack(ret_p), jnp.stack(gla_s), jnp.stack(ret_s))
```

```python
import functools

import jax
import jax.numpy as jnp
from jax import lax
from jax.experimental import pallas as pl
from jax.experimental.pallas import tpu as pltpu

F32, BF16 = jnp.float32, jnp.bfloat16

D_MODEL = 2048
PAST_LEN = 16384
GLA_HEADS = 4
GLA_DK = D_MODEL // (2 * GLA_HEADS)
GLA_DV = D_MODEL // GLA_HEADS
GLA_GATE_RANK = 16
GLA_GATE_TAU = 16.0
RET_HEADS = 8
RET_DK = D_MODEL // RET_HEADS
RET_DV = 2 * D_MODEL // RET_HEADS
ROPE_BASE = 10000.0
EPS = 1e-6
GLA_QK = GLA_HEADS * GLA_DK
GLA_V = GLA_HEADS * GLA_DV
RET_QK = RET_HEADS * RET_DK
RET_V = RET_HEADS * RET_DV

WA_Q, WA_K, WA_V, WA_G = 0, GLA_QK, 2 * GLA_QK, 2 * GLA_QK + GLA_V
WA_WIDTH = 2 * GLA_QK + 2 * GLA_V
WB_Q, WB_K, WB_V, WB_G = 0, RET_QK, 2 * RET_QK, 2 * RET_QK + RET_V
WB_ZA = 2 * RET_QK + 2 * RET_V
WB_ZB = WB_ZA + D_MODEL
WB_START = WA_WIDTH + GLA_GATE_RANK

LANES = 128
SUBLANES = 8
MXU_WIDTH = 256
VMEM_LIMIT = 58 * 2**20

GLA_CHUNK = 64
RET_CHUNK = 256
SEQ_TILE = 512


def _cparams(*sem):
    return pltpu.CompilerParams(dimension_semantics=sem, vmem_limit_bytes=VMEM_LIMIT)


def _dot(a, b):
    return jnp.dot(a, b, preferred_element_type=F32)


def _dot_tb(a, b):
    return lax.dot_general(a, b, (((1,), (1,)), ((), ())), preferred_element_type=F32)


def _dot_ta(a, b):
    return lax.dot_general(a, b, (((0,), (0,)), ((), ())), preferred_element_type=F32)


def _rms(x, g):
    return x * lax.rsqrt(jnp.mean(x * x, axis=-1, keepdims=True) + EPS) * g


def _sigmoid(x):
    return 1.0 / (1.0 + jnp.exp(-x))


def _silu(x):
    return x * _sigmoid(x)


def _log_sigmoid(x):
    return jnp.minimum(x, 0.0) - jnp.log(1.0 + jnp.exp(-jnp.abs(x)))


def _lane_replicated_column(row):
    return jnp.broadcast_to(row, (LANES, row.shape[-1])).T


def _ffn_body(x_ref, g_ref, w1_ref, w3_ref, w2_ref, g2_ref, *refs, tm, rows, final, emit_w):
    refs = list(refs)
    u_scr = refs.pop()
    wb_refs = [refs.pop() for _ in range(3)][::-1] if emit_w else None
    if final:
        y_ref, = refs
        acc_ref = y_ref
    else:
        h_ref, un_ref = refs
        acc_ref = h_ref
    j = pl.program_id(1)

    @pl.when(j == 0)
    def _():
        for r in range(0, tm, rows):
            u_scr[r:r + rows, :] = _rms(x_ref[r:r + rows, :], g_ref[...]).astype(BF16)
            acc_ref[r:r + rows, :] = jnp.zeros((rows, acc_ref.shape[1]), F32)

    w1, w3, w2 = w1_ref[...].astype(BF16), w3_ref[...].astype(BF16), w2_ref[...].astype(BF16)
    if emit_w:
        for ref, w in zip(wb_refs, (w1, w3, w2)):
            ref[...] = w
    u = u_scr[...]
    part = None
    for c0 in range(0, w1.shape[1], MXU_WIDTH):
        cols = slice(c0, c0 + MXU_WIDTH)
        a = _dot(u, w1[:, cols])
        b = _dot(u, w3[:, cols])
        p = _dot((_silu(a) * b).astype(BF16), w2[cols, :])
        part = p if part is None else part + p
    acc_ref[...] += part

    @pl.when(j == pl.num_programs(1) - 1)
    def _():
        for r in range(0, tm, rows):
            h = x_ref[r:r + rows, :] + 0.5 * acc_ref[r:r + rows, :]
            if final:
                y_ref[r:r + rows, :] = _rms(h, g2_ref[...])
            else:
                h_ref[r:r + rows, :] = h
                un_ref[r:r + rows, :] = _rms(h, g2_ref[...]).astype(BF16)


def _ffn(x, g, w1, w3, w2, g2, *, final, tm=512):
    t, d = x.shape
    dff = w1.shape[1]
    tm = min(tm, t)
    emit_w = w1.dtype == F32
    assert not emit_w or t == tm, "weights are emitted by a single-token-tile call"
    tf = 256 if emit_w else 512
    grid = (t // tm, dff // tf)
    row = pl.BlockSpec((tm, d), lambda i, j: (i, 0))
    vec = pl.BlockSpec((1, d), lambda i, j: (0, 0))
    w_up = pl.BlockSpec((d, tf), lambda i, j: (0, j))
    w_dn = pl.BlockSpec((tf, d), lambda i, j: (j, 0))
    out_shape = [jax.ShapeDtypeStruct((t, d), F32)] + ([] if final else [jax.ShapeDtypeStruct((t, d), BF16)])
    out_specs = [row] * len(out_shape)
    if emit_w:
        out_shape += [jax.ShapeDtypeStruct(w.shape, BF16) for w in (w1, w3, w2)]
        out_specs += [w_up, w_up, w_dn]
    return pl.pallas_call(
        functools.partial(_ffn_body, tm=tm, rows=min(256, tm), final=final, emit_w=emit_w),
        grid=grid,
        in_specs=[row, vec, w_up, w_up, w_dn, vec],
        out_specs=out_specs,
        out_shape=out_shape,
        scratch_shapes=[pltpu.VMEM((tm, d), BF16)],
        compiler_params=_cparams("parallel", "arbitrary"),
        name=("ffn_final" if final else "ffn_mid") + ("_castw" if emit_w else ""),
    )(x, g.reshape(1, d), w1, w3, w2, g2.reshape(1, d))


def _w_cols(a_ref, b_ref):
    rows = a_ref[...] if b_ref is None else jnp.concatenate([a_ref[b_ref.shape[0]:, :], b_ref[...]], axis=0)
    return rows.T.astype(BF16)


def _w_cols_specs(width, tn, start, col_of):
    shift = start % tn
    base = start - shift
    if shift == 0:
        return [pl.BlockSpec((tn, width), lambda *g: (base // tn + col_of(*g), 0))]
    assert shift % SUBLANES == 0 and tn % shift == 0 and base % shift == 0
    return [pl.BlockSpec((tn, width), lambda *g: (base // tn + col_of(*g), 0)),
            pl.BlockSpec((shift, width), lambda *g: ((base + tn * (col_of(*g) + 1)) // shift, 0))]


def _merge_body(u_ref, og_ref, or_ref, *refs, emit_w):
    u = u_ref[...]
    if not emit_w:
        wza_ref, wzb_ref, wgo_ref, wro_ref, m_ref = refs
        wza, wzb, wgo, wro = wza_ref[...], wzb_ref[...], wgo_ref[...], wro_ref[...]
    else:
        za_a, za_b, zb_a, zb_b, wgo_ref, wro_ref, m_ref, wza_o, wzb_o, wgo_o, wro_o = refs
        wza = _w_cols(za_a, za_b)
        wzb = _w_cols(zb_a, zb_b)
        wgo, wro = wgo_ref[...].astype(BF16), wro_ref[...].astype(BF16)
        wza_o[...], wzb_o[...], wgo_o[...], wro_o[...] = wza, wzb, wgo, wro
    za = _dot(u, wza)
    zb = _dot(u, wzb)
    branch_a = _dot(og_ref[...].astype(BF16), wgo)
    branch_b = _dot(or_ref[...].astype(BF16), wro)
    m_ref[...] = (_sigmoid(za) * branch_a + _sigmoid(zb) * branch_b).astype(BF16)


def _merge_gate(u, og, orr, wz, wgo, wro, *, tm=512):
    t, d = u.shape
    tm = min(tm, t)
    emit_w = wgo.dtype == F32
    assert not emit_w or t == tm, "weights are emitted by a single-token-tile call"
    tn = 256 if emit_w else 512
    col = lambda i, j: j
    acts = [pl.BlockSpec((tm, d), lambda i, j: (i, 0)),
            pl.BlockSpec((tm, GLA_V), lambda i, j: (i, 0)),
            pl.BlockSpec((tm, RET_V), lambda i, j: (i, 0))]
    w_col = lambda rows: pl.BlockSpec((rows, tn), lambda i, j: (0, j))
    out_shape = [jax.ShapeDtypeStruct((t, d), BF16)]
    out_specs = [pl.BlockSpec((tm, tn), lambda i, j: (i, j))]
    if emit_w:
        w_specs = (_w_cols_specs(d, tn, WB_START + WB_ZA, col) + _w_cols_specs(d, tn, WB_START + WB_ZB, col)
                   + [w_col(GLA_V), w_col(RET_V)])
        w_args = (wz, wz, wz, wz, wgo, wro)
        out_shape += [jax.ShapeDtypeStruct((d, d), BF16), jax.ShapeDtypeStruct((d, d), BF16),
                      jax.ShapeDtypeStruct(wgo.shape, BF16), jax.ShapeDtypeStruct(wro.shape, BF16)]
        out_specs += [w_col(d), w_col(d), w_col(GLA_V), w_col(RET_V)]
    else:
        w_specs, w_args = [w_col(d), w_col(d), w_col(GLA_V), w_col(RET_V)], (*wz, wgo, wro)
    return pl.pallas_call(
        functools.partial(_merge_body, emit_w=emit_w),
        grid=(t // tm, d // tn),
        in_specs=acts + w_specs,
        out_specs=out_specs,
        out_shape=out_shape,
        compiler_params=_cparams("parallel", "parallel"),
        name="merge_gate" + ("_castw" if emit_w else ""),
    )(u, og, orr, *w_args)


def _out_body(h_ref, m_ref, wout_ref, o_ref, *wb_ref):
    wout = wout_ref[...].astype(BF16)
    if wb_ref:
        wb_ref[0][...] = wout
    o_ref[...] = h_ref[...] + _dot(m_ref[...], wout)


def _out_proj(h, merged, wout, *, tm=512):
    t, d = h.shape
    tm = min(tm, t)
    emit_w = wout.dtype == F32
    assert not emit_w or t == tm, "weights are emitted by a single-token-tile call"
    tn = 512 if emit_w else d
    out_shape = [jax.ShapeDtypeStruct((t, d), F32)] + ([jax.ShapeDtypeStruct((d, d), BF16)] if emit_w else [])
    out_specs = [pl.BlockSpec((tm, tn), lambda i, j: (i, j))] + ([pl.BlockSpec((d, tn), lambda i, j: (0, j))] if emit_w else [])
    return pl.pallas_call(
        _out_body,
        grid=(t // tm, d // tn),
        in_specs=[pl.BlockSpec((tm, tn), lambda i, j: (i, j)),
                  pl.BlockSpec((tm, d), lambda i, j: (i, 0)),
                  pl.BlockSpec((d, tn), lambda i, j: (0, j))],
        out_specs=out_specs,
        out_shape=out_shape,
        compiler_params=_cparams("parallel", "parallel"),
        name="out_proj" + ("_castw" if emit_w else ""),
    )(h, merged, wout)


def _gla_body(u_ref, wq_ref, wk_ref, wv_ref, wg_ref, la_ref, ng_ref,
              dq_ref, dk_ref, dv_ref, dg_ref, dla_ref, ds_ref,
              og_ref, st_ref, dog_ref, dso_ref, s_scr, *, ts, c, nt, bt):
    s = pl.program_id(2)

    @pl.when(s == 0)
    def _():
        s_scr[...] = jnp.zeros_like(s_scr)

    u = u_ref[...]
    q_all = _dot(u, wq_ref[...]) * (GLA_DK ** -0.5)
    k_all = _dot(u, wk_ref[...])
    v_all = _dot(u, wv_ref[...]).astype(BF16)
    g_all = _dot(u, wg_ref[...])
    la_all = la_ref[...]
    la_hi = la_all.astype(BF16)
    la_lo = (la_all - la_hi.astype(F32)).astype(BF16)

    rr = lax.broadcasted_iota(jnp.int32, (c, c), 0)
    cc = lax.broadcasted_iota(jnp.int32, (c, c), 1)
    causal = rr >= cc
    tril = causal.astype(BF16)
    mid = c // 2
    rows = [slice(ci * c, (ci + 1) * c) for ci in range(ts // c)]

    b = [_dot(tril, la_hi[r]) + _dot(tril, la_lo[r]) for r in rows]
    q_rel, k_rel, q_dec, k_end, decay = [], [], [], [], []
    for r, bc in zip(rows, b):
        q, k = q_all[r], k_all[r]
        b_mid, b_last = bc[mid:mid + 1, :], bc[c - 1:c, :]
        q_rel.append((q * jnp.exp(bc - b_mid)).astype(BF16))
        k_rel.append((k * jnp.exp(b_mid - bc)).astype(BF16))
        q_dec.append((q * jnp.exp(bc)).astype(BF16))
        k_end.append((k * jnp.exp(b_last - bc)).astype(BF16))
        dcol = _lane_replicated_column(jnp.exp(b_last))
        decay.append(jnp.concatenate([dcol] * (GLA_DV // LANES), axis=1))
    att = [jnp.where(causal, _dot_tb(qr, kr), 0.0).astype(BF16) for qr, kr in zip(q_rel, k_rel)]
    o_intra = [_dot(a, v_all[r]) for a, r in zip(att, rows)]
    upd = [_dot_ta(ke, v_all[r]) for ke, r in zip(k_end, rows)]

    dec = _gla_decode_prep(dq_ref, dk_ref, dv_ref, dla_ref, ds_ref, dso_ref, nt=nt, bt=bt)
    dec.units(0, bt)
    dec.finish(dg_ref, ng_ref, dog_ref)

    st = s_scr[...]
    for ci, r in enumerate(rows):
        o = o_intra[ci] + _dot(q_dec[ci], st.astype(BF16))
        og_ref[r, :] = (_rms(o, ng_ref[...]) * _silu(g_all[r])).astype(BF16)
        st = decay[ci] * st + upd[ci]
    s_scr[...] = st

    @pl.when(s == pl.num_programs(2) - 1)
    def _():
        st_ref[...] = st


def _decode_tile(batch, ns, nb):
    bt = nb // (batch * ns)
    assert bt * batch * ns == nb and bt % SUBLANES == 0
    return bt


def _gla_mixer(u, la_p, batch, seq, w_a, ng, pg, la, state):
    ts, c = SEQ_TILE, GLA_CHUNK
    ns = seq // ts
    d = u.shape[1]
    nt, nb, _ = pg.shape
    bt = _decode_tile(batch, ns, nb)
    kq, kk, kv, kg = WA_Q // GLA_DK, WA_K // GLA_DK, WA_V // GLA_DV, WA_G // GLA_DV
    dec = lambda width, col0: pl.BlockSpec((nt, bt, width), lambda b, h, s: (0, b * ns + s, col0 + h))
    st_spec = pl.BlockSpec((bt, None, GLA_DK, GLA_DV), lambda b, h, s: (b * ns + s, h, 0, 0))
    return pl.pallas_call(
        functools.partial(_gla_body, ts=ts, c=c, nt=nt, bt=bt),
        grid=(batch, GLA_HEADS, ns),
        in_specs=[pl.BlockSpec((ts, d), lambda b, h, s: (b * ns + s, 0)),
                  pl.BlockSpec((d, GLA_DK), lambda b, h, s: (0, kq + h)),
                  pl.BlockSpec((d, GLA_DK), lambda b, h, s: (0, kk + h)),
                  pl.BlockSpec((d, GLA_DV), lambda b, h, s: (0, kv + h)),
                  pl.BlockSpec((d, GLA_DV), lambda b, h, s: (0, kg + h)),
                  pl.BlockSpec((ts, GLA_DK), lambda b, h, s: (b * ns + s, h)),
                  pl.BlockSpec((None, 1, GLA_DV), lambda b, h, s: (h, 0, 0)),
                  dec(GLA_DK, kq), dec(GLA_DK, kk), dec(GLA_DV, kv), dec(GLA_DV, kg), dec(GLA_DK, 0), st_spec],
        out_specs=(pl.BlockSpec((ts, GLA_DV), lambda b, h, s: (b * ns + s, h)),
                   pl.BlockSpec((None, None, GLA_DK, GLA_DV), lambda b, h, s: (b, h, 0, 0)),
                   dec(GLA_DV, 0), st_spec),
        out_shape=(jax.ShapeDtypeStruct((batch * seq, GLA_V), BF16),
                   jax.ShapeDtypeStruct((batch, GLA_HEADS, GLA_DK, GLA_DV), F32),
                   jax.ShapeDtypeStruct((nt, nb, GLA_V), F32),
                   jax.ShapeDtypeStruct(state.shape, F32)),
        scratch_shapes=[pltpu.VMEM((GLA_DK, GLA_DV), F32)],
        compiler_params=_cparams("parallel", "parallel", "arbitrary"),
        name="gla_mixer",
    )(u, w_a, w_a, w_a, w_a, la_p, ng, pg, pg, pg, pg, la, state)


def _rope(x, cos, sin):
    half = x.shape[-1] // 2
    x1, x2 = x[:, :half], x[:, half:]
    return jnp.concatenate([x1 * cos - x2 * sin, x2 * cos + x1 * sin], axis=-1)


def _ret_body(u_ref, wq_ref, wk_ref, wv_ref, wg_ref, cos_ref, sin_ref, dm_ref, qd_ref, kd_ref, cd_ref, ng_ref,
              pw_ref, dq_ref, dk_ref, dv_ref, dg_ref, dcos_ref, dsin_ref, ds_ref,
              or_ref, st_ref, dor_ref, dso_ref, s_scr, *, ts, c, nt, bt):
    s = pl.program_id(2)

    @pl.when(s == 0)
    def _():
        s_scr[...] = jnp.zeros_like(s_scr)

    dec = _ret_decode_prep(pw_ref, dq_ref, dk_ref, dv_ref, dcos_ref, dsin_ref, ds_ref, dso_ref, nt=nt, bt=bt)
    cut = [bt * n // 4 for n in range(5)]

    u = u_ref[...]
    tok = pl.ds(pl.multiple_of(s * ts, ts), ts)
    cos, sin = cos_ref[tok, :], sin_ref[tok, :]
    q_all = _rope(_dot(u, wq_ref[...]), cos, sin)
    dec.units(cut[0], cut[1])
    k_all = _rope(_dot(u, wk_ref[...]), cos, sin) * (RET_DK ** -0.5)
    dec.units(cut[1], cut[2])
    v_all = _dot(u, wv_ref[...]).astype(BF16)
    dec.units(cut[2], cut[3])
    g_all = _dot(u, wg_ref[...])
    dec.units(cut[3], cut[4])
    dec.finish(dg_ref, ng_ref, dor_ref)

    rows = [slice(ci * c, (ci + 1) * c) for ci in range(ts // c)]
    att = [(_dot_tb(q_all[r].astype(BF16), k_all[r].astype(BF16)) * dm_ref[...]).astype(BF16) for r in rows]
    o_intra = [_dot(a, v_all[r]) for a, r in zip(att, rows)]
    upd = [_dot_ta((k_all[r] * kd_ref[...]).astype(BF16), v_all[r]) for r in rows]
    q_dec = [(q_all[r] * qd_ref[...]).astype(BF16) for r in rows]
    st = s_scr[...]
    for ci, r in enumerate(rows):
        o = o_intra[ci] + _dot(q_dec[ci], st.astype(BF16))
        or_ref[r, :] = (_rms(o, ng_ref[...]) * _silu(g_all[r])).astype(BF16)
        st = cd_ref[...] * st + upd[ci]
    s_scr[...] = st

    @pl.when(s == pl.num_programs(2) - 1)
    def _():
        st_ref[...] = st


def _ret_log_gamma():
    return jnp.log1p(-jnp.exp2(-5.0 - jnp.arange(RET_HEADS, dtype=F32)))


def _rope_tables(pos):
    half = RET_DK // 2
    freqs = ROPE_BASE ** (-jnp.arange(half, dtype=F32) / half)
    ang = pos[:, None] * freqs[None, :]
    return jnp.cos(ang), jnp.sin(ang)


def _ret_mixer(u, batch, seq, w_b, ng, pr, state):
    ts, c = SEQ_TILE, RET_CHUNK
    ns = seq // ts
    d = u.shape[1]
    nt, nb, _ = pr.shape
    bt = _decode_tile(batch, ns, nb)
    kq, kk, kv, kg = WB_Q // RET_DK, WB_K // RET_DK, WB_V // RET_DV, WB_G // RET_DV
    cos, sin = _rope_tables(jnp.arange(seq, dtype=F32))
    dcos, dsin = _rope_tables(PAST_LEN + jnp.arange(nt, dtype=F32))
    pw = jnp.exp(_ret_log_gamma()[:, None] * jnp.arange(nt + 1, dtype=F32)[None, :])
    dec = lambda width, col0: pl.BlockSpec((nt, bt, width), lambda b, h, s: (0, b * ns + s, col0 + h))
    rope_spec = pl.BlockSpec((nt, RET_DK // 2), lambda b, h, s: (0, 0))
    st_spec = pl.BlockSpec((bt, None, RET_DK, RET_DV), lambda b, h, s: (b * ns + s, h, 0, 0))
    lg = _ret_log_gamma()
    idx = jnp.arange(c, dtype=F32)
    diff = idx[:, None] - idx[None, :]
    dmat = jnp.where(diff >= 0, jnp.exp(lg[:, None, None] * jnp.maximum(diff, 0.0)), 0.0)
    qdec = jnp.broadcast_to(jnp.exp(lg[:, None] * (idx + 1.0)[None, :])[:, :, None], (RET_HEADS, c, RET_DK))
    kdec = jnp.broadcast_to(jnp.exp(lg[:, None] * (c - 1.0 - idx)[None, :])[:, :, None], (RET_HEADS, c, RET_DK))
    cdec = jnp.broadcast_to(jnp.exp(lg * c)[:, None, None], (RET_HEADS, 1, RET_DV))
    return pl.pallas_call(
        functools.partial(_ret_body, ts=ts, c=c, nt=nt, bt=bt),
        grid=(batch, RET_HEADS, ns),
        in_specs=[pl.BlockSpec((ts, d), lambda b, h, s: (b * ns + s, 0)),
                  pl.BlockSpec((d, RET_DK), lambda b, h, s: (0, kq + h)),
                  pl.BlockSpec((d, RET_DK), lambda b, h, s: (0, kk + h)),
                  pl.BlockSpec((d, RET_DV), lambda b, h, s: (0, kv + h)),
                  pl.BlockSpec((d, RET_DV), lambda b, h, s: (0, kg + h)),
                  pl.BlockSpec((seq, RET_DK // 2), lambda b, h, s: (0, 0)),
                  pl.BlockSpec((seq, RET_DK // 2), lambda b, h, s: (0, 0)),
                  pl.BlockSpec((None, c, c), lambda b, h, s: (h, 0, 0)),
                  pl.BlockSpec((None, c, RET_DK), lambda b, h, s: (h, 0, 0)),
                  pl.BlockSpec((None, c, RET_DK), lambda b, h, s: (h, 0, 0)),
                  pl.BlockSpec((None, 1, RET_DV), lambda b, h, s: (h, 0, 0)),
                  pl.BlockSpec((None, 1, RET_DV), lambda b, h, s: (h, 0, 0)),
                  pl.BlockSpec(memory_space=pltpu.SMEM),
                  dec(RET_DK, kq), dec(RET_DK, kk), dec(RET_DV, kv), dec(RET_DV, kg), rope_spec, rope_spec, st_spec],
        out_specs=(pl.BlockSpec((ts, RET_DV), lambda b, h, s: (b * ns + s, h)),
                   pl.BlockSpec((None, None, RET_DK, RET_DV), lambda b, h, s: (b, h, 0, 0)),
                   dec(RET_DV, 0), st_spec),
        out_shape=(jax.ShapeDtypeStruct((batch * seq, RET_V), BF16),
                   jax.ShapeDtypeStruct((batch, RET_HEADS, RET_DK, RET_DV), F32),
                   jax.ShapeDtypeStruct((nt, nb, RET_V), F32),
                   jax.ShapeDtypeStruct(state.shape, F32)),
        scratch_shapes=[pltpu.VMEM((RET_DK, RET_DV), F32)],
        compiler_params=_cparams("parallel", "parallel", "arbitrary"),
        name="ret_mixer",
    )(u, w_b, w_b, w_b, w_b, cos, sin, dmat, qdec, kdec, cdec, ng, pw, pr, pr, pr, pr, dcos, dsin, state)


def _proj_body(u_ref, *refs):
    *w_refs, o_ref, wb_ref = refs
    w = _w_cols(w_refs[0], w_refs[1] if len(w_refs) > 1 else None)
    wb_ref[...] = w
    o_ref[...] = _dot(u_ref[...], w)


def _proj(u, w_t, start, n, *, tn=1024):
    t, d = u.shape
    w_specs = _w_cols_specs(d, tn, start, lambda j: j)
    return pl.pallas_call(
        _proj_body,
        grid=(n // tn,),
        in_specs=[pl.BlockSpec((t, d), lambda j: (0, 0))] + w_specs,
        out_specs=[pl.BlockSpec((t, tn), lambda j: (0, j)), pl.BlockSpec((d, tn), lambda j: (0, j))],
        out_shape=[jax.ShapeDtypeStruct((t, n), F32), jax.ShapeDtypeStruct((d, n), BF16)],
        compiler_params=_cparams("parallel"),
        name="sample_proj_castw",
    )(u, *([w_t] * len(w_specs)))


def _gate_body(u_ref, wga_ref, wgu_ref, bg_ref, o_ref):
    ga = _dot(u_ref[...], wga_ref[...])
    logit = _dot(ga.astype(BF16), wgu_ref[...]) + bg_ref[...]
    o_ref[...] = _log_sigmoid(logit) / GLA_GATE_TAU


def _gate(u, wga, wgu, bg, *, tm=1024):
    t, d = u.shape
    tm = min(tm, t)
    return pl.pallas_call(
        _gate_body,
        grid=(t // tm,),
        in_specs=[pl.BlockSpec((tm, d), lambda i: (i, 0)),
                  pl.BlockSpec(wga.shape, lambda i: (0, 0)),
                  pl.BlockSpec(wgu.shape, lambda i: (0, 0)),
                  pl.BlockSpec(bg.shape, lambda i: (0, 0))],
        out_specs=pl.BlockSpec((tm, GLA_QK), lambda i: (i, 0)),
        out_shape=jax.ShapeDtypeStruct((t, GLA_QK), F32),
        compiler_params=_cparams("parallel"),
        name="log_gate",
    )(u, wga, wgu, bg)


def _row_group_mask(rows, cols, bt, i):
    return (lax.broadcasted_iota(jnp.int32, (rows, cols), 0) % bt) == i


class _Decode:
    def __init__(self, intra, q_dec, k_end, v_all, new_state, s_ref, so_ref, nt, bt):
        self.intra, self.q_dec, self.k_end, self.v_all = intra, q_dec, k_end, v_all
        self.new_state, self.s_ref, self.so_ref, self.nt, self.bt = new_state, s_ref, so_ref, nt, bt
        self.inter = jnp.zeros((nt * bt, v_all.shape[1]), F32)

    def units(self, lo, hi):
        rows, bt = self.nt * self.bt, self.bt
        for i in range(lo, hi):
            st = self.s_ref[i]
            qs = _dot(self.q_dec, st.astype(BF16))
            self.inter = self.inter + jnp.where(_row_group_mask(rows, qs.shape[1], bt, i), qs, 0.0)
            k_i = jnp.where(_row_group_mask(rows, self.k_end.shape[1], bt, i), self.k_end, 0.0).astype(BF16)
            self.so_ref[i] = self.new_state(i, st, _dot_ta(k_i, self.v_all))

    def finish(self, g_ref, ng_ref, o_ref):
        bt = self.bt
        for t in range(self.nt):
            o = self.intra[t] + self.inter[t * bt:(t + 1) * bt, :]
            o_ref[t] = _rms(o, ng_ref[...]) * _silu(g_ref[t])


def _gla_decode_prep(q_ref, k_ref, v_ref, la_ref, s_ref, so_ref, *, nt, bt):
    q = [q_ref[t] * (GLA_DK ** -0.5) for t in range(nt)]
    k = [k_ref[t] for t in range(nt)]
    v = [v_ref[t] for t in range(nt)]
    b = [la_ref[0]]
    for t in range(1, nt):
        b.append(b[-1] + la_ref[t])
    b_mid, b_last = b[nt // 2], b[nt - 1]
    q_rel = [q[t] * jnp.exp(b[t] - b_mid) for t in range(nt)]
    k_rel = [k[t] * jnp.exp(b_mid - b[t]) for t in range(nt)]
    intra = []
    for t in range(nt):
        acc = None
        for s in range(t + 1):
            term = jnp.sum(q_rel[t] * k_rel[s], axis=-1, keepdims=True) * v[s]
            acc = term if acc is None else acc + term
        intra.append(acc)
    q_dec = jnp.concatenate([q[t] * jnp.exp(b[t]) for t in range(nt)], axis=0).astype(BF16)
    k_end = jnp.concatenate([k[t] * jnp.exp(b_last - b[t]) for t in range(nt)], axis=0)
    v_all = jnp.concatenate(v, axis=0).astype(BF16)
    decay = jnp.exp(b_last)

    def new_state(i, st, upd):
        dcol = _lane_replicated_column(decay[i:i + 1, :])
        return jnp.concatenate([dcol] * (GLA_DV // LANES), axis=1) * st + upd

    return _Decode(intra, q_dec, k_end, v_all, new_state, s_ref, so_ref, nt, bt)


def _ret_decode_prep(pw_ref, q_ref, k_ref, v_ref, cos_ref, sin_ref, s_ref, so_ref, *, nt, bt):
    h = pl.program_id(1)
    pw = [pw_ref[h, n] for n in range(nt + 1)]
    q = [_rope(q_ref[t], cos_ref[t:t + 1, :], sin_ref[t:t + 1, :]) for t in range(nt)]
    k = [_rope(k_ref[t], cos_ref[t:t + 1, :], sin_ref[t:t + 1, :]) * (RET_DK ** -0.5) for t in range(nt)]
    v = [v_ref[t] for t in range(nt)]
    intra = []
    for t in range(nt):
        acc = None
        for s in range(t + 1):
            term = (jnp.sum(q[t] * k[s], axis=-1, keepdims=True) * pw[t - s]) * v[s]
            acc = term if acc is None else acc + term
        intra.append(acc)
    q_dec = jnp.concatenate([q[t] * pw[t + 1] for t in range(nt)], axis=0).astype(BF16)
    k_end = jnp.concatenate([k[t] * pw[nt - 1 - t] for t in range(nt)], axis=0)
    v_all = jnp.concatenate(v, axis=0).astype(BF16)
    return _Decode(intra, q_dec, k_end, v_all, lambda i, st, upd: pw[nt] * st + upd, s_ref, so_ref, nt, bt)


def kernel(x_prompt, x_sample, state_gla, state_ret, ffn1_norm, ffn1_w1, ffn1_w3, ffn1_w2, mix_norm, w_in, w_gate_up, b_gate, gla_norm, w_gla_o, ret_norm, w_ret_o, w_out, ffn2_norm, ffn2_w1, ffn2_w3, ffn2_w2, final_norm):
    depth = w_in.shape[0]
    batch, seq, d = x_prompt.shape
    nb, nt, _ = x_sample.shape

    hp = x_prompt.reshape(batch * seq, d)
    hs = x_sample.transpose(1, 0, 2).reshape(nt * nb, d)
    gla_p, ret_p, gla_s, ret_s = [], [], [], []
    for l in range(depth):
        last = l == depth - 1
        wl = w_in[l].T
        w_ga = w_in[l][:, WA_WIDTH:WB_START].astype(BF16)
        wgu = w_gate_up[l].astype(BF16)
        bg = b_gate[l].reshape(1, GLA_QK)
        ng = gla_norm[l].reshape(GLA_HEADS, 1, GLA_DV)
        nr = ret_norm[l].reshape(RET_HEADS, 1, RET_DV)
        g_next = final_norm if last else ffn1_norm[l + 1]

        hs, us, *f1 = _ffn(hs, ffn1_norm[l], ffn1_w1[l], ffn1_w3[l], ffn1_w2[l], mix_norm[l], final=False)
        hp, up = _ffn(hp, ffn1_norm[l], *f1, mix_norm[l], final=False)

        pg, w_a = _proj(us, wl, 0, WA_WIDTH)
        pr, w_b = _proj(us, wl, WB_START, WB_ZA)
        la = _gate(us, w_ga, wgu, bg)
        la_p = _gate(up, w_ga, wgu, bg)
        og, sg, og_s, sg_s = _gla_mixer(up, la_p, batch, seq, w_a, ng,
                                        pg.reshape(nt, nb, -1), la.reshape(nt, nb, -1), state_gla[l])
        orr, sr, or_s, sr_s = _ret_mixer(up, batch, seq, w_b, nr, pr.reshape(nt, nb, -1), state_ret[l])
        gla_s.append(sg_s)
        ret_s.append(sr_s)
        gla_p.append(sg)
        ret_p.append(sr)

        ms, wza, wzb, wgo, wro = _merge_gate(us, og_s.reshape(nt * nb, GLA_V), or_s.reshape(nt * nb, RET_V),
                                             wl, w_gla_o[l], w_ret_o[l])
        mp, = _merge_gate(up, og, orr, (wza, wzb), wgo, wro)
        hs, wout = _out_proj(hs, ms, w_out[l])
        hp, = _out_proj(hp, mp, wout)

        hs, *f2 = _ffn(hs, ffn2_norm[l], ffn2_w1[l], ffn2_w3[l], ffn2_w2[l], g_next, final=last)
        hp, *_ = _ffn(hp, ffn2_norm[l], *f2[-3:], g_next, final=last)

    y_prompt = hp.reshape(batch, seq, d)
    y_sample = hs.reshape(nt, nb, d).transpose(1, 0, 2)
    return (y_prompt, y_sample, jnp.stack(gla_p), jnp.stack(ret_p), jnp.stack(gla_s), jnp.stack(ret_s))
```

```python
import functools

import jax
import jax.numpy as jnp
from jax import lax
from jax.experimental import pallas as pl
from jax.experimental.pallas import tpu as pltpu

F32, BF16 = jnp.float32, jnp.bfloat16

D_MODEL = 2048
PAST_LEN = 16384
GLA_HEADS = 4
GLA_DK = D_MODEL // (2 * GLA_HEADS)
GLA_DV = D_MODEL // GLA_HEADS
GLA_GATE_RANK = 16
GLA_GATE_TAU = 16.0
RET_HEADS = 8
RET_DK = D_MODEL // RET_HEADS
RET_DV = 2 * D_MODEL // RET_HEADS
ROPE_BASE = 10000.0
EPS = 1e-6
GLA_QK = GLA_HEADS * GLA_DK
GLA_V = GLA_HEADS * GLA_DV
RET_QK = RET_HEADS * RET_DK
RET_V = RET_HEADS * RET_DV

WA_Q, WA_K, WA_V, WA_G = 0, GLA_QK, 2 * GLA_QK, 2 * GLA_QK + GLA_V
WA_WIDTH = 2 * GLA_QK + 2 * GLA_V
WB_Q, WB_K, WB_V, WB_G = 0, RET_QK, 2 * RET_QK, 2 * RET_QK + RET_V
WB_ZA = 2 * RET_QK + 2 * RET_V
WB_ZB = WB_ZA + D_MODEL
WB_START = WA_WIDTH + GLA_GATE_RANK

LANES = 128
SUBLANES = 8
MXU_WIDTH = 256
VMEM_LIMIT = 58 * 2**20

GLA_CHUNK = 64
RET_CHUNK = 256
SEQ_TILE = 512


def _cparams(*sem):
    return pltpu.CompilerParams(dimension_semantics=sem, vmem_limit_bytes=VMEM_LIMIT)


def _dot(a, b):
    return jnp.dot(a, b, preferred_element_type=F32)


def _dot_tb(a, b):
    return lax.dot_general(a, b, (((1,), (1,)), ((), ())), preferred_element_type=F32)


def _dot_ta(a, b):
    return lax.dot_general(a, b, (((0,), (0,)), ((), ())), preferred_element_type=F32)


def _rms(x, g):
    return x * lax.rsqrt(jnp.mean(x * x, axis=-1, keepdims=True) + EPS) * g


def _sigmoid(x):
    return 1.0 / (1.0 + jnp.exp(-x))


def _silu(x):
    return x * _sigmoid(x)


def _log_sigmoid(x):
    return jnp.minimum(x, 0.0) - jnp.log(1.0 + jnp.exp(-jnp.abs(x)))


def _lane_replicated_column(row):
    return jnp.broadcast_to(row, (LANES, row.shape[-1])).T


def _ffn_body(x_ref, g_ref, w1_ref, w3_ref, w2_ref, g2_ref, *refs, tm, rows, final, emit_w):
    refs = list(refs)
    u_scr = refs.pop()
    wb_refs = [refs.pop() for _ in range(3)][::-1] if emit_w else None
    if final:
        y_ref, = refs
        acc_ref = y_ref
    else:
        h_ref, un_ref = refs
        acc_ref = h_ref
    j = pl.program_id(1)

    @pl.when(j == 0)
    def _():
        for r in range(0, tm, rows):
            u_scr[r:r + rows, :] = _rms(x_ref[r:r + rows, :], g_ref[...]).astype(BF16)
            acc_ref[r:r + rows, :] = jnp.zeros((rows, acc_ref.shape[1]), F32)

    w1, w3, w2 = w1_ref[...].astype(BF16), w3_ref[...].astype(BF16), w2_ref[...].astype(BF16)
    if emit_w:
        for ref, w in zip(wb_refs, (w1, w3, w2)):
            ref[...] = w
    u = u_scr[...]
    part = None
    for c0 in range(0, w1.shape[1], MXU_WIDTH):
        cols = slice(c0, c0 + MXU_WIDTH)
        a = _dot(u, w1[:, cols])
        b = _dot(u, w3[:, cols])
        p = _dot((_silu(a) * b).astype(BF16), w2[cols, :])
        part = p if part is None else part + p
    acc_ref[...] += part

    @pl.when(j == pl.num_programs(1) - 1)
    def _():
        step = 2 * SUBLANES
        for r in range(0, tm, step):
            h = x_ref[r:r + step, :] + 0.5 * acc_ref[r:r + step, :]
            if final:
                y_ref[r:r + step, :] = _rms(h, g2_ref[...])
            else:
                h_ref[r:r + step, :] = h
                un_ref[r:r + step, :] = _rms(h, g2_ref[...]).astype(BF16)


def _ffn(x, g, w1, w3, w2, g2, *, final, tm=512):
    t, d = x.shape
    dff = w1.shape[1]
    tm = min(tm, t)
    emit_w = w1.dtype == F32
    assert not emit_w or t == tm, "weights are emitted by a single-token-tile call"
    tf = 256 if emit_w else 512
    grid = (t // tm, dff // tf)
    row = pl.BlockSpec((tm, d), lambda i, j: (i, 0))
    vec = pl.BlockSpec((1, d), lambda i, j: (0, 0))
    w_up = pl.BlockSpec((d, tf), lambda i, j: (0, j))
    w_dn = pl.BlockSpec((tf, d), lambda i, j: (j, 0))
    out_shape = [jax.ShapeDtypeStruct((t, d), F32)] + ([] if final else [jax.ShapeDtypeStruct((t, d), BF16)])
    out_specs = [row] * len(out_shape)
    if emit_w:
        out_shape += [jax.ShapeDtypeStruct(w.shape, BF16) for w in (w1, w3, w2)]
        out_specs += [w_up, w_up, w_dn]
    return pl.pallas_call(
        functools.partial(_ffn_body, tm=tm, rows=min(256, tm), final=final, emit_w=emit_w),
        grid=grid,
        in_specs=[row, vec, w_up, w_up, w_dn, vec],
        out_specs=out_specs,
        out_shape=out_shape,
        scratch_shapes=[pltpu.VMEM((tm, d), BF16)],
        compiler_params=_cparams("parallel", "arbitrary"),
        name=("ffn_final" if final else "ffn_mid") + ("_castw" if emit_w else ""),
    )(x, g.reshape(1, d), w1, w3, w2, g2.reshape(1, d))


def _w_cols(a_ref, b_ref):
    rows = a_ref[...] if b_ref is None else jnp.concatenate([a_ref[b_ref.shape[0]:, :], b_ref[...]], axis=0)
    return rows.T.astype(BF16)


def _w_cols_specs(width, tn, start, col_of):
    shift = start % tn
    base = start - shift
    if shift == 0:
        return [pl.BlockSpec((tn, width), lambda *g: (base // tn + col_of(*g), 0))]
    assert shift % SUBLANES == 0 and tn % shift == 0 and base % shift == 0
    return [pl.BlockSpec((tn, width), lambda *g: (base // tn + col_of(*g), 0)),
            pl.BlockSpec((shift, width), lambda *g: ((base + tn * (col_of(*g) + 1)) // shift, 0))]


def _merge_body(u_ref, og_ref, or_ref, *refs, emit_w):
    u = u_ref[...]
    if not emit_w:
        wza_ref, wzb_ref, wgo_ref, wro_ref, m_ref = refs
        wza, wzb, wgo, wro = wza_ref[...], wzb_ref[...], wgo_ref[...], wro_ref[...]
    else:
        za_a, za_b, zb_a, zb_b, wgo_ref, wro_ref, m_ref, wza_o, wzb_o, wgo_o, wro_o = refs
        wza = _w_cols(za_a, za_b)
        wzb = _w_cols(zb_a, zb_b)
        wgo, wro = wgo_ref[...].astype(BF16), wro_ref[...].astype(BF16)
        wza_o[...], wzb_o[...], wgo_o[...], wro_o[...] = wza, wzb, wgo, wro
    za = _dot(u, wza)
    zb = _dot(u, wzb)
    branch_a = _dot(og_ref[...].astype(BF16), wgo)
    branch_b = _dot(or_ref[...].astype(BF16), wro)
    m_ref[...] = (_sigmoid(za) * branch_a + _sigmoid(zb) * branch_b).astype(BF16)


def _merge_gate(u, og, orr, wz, wgo, wro, *, tm=512):
    t, d = u.shape
    tm = min(tm, t)
    emit_w = wgo.dtype == F32
    assert not emit_w or t == tm, "weights are emitted by a single-token-tile call"
    tn = 256 if emit_w else 512
    col = lambda i, j: j
    acts = [pl.BlockSpec((tm, d), lambda i, j: (i, 0)),
            pl.BlockSpec((tm, GLA_V), lambda i, j: (i, 0)),
            pl.BlockSpec((tm, RET_V), lambda i, j: (i, 0))]
    w_col = lambda rows: pl.BlockSpec((rows, tn), lambda i, j: (0, j))
    out_shape = [jax.ShapeDtypeStruct((t, d), BF16)]
    out_specs = [pl.BlockSpec((tm, tn), lambda i, j: (i, j))]
    if emit_w:
        w_specs = (_w_cols_specs(d, tn, WB_START + WB_ZA, col) + _w_cols_specs(d, tn, WB_START + WB_ZB, col)
                   + [w_col(GLA_V), w_col(RET_V)])
        w_args = (wz, wz, wz, wz, wgo, wro)
        out_shape += [jax.ShapeDtypeStruct((d, d), BF16), jax.ShapeDtypeStruct((d, d), BF16),
                      jax.ShapeDtypeStruct(wgo.shape, BF16), jax.ShapeDtypeStruct(wro.shape, BF16)]
        out_specs += [w_col(d), w_col(d), w_col(GLA_V), w_col(RET_V)]
    else:
        w_specs, w_args = [w_col(d), w_col(d), w_col(GLA_V), w_col(RET_V)], (*wz, wgo, wro)
    return pl.pallas_call(
        functools.partial(_merge_body, emit_w=emit_w),
        grid=(t // tm, d // tn),
        in_specs=acts + w_specs,
        out_specs=out_specs,
        out_shape=out_shape,
        compiler_params=_cparams("parallel", "parallel"),
        name="merge_gate" + ("_castw" if emit_w else ""),
    )(u, og, orr, *w_args)


def _out_body(h_ref, m_ref, wout_ref, o_ref, *wb_ref):
    wout = wout_ref[...].astype(BF16)
    if wb_ref:
        wb_ref[0][...] = wout
    o_ref[...] = h_ref[...] + _dot(m_ref[...], wout)


def _out_proj(h, merged, wout, *, tm=512):
    t, d = h.shape
    tm = min(tm, t)
    emit_w = wout.dtype == F32
    assert not emit_w or t == tm, "weights are emitted by a single-token-tile call"
    tn = 512 if emit_w else d
    out_shape = [jax.ShapeDtypeStruct((t, d), F32)] + ([jax.ShapeDtypeStruct((d, d), BF16)] if emit_w else [])
    out_specs = [pl.BlockSpec((tm, tn), lambda i, j: (i, j))] + ([pl.BlockSpec((d, tn), lambda i, j: (0, j))] if emit_w else [])
    return pl.pallas_call(
        _out_body,
        grid=(t // tm, d // tn),
        in_specs=[pl.BlockSpec((tm, tn), lambda i, j: (i, j)),
                  pl.BlockSpec((tm, d), lambda i, j: (i, 0)),
                  pl.BlockSpec((d, tn), lambda i, j: (0, j))],
        out_specs=out_specs,
        out_shape=out_shape,
        compiler_params=_cparams("parallel", "parallel"),
        name="out_proj" + ("_castw" if emit_w else ""),
    )(h, merged, wout)


def _gla_body(u_ref, wq_ref, wk_ref, wv_ref, wg_ref, la_ref, ng_ref,
              dq_ref, dk_ref, dv_ref, dg_ref, dla_ref, ds_ref,
              og_ref, st_ref, dog_ref, dso_ref, s_scr, *, ts, c, nt, bt):
    s = pl.program_id(2)

    @pl.when(s == 0)
    def _():
        s_scr[...] = jnp.zeros_like(s_scr)

    u = u_ref[...]
    q_all = _dot(u, wq_ref[...]) * (GLA_DK ** -0.5)
    k_all = _dot(u, wk_ref[...])
    v_all = _dot(u, wv_ref[...]).astype(BF16)
    g_all = _dot(u, wg_ref[...])
    la_all = la_ref[...]
    la_hi = la_all.astype(BF16)
    la_lo = (la_all - la_hi.astype(F32)).astype(BF16)

    rr = lax.broadcasted_iota(jnp.int32, (c, c), 0)
    cc = lax.broadcasted_iota(jnp.int32, (c, c), 1)
    causal = rr >= cc
    tril = causal.astype(BF16)
    mid = c // 2
    rows = [slice(ci * c, (ci + 1) * c) for ci in range(ts // c)]

    b = [_dot(tril, la_hi[r]) + _dot(tril, la_lo[r]) for r in rows]
    q_rel, k_rel, q_dec, k_end, decay = [], [], [], [], []
    for r, bc in zip(rows, b):
        q, k = q_all[r], k_all[r]
        b_mid, b_last = bc[mid:mid + 1, :], bc[c - 1:c, :]
        q_rel.append((q * jnp.exp(bc - b_mid)).astype(BF16))
        k_rel.append((k * jnp.exp(b_mid - bc)).astype(BF16))
        q_dec.append((q * jnp.exp(bc)).astype(BF16))
        k_end.append((k * jnp.exp(b_last - bc)).astype(BF16))
        dcol = _lane_replicated_column(jnp.exp(b_last))
        decay.append(jnp.concatenate([dcol] * (GLA_DV // LANES), axis=1))
    att = [jnp.where(causal, _dot_tb(qr, kr), 0.0).astype(BF16) for qr, kr in zip(q_rel, k_rel)]
    o_intra = [_dot(a, v_all[r]) for a, r in zip(att, rows)]
    upd = [_dot_ta(ke, v_all[r]) for ke, r in zip(k_end, rows)]

    dec = _gla_decode_prep(dq_ref, dk_ref, dv_ref, dla_ref, ds_ref, dso_ref, nt=nt, bt=bt)
    dec.units(0, bt)
    dec.finish(dg_ref, ng_ref, dog_ref)

    st = s_scr[...]
    for ci, r in enumerate(rows):
        o = o_intra[ci] + _dot(q_dec[ci], st.astype(BF16))
        og_ref[r, :] = (_rms(o, ng_ref[...]) * _silu(g_all[r])).astype(BF16)
        st = decay[ci] * st + upd[ci]
    s_scr[...] = st

    @pl.when(s == pl.num_programs(2) - 1)
    def _():
        st_ref[...] = st


def _decode_tile(batch, ns, nb):
    bt = nb // (batch * ns)
    assert bt * batch * ns == nb and bt % SUBLANES == 0
    return bt


def _gla_mixer(u, la_p, batch, seq, w_a, ng, pg, la, state):
    ts, c = SEQ_TILE, GLA_CHUNK
    ns = seq // ts
    d = u.shape[1]
    nt, nb, _ = pg.shape
    bt = _decode_tile(batch, ns, nb)
    kq, kk, kv, kg = WA_Q // GLA_DK, WA_K // GLA_DK, WA_V // GLA_DV, WA_G // GLA_DV
    dec = lambda width, col0: pl.BlockSpec((nt, bt, width), lambda b, h, s: (0, b * ns + s, col0 + h))
    st_spec = pl.BlockSpec((bt, None, GLA_DK, GLA_DV), lambda b, h, s: (b * ns + s, h, 0, 0))
    return pl.pallas_call(
        functools.partial(_gla_body, ts=ts, c=c, nt=nt, bt=bt),
        grid=(batch, GLA_HEADS, ns),
        in_specs=[pl.BlockSpec((ts, d), lambda b, h, s: (b * ns + s, 0)),
                  pl.BlockSpec((d, GLA_DK), lambda b, h, s: (0, kq + h)),
                  pl.BlockSpec((d, GLA_DK), lambda b, h, s: (0, kk + h)),
                  pl.BlockSpec((d, GLA_DV), lambda b, h, s: (0, kv + h)),
                  pl.BlockSpec((d, GLA_DV), lambda b, h, s: (0, kg + h)),
                  pl.BlockSpec((ts, GLA_DK), lambda b, h, s: (b * ns + s, h)),
                  pl.BlockSpec((None, 1, GLA_DV), lambda b, h, s: (h, 0, 0)),
                  dec(GLA_DK, kq), dec(GLA_DK, kk), dec(GLA_DV, kv), dec(GLA_DV, kg), dec(GLA_DK, 0), st_spec],
        out_specs=(pl.BlockSpec((ts, GLA_DV), lambda b, h, s: (b * ns + s, h)),
                   pl.BlockSpec((None, None, GLA_DK, GLA_DV), lambda b, h, s: (b, h, 0, 0)),
                   dec(GLA_DV, 0), st_spec),
        out_shape=(jax.ShapeDtypeStruct((batch * seq, GLA_V), BF16),
                   jax.ShapeDtypeStruct((batch, GLA_HEADS, GLA_DK, GLA_DV), F32),
                   jax.ShapeDtypeStruct((nt, nb, GLA_V), F32),
                   jax.ShapeDtypeStruct(state.shape, F32)),
        scratch_shapes=[pltpu.VMEM((GLA_DK, GLA_DV), F32)],
        compiler_params=_cparams("parallel", "parallel", "arbitrary"),
        name="gla_mixer",
    )(u, w_a, w_a, w_a, w_a, la_p, ng, pg, pg, pg, pg, la, state)


def _rope(x, cos, sin):
    half = x.shape[-1] // 2
    x1, x2 = x[:, :half], x[:, half:]
    return jnp.concatenate([x1 * cos - x2 * sin, x2 * cos + x1 * sin], axis=-1)


def _ret_body(u_ref, wq_ref, wk_ref, wv_ref, wg_ref, cos_ref, sin_ref, dm_ref, qd_ref, kd_ref, cd_ref, ng_ref,
              pw_ref, dq_ref, dk_ref, dv_ref, dg_ref, dcos_ref, dsin_ref, ds_ref,
              or_ref, st_ref, dor_ref, dso_ref, s_scr, *, ts, c, nt, bt):
    s = pl.program_id(2)

    @pl.when(s == 0)
    def _():
        s_scr[...] = jnp.zeros_like(s_scr)

    dec = _ret_decode_prep(pw_ref, dq_ref, dk_ref, dv_ref, dcos_ref, dsin_ref, ds_ref, dso_ref, nt=nt, bt=bt)
    cut = [bt * n // 4 for n in range(5)]

    u = u_ref[...]
    tok = pl.ds(pl.multiple_of(s * ts, ts), ts)
    cos, sin = cos_ref[tok, :], sin_ref[tok, :]
    q_all = _rope(_dot(u, wq_ref[...]), cos, sin)
    dec.units(cut[0], cut[1])
    k_all = _rope(_dot(u, wk_ref[...]), cos, sin) * (RET_DK ** -0.5)
    dec.units(cut[1], cut[2])
    v_all = _dot(u, wv_ref[...]).astype(BF16)
    dec.units(cut[2], cut[3])
    g_all = _dot(u, wg_ref[...])
    dec.units(cut[3], cut[4])
    dec.finish(dg_ref, ng_ref, dor_ref)

    rows = [slice(ci * c, (ci + 1) * c) for ci in range(ts // c)]
    att = [(_dot_tb(q_all[r].astype(BF16), k_all[r].astype(BF16)) * dm_ref[...]).astype(BF16) for r in rows]
    o_intra = [_dot(a, v_all[r]) for a, r in zip(att, rows)]
    upd = [_dot_ta((k_all[r] * kd_ref[...]).astype(BF16), v_all[r]) for r in rows]
    q_dec = [(q_all[r] * qd_ref[...]).astype(BF16) for r in rows]
    st = s_scr[...]
    for ci, r in enumerate(rows):
        o = o_intra[ci] + _dot(q_dec[ci], st.astype(BF16))
        or_ref[r, :] = (_rms(o, ng_ref[...]) * _silu(g_all[r])).astype(BF16)
        st = cd_ref[...] * st + upd[ci]
    s_scr[...] = st

    @pl.when(s == pl.num_programs(2) - 1)
    def _():
        st_ref[...] = st


def _ret_log_gamma():
    return jnp.log1p(-jnp.exp2(-5.0 - jnp.arange(RET_HEADS, dtype=F32)))


def _rope_tables(pos):
    half = RET_DK // 2
    freqs = ROPE_BASE ** (-jnp.arange(half, dtype=F32) / half)
    ang = pos[:, None] * freqs[None, :]
    return jnp.cos(ang), jnp.sin(ang)


def _ret_mixer(u, batch, seq, w_b, ng, pr, state):
    ts, c = SEQ_TILE, RET_CHUNK
    ns = seq // ts
    d = u.shape[1]
    nt, nb, _ = pr.shape
    bt = _decode_tile(batch, ns, nb)
    kq, kk, kv, kg = WB_Q // RET_DK, WB_K // RET_DK, WB_V // RET_DV, WB_G // RET_DV
    cos, sin = _rope_tables(jnp.arange(seq, dtype=F32))
    dcos, dsin = _rope_tables(PAST_LEN + jnp.arange(nt, dtype=F32))
    pw = jnp.exp(_ret_log_gamma()[:, None] * jnp.arange(nt + 1, dtype=F32)[None, :])
    dec = lambda width, col0: pl.BlockSpec((nt, bt, width), lambda b, h, s: (0, b * ns + s, col0 + h))
    rope_spec = pl.BlockSpec((nt, RET_DK // 2), lambda b, h, s: (0, 0))
    st_spec = pl.BlockSpec((bt, None, RET_DK, RET_DV), lambda b, h, s: (b * ns + s, h, 0, 0))
    lg = _ret_log_gamma()
    idx = jnp.arange(c, dtype=F32)
    diff = idx[:, None] - idx[None, :]
    dmat = jnp.where(diff >= 0, jnp.exp(lg[:, None, None] * jnp.maximum(diff, 0.0)), 0.0)
    qdec = jnp.broadcast_to(jnp.exp(lg[:, None] * (idx + 1.0)[None, :])[:, :, None], (RET_HEADS, c, RET_DK))
    kdec = jnp.broadcast_to(jnp.exp(lg[:, None] * (c - 1.0 - idx)[None, :])[:, :, None], (RET_HEADS, c, RET_DK))
    cdec = jnp.broadcast_to(jnp.exp(lg * c)[:, None, None], (RET_HEADS, 1, RET_DV))
    return pl.pallas_call(
        functools.partial(_ret_body, ts=ts, c=c, nt=nt, bt=bt),
        grid=(batch, RET_HEADS, ns),
        in_specs=[pl.BlockSpec((ts, d), lambda b, h, s: (b * ns + s, 0)),
                  pl.BlockSpec((d, RET_DK), lambda b, h, s: (0, kq + h)),
                  pl.BlockSpec((d, RET_DK), lambda b, h, s: (0, kk + h)),
                  pl.BlockSpec((d, RET_DV), lambda b, h, s: (0, kv + h)),
                  pl.BlockSpec((d, RET_DV), lambda b, h, s: (0, kg + h)),
                  pl.BlockSpec((seq, RET_DK // 2), lambda b, h, s: (0, 0)),
                  pl.BlockSpec((seq, RET_DK // 2), lambda b, h, s: (0, 0)),
                  pl.BlockSpec((None, c, c), lambda b, h, s: (h, 0, 0)),
                  pl.BlockSpec((None, c, RET_DK), lambda b, h, s: (h, 0, 0)),
                  pl.BlockSpec((None, c, RET_DK), lambda b, h, s: (h, 0, 0)),
                  pl.BlockSpec((None, 1, RET_DV), lambda b, h, s: (h, 0, 0)),
                  pl.BlockSpec((None, 1, RET_DV), lambda b, h, s: (h, 0, 0)),
                  pl.BlockSpec(memory_space=pltpu.SMEM),
                  dec(RET_DK, kq), dec(RET_DK, kk), dec(RET_DV, kv), dec(RET_DV, kg), rope_spec, rope_spec, st_spec],
        out_specs=(pl.BlockSpec((ts, RET_DV), lambda b, h, s: (b * ns + s, h)),
                   pl.BlockSpec((None, None, RET_DK, RET_DV), lambda b, h, s: (b, h, 0, 0)),
                   dec(RET_DV, 0), st_spec),
        out_shape=(jax.ShapeDtypeStruct((batch * seq, RET_V), BF16),
                   jax.ShapeDtypeStruct((batch, RET_HEADS, RET_DK, RET_DV), F32),
                   jax.ShapeDtypeStruct((nt, nb, RET_V), F32),
                   jax.ShapeDtypeStruct(state.shape, F32)),
        scratch_shapes=[pltpu.VMEM((RET_DK, RET_DV), F32)],
        compiler_params=_cparams("parallel", "parallel", "arbitrary"),
        name="ret_mixer",
    )(u, w_b, w_b, w_b, w_b, cos, sin, dmat, qdec, kdec, cdec, ng, pw, pr, pr, pr, pr, dcos, dsin, state)


def _proj_body(u_ref, *refs):
    *w_refs, o_ref, wb_ref = refs
    w = _w_cols(w_refs[0], w_refs[1] if len(w_refs) > 1 else None)
    wb_ref[...] = w
    o_ref[...] = _dot(u_ref[...], w)


def _proj(u, w_t, start, n, *, tn=1024):
    t, d = u.shape
    w_specs = _w_cols_specs(d, tn, start, lambda j: j)
    return pl.pallas_call(
        _proj_body,
        grid=(n // tn,),
        in_specs=[pl.BlockSpec((t, d), lambda j: (0, 0))] + w_specs,
        out_specs=[pl.BlockSpec((t, tn), lambda j: (0, j)), pl.BlockSpec((d, tn), lambda j: (0, j))],
        out_shape=[jax.ShapeDtypeStruct((t, n), F32), jax.ShapeDtypeStruct((d, n), BF16)],
        compiler_params=_cparams("parallel"),
        name="sample_proj_castw",
    )(u, *([w_t] * len(w_specs)))


def _gate_body(u_ref, wga_ref, wgu_ref, bg_ref, o_ref):
    ga = _dot(u_ref[...], wga_ref[...])
    logit = _dot(ga.astype(BF16), wgu_ref[...]) + bg_ref[...]
    o_ref[...] = _log_sigmoid(logit) / GLA_GATE_TAU


def _gate(u, wga, wgu, bg, *, tm=1024):
    t, d = u.shape
    tm = min(tm, t)
    return pl.pallas_call(
        _gate_body,
        grid=(t // tm,),
        in_specs=[pl.BlockSpec((tm, d), lambda i: (i, 0)),
                  pl.BlockSpec(wga.shape, lambda i: (0, 0)),
                  pl.BlockSpec(wgu.shape, lambda i: (0, 0)),
                  pl.BlockSpec(bg.shape, lambda i: (0, 0))],
        out_specs=pl.BlockSpec((tm, GLA_QK), lambda i: (i, 0)),
        out_shape=jax.ShapeDtypeStruct((t, GLA_QK), F32),
        compiler_params=_cparams("parallel"),
        name="log_gate",
    )(u, wga, wgu, bg)


def _row_group_mask(rows, cols, bt, i):
    return (lax.broadcasted_iota(jnp.int32, (rows, cols), 0) % bt) == i


class _Decode:
    def __init__(self, intra, q_dec, k_end, v_all, new_state, s_ref, so_ref, nt, bt):
        self.intra, self.q_dec, self.k_end, self.v_all = intra, q_dec, k_end, v_all
        self.new_state, self.s_ref, self.so_ref, self.nt, self.bt = new_state, s_ref, so_ref, nt, bt
        self.inter = jnp.zeros((nt * bt, v_all.shape[1]), F32)

    def units(self, lo, hi):
        rows, bt = self.nt * self.bt, self.bt
        for i in range(lo, hi):
            st = self.s_ref[i]
            qs = _dot(self.q_dec, st.astype(BF16))
            self.inter = self.inter + jnp.where(_row_group_mask(rows, qs.shape[1], bt, i), qs, 0.0)
            k_i = jnp.where(_row_group_mask(rows, self.k_end.shape[1], bt, i), self.k_end, 0.0).astype(BF16)
            self.so_ref[i] = self.new_state(i, st, _dot_ta(k_i, self.v_all))

    def finish(self, g_ref, ng_ref, o_ref):
        bt = self.bt
        for t in range(self.nt):
            o = self.intra[t] + self.inter[t * bt:(t + 1) * bt, :]
            o_ref[t] = _rms(o, ng_ref[...]) * _silu(g_ref[t])


def _gla_decode_prep(q_ref, k_ref, v_ref, la_ref, s_ref, so_ref, *, nt, bt):
    q = [q_ref[t] * (GLA_DK ** -0.5) for t in range(nt)]
    k = [k_ref[t] for t in range(nt)]
    v = [v_ref[t] for t in range(nt)]
    b = [la_ref[0]]
    for t in range(1, nt):
        b.append(b[-1] + la_ref[t])
    b_mid, b_last = b[nt // 2], b[nt - 1]
    q_rel = [q[t] * jnp.exp(b[t] - b_mid) for t in range(nt)]
    k_rel = [k[t] * jnp.exp(b_mid - b[t]) for t in range(nt)]
    intra = []
    for t in range(nt):
        acc = None
        for s in range(t + 1):
            term = jnp.sum(q_rel[t] * k_rel[s], axis=-1, keepdims=True) * v[s]
            acc = term if acc is None else acc + term
        intra.append(acc)
    q_dec = jnp.concatenate([q[t] * jnp.exp(b[t]) for t in range(nt)], axis=0).astype(BF16)
    k_end = jnp.concatenate([k[t] * jnp.exp(b_last - b[t]) for t in range(nt)], axis=0)
    v_all = jnp.concatenate(v, axis=0).astype(BF16)
    decay = jnp.exp(b_last)

    def new_state(i, st, upd):
        dcol = _lane_replicated_column(decay[i:i + 1, :])
        return jnp.concatenate([dcol] * (GLA_DV // LANES), axis=1) * st + upd

    return _Decode(intra, q_dec, k_end, v_all, new_state, s_ref, so_ref, nt, bt)


def _ret_decode_prep(pw_ref, q_ref, k_ref, v_ref, cos_ref, sin_ref, s_ref, so_ref, *, nt, bt):
    h = pl.program_id(1)
    pw = [pw_ref[h, n] for n in range(nt + 1)]
    q = [_rope(q_ref[t], cos_ref[t:t + 1, :], sin_ref[t:t + 1, :]) for t in range(nt)]
    k = [_rope(k_ref[t], cos_ref[t:t + 1, :], sin_ref[t:t + 1, :]) * (RET_DK ** -0.5) for t in range(nt)]
    v = [v_ref[t] for t in range(nt)]
    intra = []
    for t in range(nt):
        acc = None
        for s in range(t + 1):
            term = (jnp.sum(q[t] * k[s], axis=-1, keepdims=True) * pw[t - s]) * v[s]
            acc = term if acc is None else acc + term
        intra.append(acc)
    q_dec = jnp.concatenate([q[t] * pw[t + 1] for t in range(nt)], axis=0).astype(BF16)
    k_end = jnp.concatenate([k[t] * pw[nt - 1 - t] for t in range(nt)], axis=0)
    v_all = jnp.concatenate(v, axis=0).astype(BF16)
    return _Decode(intra, q_dec, k_end, v_all, lambda i, st, upd: pw[nt] * st + upd, s_ref, so_ref, nt, bt)


def kernel(x_prompt, x_sample, state_gla, state_ret, ffn1_norm, ffn1_w1, ffn1_w3, ffn1_w2, mix_norm, w_in, w_gate_up, b_gate, gla_norm, w_gla_o, ret_norm, w_ret_o, w_out, ffn2_norm, ffn2_w1, ffn2_w3, ffn2_w2, final_norm):
    depth = w_in.shape[0]
    batch, seq, d = x_prompt.shape
    nb, nt, _ = x_sample.shape

    hp = x_prompt.reshape(batch * seq, d)
    hs = x_sample.transpose(1, 0, 2).reshape(nt * nb, d)
    gla_p, ret_p, gla_s, ret_s = [], [], [], []
    for l in range(depth):
        last = l == depth - 1
        wl = w_in[l].T
        w_ga = w_in[l][:, WA_WIDTH:WB_START].astype(BF16)
        wgu = w_gate_up[l].astype(BF16)
        bg = b_gate[l].reshape(1, GLA_QK)
        ng = gla_norm[l].reshape(GLA_HEADS, 1, GLA_DV)
        nr = ret_norm[l].reshape(RET_HEADS, 1, RET_DV)
        g_next = final_norm if last else ffn1_norm[l + 1]

        hs, us, *f1 = _ffn(hs, ffn1_norm[l], ffn1_w1[l], ffn1_w3[l], ffn1_w2[l], mix_norm[l], final=False)
        hp, up = _ffn(hp, ffn1_norm[l], *f1, mix_norm[l], final=False)

        pg, w_a = _proj(us, wl, 0, WA_WIDTH)
        pr, w_b = _proj(us, wl, WB_START, WB_ZA)
        la = _gate(us, w_ga, wgu, bg)
        la_p = _gate(up, w_ga, wgu, bg)
        og, sg, og_s, sg_s = _gla_mixer(up, la_p, batch, seq, w_a, ng,
                                        pg.reshape(nt, nb, -1), la.reshape(nt, nb, -1), state_gla[l])
        orr, sr, or_s, sr_s = _ret_mixer(up, batch, seq, w_b, nr, pr.reshape(nt, nb, -1), state_ret[l])
        gla_s.append(sg_s)
        ret_s.append(sr_s)
        gla_p.append(sg)
        ret_p.append(sr)

        ms, wza, wzb, wgo, wro = _merge_gate(us, og_s.reshape(nt * nb, GLA_V), or_s.reshape(nt * nb, RET_V),
                                             wl, w_gla_o[l], w_ret_o[l])
        mp, = _merge_gate(up, og, orr, (wza, wzb), wgo, wro)
        hs, wout = _out_proj(hs, ms, w_out[l])
        hp, = _out_proj(hp, mp, wout)

        hs, *f2 = _ffn(hs, ffn2_norm[l], ffn2_w1[l], ffn2_w3[l], ffn2_w2[l], g_next, final=last)
        hp, *_ = _ffn(hp, ffn2_norm[l], *f2[-3:], g_next, final=last)

    y_prompt = hp.reshape(batch, seq, d)
    y_sample = hs.reshape(nt, nb, d).transpose(1, 0, 2)
    return (y_prompt, y_sample, jnp.stack(gla_p), jnp.stack(ret_p), jnp.stack(gla_s), jnp.stack(ret_s))
```

```python
import functools

import jax
import jax.numpy as jnp
from jax import lax
from jax.experimental import pallas as pl
from jax.experimental.pallas import tpu as pltpu

F32, BF16 = jnp.float32, jnp.bfloat16

D_MODEL = 2048
PAST_LEN = 16384
GLA_HEADS = 4
GLA_DK = D_MODEL // (2 * GLA_HEADS)
GLA_DV = D_MODEL // GLA_HEADS
GLA_GATE_RANK = 16
GLA_GATE_TAU = 16.0
RET_HEADS = 8
RET_DK = D_MODEL // RET_HEADS
RET_DV = 2 * D_MODEL // RET_HEADS
ROPE_BASE = 10000.0
EPS = 1e-6
GLA_QK = GLA_HEADS * GLA_DK
GLA_V = GLA_HEADS * GLA_DV
RET_QK = RET_HEADS * RET_DK
RET_V = RET_HEADS * RET_DV

WA_Q, WA_K, WA_V, WA_G = 0, GLA_QK, 2 * GLA_QK, 2 * GLA_QK + GLA_V
WA_WIDTH = 2 * GLA_QK + 2 * GLA_V
WB_Q, WB_K, WB_V, WB_G = 0, RET_QK, 2 * RET_QK, 2 * RET_QK + RET_V
WB_ZA = 2 * RET_QK + 2 * RET_V
WB_ZB = WB_ZA + D_MODEL
WB_START = WA_WIDTH + GLA_GATE_RANK

LANES = 128
SUBLANES = 8
MXU_WIDTH = 256
VMEM_LIMIT = 58 * 2**20

GLA_CHUNK = 64
RET_CHUNK = 256
SEQ_TILE = 512


def _cparams(*sem):
    return pltpu.CompilerParams(dimension_semantics=sem, vmem_limit_bytes=VMEM_LIMIT)


def _dot(a, b):
    return jnp.dot(a, b, preferred_element_type=F32)


def _dot_tb(a, b):
    return lax.dot_general(a, b, (((1,), (1,)), ((), ())), preferred_element_type=F32)


def _dot_ta(a, b):
    return lax.dot_general(a, b, (((0,), (0,)), ((), ())), preferred_element_type=F32)


def _rms(x, g):
    return x * lax.rsqrt(jnp.mean(x * x, axis=-1, keepdims=True) + EPS) * g


def _sigmoid(x):
    return 1.0 / (1.0 + jnp.exp(-x))


def _silu(x):
    return x * _sigmoid(x)


def _log_sigmoid(x):
    return jnp.minimum(x, 0.0) - jnp.log(1.0 + jnp.exp(-jnp.abs(x)))


def _lane_replicated_column(row):
    return jnp.broadcast_to(row, (LANES, row.shape[-1])).T


def _ffn_body(x_ref, g_ref, w1_ref, w3_ref, w2_ref, g2_ref, *refs, tm, rows, final, emit_w):
    refs = list(refs)
    u_scr = refs.pop()
    wb_refs = [refs.pop() for _ in range(3)][::-1] if emit_w else None
    if final:
        y_ref, = refs
        acc_ref = y_ref
    else:
        h_ref, un_ref = refs
        acc_ref = h_ref
    j = pl.program_id(1)

    @pl.when(j == 0)
    def _():
        for r in range(0, tm, rows):
            u_scr[r:r + rows, :] = _rms(x_ref[r:r + rows, :], g_ref[...]).astype(BF16)
            acc_ref[r:r + rows, :] = jnp.zeros((rows, acc_ref.shape[1]), F32)

    w1, w3, w2 = w1_ref[...].astype(BF16), w3_ref[...].astype(BF16), w2_ref[...].astype(BF16)
    if emit_w:
        for ref, w in zip(wb_refs, (w1, w3, w2)):
            ref[...] = w
    u = u_scr[...]
    part = None
    for c0 in range(0, w1.shape[1], MXU_WIDTH):
        cols = slice(c0, c0 + MXU_WIDTH)
        a = _dot(u, w1[:, cols])
        b = _dot(u, w3[:, cols])
        p = _dot((_silu(a) * b).astype(BF16), w2[cols, :])
        part = p if part is None else part + p
    acc_ref[...] += part

    @pl.when(j == pl.num_programs(1) - 1)
    def _():
        step = 2 * SUBLANES
        for r in range(0, tm, step):
            h = x_ref[r:r + step, :] + 0.5 * acc_ref[r:r + step, :]
            if final:
                y_ref[r:r + step, :] = _rms(h, g2_ref[...])
            else:
                h_ref[r:r + step, :] = h
                un_ref[r:r + step, :] = _rms(h, g2_ref[...]).astype(BF16)


def _ffn(x, g, w1, w3, w2, g2, *, final, tm=512):
    t, d = x.shape
    dff = w1.shape[1]
    tm = min(tm, t)
    emit_w = w1.dtype == F32
    assert not emit_w or t == tm, "weights are emitted by a single-token-tile call"
    tf = 256 if emit_w else 512
    grid = (t // tm, dff // tf)
    row = pl.BlockSpec((tm, d), lambda i, j: (i, 0))
    vec = pl.BlockSpec((1, d), lambda i, j: (0, 0))
    w_up = pl.BlockSpec((d, tf), lambda i, j: (0, j))
    w_dn = pl.BlockSpec((tf, d), lambda i, j: (j, 0))
    out_shape = [jax.ShapeDtypeStruct((t, d), F32)] + ([] if final else [jax.ShapeDtypeStruct((t, d), BF16)])
    out_specs = [row] * len(out_shape)
    if emit_w:
        out_shape += [jax.ShapeDtypeStruct(w.shape, BF16) for w in (w1, w3, w2)]
        out_specs += [w_up, w_up, w_dn]
    return pl.pallas_call(
        functools.partial(_ffn_body, tm=tm, rows=min(256, tm), final=final, emit_w=emit_w),
        grid=grid,
        in_specs=[row, vec, w_up, w_up, w_dn, vec],
        out_specs=out_specs,
        out_shape=out_shape,
        scratch_shapes=[pltpu.VMEM((tm, d), BF16)],
        compiler_params=_cparams("parallel", "arbitrary"),
        name=("ffn_final" if final else "ffn_mid") + ("_castw" if emit_w else ""),
    )(x, g.reshape(1, d), w1, w3, w2, g2.reshape(1, d))


def _w_cols(a_ref, b_ref):
    rows = a_ref[...] if b_ref is None else jnp.concatenate([a_ref[b_ref.shape[0]:, :], b_ref[...]], axis=0)
    return rows.T.astype(BF16)


def _w_cols_specs(width, tn, start, col_of):
    shift = start % tn
    base = start - shift
    if shift == 0:
        return [pl.BlockSpec((tn, width), lambda *g: (base // tn + col_of(*g), 0))]
    assert shift % SUBLANES == 0 and tn % shift == 0 and base % shift == 0
    return [pl.BlockSpec((tn, width), lambda *g: (base // tn + col_of(*g), 0)),
            pl.BlockSpec((shift, width), lambda *g: ((base + tn * (col_of(*g) + 1)) // shift, 0))]


def _merge_body(u_ref, og_ref, or_ref, *refs, emit_w):
    u = u_ref[...]
    if not emit_w:
        wza_ref, wzb_ref, wgo_ref, wro_ref, m_ref = refs
        wza, wzb, wgo, wro = wza_ref[...], wzb_ref[...], wgo_ref[...], wro_ref[...]
    else:
        za_a, za_b, zb_a, zb_b, wgo_ref, wro_ref, m_ref, wza_o, wzb_o, wgo_o, wro_o = refs
        wza = _w_cols(za_a, za_b)
        wzb = _w_cols(zb_a, zb_b)
        wgo, wro = wgo_ref[...].astype(BF16), wro_ref[...].astype(BF16)
        wza_o[...], wzb_o[...], wgo_o[...], wro_o[...] = wza, wzb, wgo, wro
    za = _dot(u, wza)
    zb = _dot(u, wzb)
    branch_a = _dot(og_ref[...].astype(BF16), wgo)
    branch_b = _dot(or_ref[...].astype(BF16), wro)
    m_ref[...] = (_sigmoid(za) * branch_a + _sigmoid(zb) * branch_b).astype(BF16)


def _merge_gate(u, og, orr, wz, wgo, wro, *, tm=512):
    t, d = u.shape
    tm = min(tm, t)
    emit_w = wgo.dtype == F32
    assert not emit_w or t == tm, "weights are emitted by a single-token-tile call"
    tn = 256 if emit_w else 512
    col = lambda i, j: j
    acts = [pl.BlockSpec((tm, d), lambda i, j: (i, 0)),
            pl.BlockSpec((tm, GLA_V), lambda i, j: (i, 0)),
            pl.BlockSpec((tm, RET_V), lambda i, j: (i, 0))]
    w_col = lambda rows: pl.BlockSpec((rows, tn), lambda i, j: (0, j))
    out_shape = [jax.ShapeDtypeStruct((t, d), BF16)]
    out_specs = [pl.BlockSpec((tm, tn), lambda i, j: (i, j))]
    if emit_w:
        w_specs = (_w_cols_specs(d, tn, WB_START + WB_ZA, col) + _w_cols_specs(d, tn, WB_START + WB_ZB, col)
                   + [w_col(GLA_V), w_col(RET_V)])
        w_args = (wz, wz, wz, wz, wgo, wro)
        out_shape += [jax.ShapeDtypeStruct((d, d), BF16), jax.ShapeDtypeStruct((d, d), BF16),
                      jax.ShapeDtypeStruct(wgo.shape, BF16), jax.ShapeDtypeStruct(wro.shape, BF16)]
        out_specs += [w_col(d), w_col(d), w_col(GLA_V), w_col(RET_V)]
    else:
        w_specs, w_args = [w_col(d), w_col(d), w_col(GLA_V), w_col(RET_V)], (*wz, wgo, wro)
    return pl.pallas_call(
        functools.partial(_merge_body, emit_w=emit_w),
        grid=(t // tm, d // tn),
        in_specs=acts + w_specs,
        out_specs=out_specs,
        out_shape=out_shape,
        compiler_params=_cparams("parallel", "parallel"),
        name="merge_gate" + ("_castw" if emit_w else ""),
    )(u, og, orr, *w_args)


def _out_body(h_ref, m_ref, wout_ref, o_ref, *wb_ref):
    wout = wout_ref[...].astype(BF16)
    if wb_ref:
        wb_ref[0][...] = wout
    o_ref[...] = h_ref[...] + _dot(m_ref[...], wout)


def _out_proj(h, merged, wout, *, tm=512):
    t, d = h.shape
    tm = min(tm, t)
    emit_w = wout.dtype == F32
    assert not emit_w or t == tm, "weights are emitted by a single-token-tile call"
    tn = 512 if emit_w else d
    out_shape = [jax.ShapeDtypeStruct((t, d), F32)] + ([jax.ShapeDtypeStruct((d, d), BF16)] if emit_w else [])
    out_specs = [pl.BlockSpec((tm, tn), lambda i, j: (i, j))] + ([pl.BlockSpec((d, tn), lambda i, j: (0, j))] if emit_w else [])
    return pl.pallas_call(
        _out_body,
        grid=(t // tm, d // tn),
        in_specs=[pl.BlockSpec((tm, tn), lambda i, j: (i, j)),
                  pl.BlockSpec((tm, d), lambda i, j: (i, 0)),
                  pl.BlockSpec((d, tn), lambda i, j: (0, j))],
        out_specs=out_specs,
        out_shape=out_shape,
        compiler_params=_cparams("parallel", "parallel"),
        name="out_proj" + ("_castw" if emit_w else ""),
    )(h, merged, wout)


def _gla_body(u_ref, wq_ref, wk_ref, wv_ref, wg_ref, la_ref, ng_ref,
              dq_ref, dk_ref, dv_ref, dg_ref, dla_ref, ds_ref,
              og_ref, st_ref, dog_ref, dso_ref, s_scr, *, ts, c, nt, bt):
    s = pl.program_id(2)

    @pl.when(s == 0)
    def _():
        s_scr[...] = jnp.zeros_like(s_scr)

    u = u_ref[...]
    q_all = _dot(u, wq_ref[...]) * (GLA_DK ** -0.5)
    k_all = _dot(u, wk_ref[...])
    v_all = _dot(u, wv_ref[...]).astype(BF16)
    g_all = _dot(u, wg_ref[...])
    la_all = la_ref[...]
    la_hi = la_all.astype(BF16)
    la_lo = (la_all - la_hi.astype(F32)).astype(BF16)

    rr = lax.broadcasted_iota(jnp.int32, (c, c), 0)
    cc = lax.broadcasted_iota(jnp.int32, (c, c), 1)
    causal = rr >= cc
    tril = causal.astype(BF16)
    mid = c // 2
    rows = [slice(ci * c, (ci + 1) * c) for ci in range(ts // c)]

    b = [_dot(tril, la_hi[r]) + _dot(tril, la_lo[r]) for r in rows]
    q_rel, k_rel, q_dec, k_end, decay = [], [], [], [], []
    for r, bc in zip(rows, b):
        q, k = q_all[r], k_all[r]
        b_mid, b_last = bc[mid:mid + 1, :], bc[c - 1:c, :]
        q_rel.append((q * jnp.exp(bc - b_mid)).astype(BF16))
        k_rel.append((k * jnp.exp(b_mid - bc)).astype(BF16))
        q_dec.append((q * jnp.exp(bc)).astype(BF16))
        k_end.append((k * jnp.exp(b_last - bc)).astype(BF16))
        dcol = _lane_replicated_column(jnp.exp(b_last))
        decay.append(jnp.concatenate([dcol] * (GLA_DV // LANES), axis=1))
    att = [jnp.where(causal, _dot_tb(qr, kr), 0.0).astype(BF16) for qr, kr in zip(q_rel, k_rel)]
    o_intra = [_dot(a, v_all[r]) for a, r in zip(att, rows)]
    upd = [_dot_ta(ke, v_all[r]) for ke, r in zip(k_end, rows)]

    dec = _gla_decode_prep(dq_ref, dk_ref, dv_ref, dla_ref, ds_ref, dso_ref, nt=nt, bt=bt)
    dec.units(0, bt)
    dec.finish(dg_ref, ng_ref, dog_ref)

    st = s_scr[...]
    for ci, r in enumerate(rows):
        o = o_intra[ci] + _dot(q_dec[ci], st.astype(BF16))
        og_ref[r, :] = (_rms(o, ng_ref[...]) * _silu(g_all[r])).astype(BF16)
        st = decay[ci] * st + upd[ci]
    s_scr[...] = st

    @pl.when(s == pl.num_programs(2) - 1)
    def _():
        st_ref[...] = st


def _decode_tile(batch, ns, nb):
    bt = nb // (batch * ns)
    assert bt * batch * ns == nb and bt % SUBLANES == 0
    return bt


def _gla_mixer(u, la_p, batch, seq, w_a, ng, pg, la, state):
    ts, c = SEQ_TILE, GLA_CHUNK
    ns = seq // ts
    d = u.shape[1]
    nt, nb, _ = pg.shape
    bt = _decode_tile(batch, ns, nb)
    kq, kk, kv, kg = WA_Q // GLA_DK, WA_K // GLA_DK, WA_V // GLA_DV, WA_G // GLA_DV
    dec = lambda width, col0: pl.BlockSpec((nt, bt, width), lambda b, h, s: (0, b * ns + s, col0 + h))
    st_spec = pl.BlockSpec((bt, None, GLA_DK, GLA_DV), lambda b, h, s: (b * ns + s, h, 0, 0))
    return pl.pallas_call(
        functools.partial(_gla_body, ts=ts, c=c, nt=nt, bt=bt),
        grid=(batch, GLA_HEADS, ns),
        in_specs=[pl.BlockSpec((ts, d), lambda b, h, s: (b * ns + s, 0)),
                  pl.BlockSpec((d, GLA_DK), lambda b, h, s: (0, kq + h)),
                  pl.BlockSpec((d, GLA_DK), lambda b, h, s: (0, kk + h)),
                  pl.BlockSpec((d, GLA_DV), lambda b, h, s: (0, kv + h)),
                  pl.BlockSpec((d, GLA_DV), lambda b, h, s: (0, kg + h)),
                  pl.BlockSpec((ts, GLA_DK), lambda b, h, s: (b * ns + s, h)),
                  pl.BlockSpec((None, 1, GLA_DV), lambda b, h, s: (h, 0, 0)),
                  dec(GLA_DK, kq), dec(GLA_DK, kk), dec(GLA_DV, kv), dec(GLA_DV, kg), dec(GLA_DK, 0), st_spec],
        out_specs=(pl.BlockSpec((ts, GLA_DV), lambda b, h, s: (b * ns + s, h)),
                   pl.BlockSpec((None, None, GLA_DK, GLA_DV), lambda b, h, s: (b, h, 0, 0)),
                   dec(GLA_DV, 0), st_spec),
        out_shape=(jax.ShapeDtypeStruct((batch * seq, GLA_V), BF16),
                   jax.ShapeDtypeStruct((batch, GLA_HEADS, GLA_DK, GLA_DV), F32),
                   jax.ShapeDtypeStruct((nt, nb, GLA_V), F32),
                   jax.ShapeDtypeStruct(state.shape, F32)),
        scratch_shapes=[pltpu.VMEM((GLA_DK, GLA_DV), F32)],
        compiler_params=_cparams("parallel", "parallel", "arbitrary"),
        name="gla_mixer",
    )(u, w_a, w_a, w_a, w_a, la_p, ng, pg, pg, pg, pg, la, state)


def _rope(x, cos, sin):
    half = x.shape[-1] // 2
    x1, x2 = x[:, :half], x[:, half:]
    return jnp.concatenate([x1 * cos - x2 * sin, x2 * cos + x1 * sin], axis=-1)


def _ret_body(u_ref, wq_ref, wk_ref, wv_ref, wg_ref, cos_ref, sin_ref, dm_ref, qd_ref, kd_ref, cd_ref, ng_ref,
              pw_ref, dq_ref, dk_ref, dv_ref, dg_ref, dcos_ref, dsin_ref, ds_ref,
              or_ref, st_ref, dor_ref, dso_ref, s_scr, *, ts, c, nt, bt):
    s = pl.program_id(2)

    @pl.when(s == 0)
    def _():
        s_scr[...] = jnp.zeros_like(s_scr)

    dec = _ret_decode_prep(pw_ref, dq_ref, dk_ref, dv_ref, dcos_ref, dsin_ref, ds_ref, dso_ref, nt=nt, bt=bt)
    cut = [bt * n // 4 for n in range(5)]

    u = u_ref[...]
    tok = pl.ds(pl.multiple_of(s * ts, ts), ts)
    cos, sin = cos_ref[tok, :], sin_ref[tok, :]
    q_all = _rope(_dot(u, wq_ref[...]), cos, sin)
    dec.units(cut[0], cut[1])
    k_all = _rope(_dot(u, wk_ref[...]), cos, sin) * (RET_DK ** -0.5)
    dec.units(cut[1], cut[2])
    v_all = _dot(u, wv_ref[...]).astype(BF16)
    dec.units(cut[2], cut[3])
    g_all = _dot(u, wg_ref[...])
    dec.units(cut[3], cut[4])
    dec.finish(dg_ref, ng_ref, dor_ref)

    rows = [slice(ci * c, (ci + 1) * c) for ci in range(ts // c)]
    att = [(_dot_tb(q_all[r].astype(BF16), k_all[r].astype(BF16)) * dm_ref[...]).astype(BF16) for r in rows]
    o_intra = [_dot(a, v_all[r]) for a, r in zip(att, rows)]
    upd = [_dot_ta((k_all[r] * kd_ref[...]).astype(BF16), v_all[r]) for r in rows]
    q_dec = [(q_all[r] * qd_ref[...]).astype(BF16) for r in rows]
    st = s_scr[...]
    for ci, r in enumerate(rows):
        o = o_intra[ci] + _dot(q_dec[ci], st.astype(BF16))
        or_ref[r, :] = (_rms(o, ng_ref[...]) * _silu(g_all[r])).astype(BF16)
        st = cd_ref[...] * st + upd[ci]
    s_scr[...] = st

    @pl.when(s == pl.num_programs(2) - 1)
    def _():
        st_ref[...] = st


def _ret_log_gamma():
    return jnp.log1p(-jnp.exp2(-5.0 - jnp.arange(RET_HEADS, dtype=F32)))


def _rope_tables(pos):
    half = RET_DK // 2
    freqs = ROPE_BASE ** (-jnp.arange(half, dtype=F32) / half)
    ang = pos[:, None] * freqs[None, :]
    return jnp.cos(ang), jnp.sin(ang)


def _ret_mixer(u, batch, seq, w_b, ng, pr, state):
    ts, c = SEQ_TILE, RET_CHUNK
    ns = seq // ts
    d = u.shape[1]
    nt, nb, _ = pr.shape
    bt = _decode_tile(batch, ns, nb)
    kq, kk, kv, kg = WB_Q // RET_DK, WB_K // RET_DK, WB_V // RET_DV, WB_G // RET_DV
    cos, sin = _rope_tables(jnp.arange(seq, dtype=F32))
    dcos, dsin = _rope_tables(PAST_LEN + jnp.arange(nt, dtype=F32))
    pw = jnp.exp(_ret_log_gamma()[:, None] * jnp.arange(nt + 1, dtype=F32)[None, :])
    dec = lambda width, col0: pl.BlockSpec((nt, bt, width), lambda b, h, s: (0, b * ns + s, col0 + h))
    rope_spec = pl.BlockSpec((nt, RET_DK // 2), lambda b, h, s: (0, 0))
    st_spec = pl.BlockSpec((bt, None, RET_DK, RET_DV), lambda b, h, s: (b * ns + s, h, 0, 0))
    lg = _ret_log_gamma()
    idx = jnp.arange(c, dtype=F32)
    diff = idx[:, None] - idx[None, :]
    dmat = jnp.where(diff >= 0, jnp.exp(lg[:, None, None] * jnp.maximum(diff, 0.0)), 0.0)
    qdec = jnp.broadcast_to(jnp.exp(lg[:, None] * (idx + 1.0)[None, :])[:, :, None], (RET_HEADS, c, RET_DK))
    kdec = jnp.broadcast_to(jnp.exp(lg[:, None] * (c - 1.0 - idx)[None, :])[:, :, None], (RET_HEADS, c, RET_DK))
    cdec = jnp.broadcast_to(jnp.exp(lg * c)[:, None, None], (RET_HEADS, 1, RET_DV))
    return pl.pallas_call(
        functools.partial(_ret_body, ts=ts, c=c, nt=nt, bt=bt),
        grid=(batch, RET_HEADS, ns),
        in_specs=[pl.BlockSpec((ts, d), lambda b, h, s: (b * ns + s, 0)),
                  pl.BlockSpec((d, RET_DK), lambda b, h, s: (0, kq + h)),
                  pl.BlockSpec((d, RET_DK), lambda b, h, s: (0, kk + h)),
                  pl.BlockSpec((d, RET_DV), lambda b, h, s: (0, kv + h)),
                  pl.BlockSpec((d, RET_DV), lambda b, h, s: (0, kg + h)),
                  pl.BlockSpec((seq, RET_DK // 2), lambda b, h, s: (0, 0)),
                  pl.BlockSpec((seq, RET_DK // 2), lambda b, h, s: (0, 0)),
                  pl.BlockSpec((None, c, c), lambda b, h, s: (h, 0, 0)),
                  pl.BlockSpec((None, c, RET_DK), lambda b, h, s: (h, 0, 0)),
                  pl.BlockSpec((None, c, RET_DK), lambda b, h, s: (h, 0, 0)),
                  pl.BlockSpec((None, 1, RET_DV), lambda b, h, s: (h, 0, 0)),
                  pl.BlockSpec((None, 1, RET_DV), lambda b, h, s: (h, 0, 0)),
                  pl.BlockSpec(memory_space=pltpu.SMEM),
                  dec(RET_DK, kq), dec(RET_DK, kk), dec(RET_DV, kv), dec(RET_DV, kg), rope_spec, rope_spec, st_spec],
        out_specs=(pl.BlockSpec((ts, RET_DV), lambda b, h, s: (b * ns + s, h)),
                   pl.BlockSpec((None, None, RET_DK, RET_DV), lambda b, h, s: (b, h, 0, 0)),
                   dec(RET_DV, 0), st_spec),
        out_shape=(jax.ShapeDtypeStruct((batch * seq, RET_V), BF16),
                   jax.ShapeDtypeStruct((batch, RET_HEADS, RET_DK, RET_DV), F32),
                   jax.ShapeDtypeStruct((nt, nb, RET_V), F32),
                   jax.ShapeDtypeStruct(state.shape, F32)),
        scratch_shapes=[pltpu.VMEM((RET_DK, RET_DV), F32)],
        compiler_params=_cparams("parallel", "parallel", "arbitrary"),
        name="ret_mixer",
    )(u, w_b, w_b, w_b, w_b, cos, sin, dmat, qdec, kdec, cdec, ng, pw, pr, pr, pr, pr, dcos, dsin, state)


def _proj_body(u_ref, *refs):
    *w_refs, o_ref, wb_ref = refs
    w = _w_cols(w_refs[0], w_refs[1] if len(w_refs) > 1 else None)
    wb_ref[...] = w
    o_ref[...] = _dot(u_ref[...], w)


def _proj(u, w_t, start, n, *, tn=512):
    t, d = u.shape
    w_specs = _w_cols_specs(d, tn, start, lambda j: j)
    w_specs = [pl.BlockSpec(s.block_shape, s.index_map, pipeline_mode=pl.Buffered(3)) for s in w_specs]
    nw = len(w_specs)

    def outer(u_ref, *refs):
        w_hbm, (o_hbm, wb_hbm) = refs[:nw], refs[nw:]
        pltpu.emit_pipeline(
            functools.partial(_proj_body, u_ref),
            grid=(n // tn,),
            in_specs=w_specs,
            out_specs=[pl.BlockSpec((t, tn), lambda j: (0, j)), pl.BlockSpec((d, tn), lambda j: (0, j))],
        )(*w_hbm, o_hbm, wb_hbm)

    hbm = pl.BlockSpec(memory_space=pl.ANY)
    return pl.pallas_call(
        outer,
        in_specs=[pl.BlockSpec(memory_space=pltpu.VMEM)] + [hbm] * nw,
        out_specs=[hbm, hbm],
        out_shape=[jax.ShapeDtypeStruct((t, n), F32), jax.ShapeDtypeStruct((d, n), BF16)],
        compiler_params=pltpu.CompilerParams(vmem_limit_bytes=VMEM_LIMIT),
        name="sample_proj_castw",
    )(u, *([w_t] * nw))


def _gate_body(u_ref, wga_ref, wgu_ref, bg_ref, o_ref):
    ga = _dot(u_ref[...], wga_ref[...])
    logit = _dot(ga.astype(BF16), wgu_ref[...]) + bg_ref[...]
    o_ref[...] = _log_sigmoid(logit) / GLA_GATE_TAU


def _gate(u, wga, wgu, bg, *, tm=1024):
    t, d = u.shape
    tm = min(tm, t)
    return pl.pallas_call(
        _gate_body,
        grid=(t // tm,),
        in_specs=[pl.BlockSpec((tm, d), lambda i: (i, 0)),
                  pl.BlockSpec(wga.shape, lambda i: (0, 0)),
                  pl.BlockSpec(wgu.shape, lambda i: (0, 0)),
                  pl.BlockSpec(bg.shape, lambda i: (0, 0))],
        out_specs=pl.BlockSpec((tm, GLA_QK), lambda i: (i, 0)),
        out_shape=jax.ShapeDtypeStruct((t, GLA_QK), F32),
        compiler_params=_cparams("parallel"),
        name="log_gate",
    )(u, wga, wgu, bg)


def _row_group_mask(rows, cols, bt, i):
    return (lax.broadcasted_iota(jnp.int32, (rows, cols), 0) % bt) == i


class _Decode:
    def __init__(self, intra, q_dec, k_end, v_all, new_state, s_ref, so_ref, nt, bt):
        self.intra, self.q_dec, self.k_end, self.v_all = intra, q_dec, k_end, v_all
        self.new_state, self.s_ref, self.so_ref, self.nt, self.bt = new_state, s_ref, so_ref, nt, bt
        self.inter = jnp.zeros((nt * bt, v_all.shape[1]), F32)

    def units(self, lo, hi):
        rows, bt = self.nt * self.bt, self.bt
        for i in range(lo, hi):
            st = self.s_ref[i]
            qs = _dot(self.q_dec, st.astype(BF16))
            self.inter = self.inter + jnp.where(_row_group_mask(rows, qs.shape[1], bt, i), qs, 0.0)
            k_i = jnp.where(_row_group_mask(rows, self.k_end.shape[1], bt, i), self.k_end, 0.0).astype(BF16)
            self.so_ref[i] = self.new_state(i, st, _dot_ta(k_i, self.v_all))

    def finish(self, g_ref, ng_ref, o_ref):
        bt = self.bt
        for t in range(self.nt):
            o = self.intra[t] + self.inter[t * bt:(t + 1) * bt, :]
            o_ref[t] = _rms(o, ng_ref[...]) * _silu(g_ref[t])


def _gla_decode_prep(q_ref, k_ref, v_ref, la_ref, s_ref, so_ref, *, nt, bt):
    q = [q_ref[t] * (GLA_DK ** -0.5) for t in range(nt)]
    k = [k_ref[t] for t in range(nt)]
    v = [v_ref[t] for t in range(nt)]
    b = [la_ref[0]]
    for t in range(1, nt):
        b.append(b[-1] + la_ref[t])
    b_mid, b_last = b[nt // 2], b[nt - 1]
    q_rel = [q[t] * jnp.exp(b[t] - b_mid) for t in range(nt)]
    k_rel = [k[t] * jnp.exp(b_mid - b[t]) for t in range(nt)]
    intra = []
    for t in range(nt):
        acc = None
        for s in range(t + 1):
            term = jnp.sum(q_rel[t] * k_rel[s], axis=-1, keepdims=True) * v[s]
            acc = term if acc is None else acc + term
        intra.append(acc)
    q_dec = jnp.concatenate([q[t] * jnp.exp(b[t]) for t in range(nt)], axis=0).astype(BF16)
    k_end = jnp.concatenate([k[t] * jnp.exp(b_last - b[t]) for t in range(nt)], axis=0)
    v_all = jnp.concatenate(v, axis=0).astype(BF16)
    decay = jnp.exp(b_last)

    def new_state(i, st, upd):
        dcol = _lane_replicated_column(decay[i:i + 1, :])
        return jnp.concatenate([dcol] * (GLA_DV // LANES), axis=1) * st + upd

    return _Decode(intra, q_dec, k_end, v_all, new_state, s_ref, so_ref, nt, bt)


def _ret_decode_prep(pw_ref, q_ref, k_ref, v_ref, cos_ref, sin_ref, s_ref, so_ref, *, nt, bt):
    h = pl.program_id(1)
    pw = [pw_ref[h, n] for n in range(nt + 1)]
    q = [_rope(q_ref[t], cos_ref[t:t + 1, :], sin_ref[t:t + 1, :]) for t in range(nt)]
    k = [_rope(k_ref[t], cos_ref[t:t + 1, :], sin_ref[t:t + 1, :]) * (RET_DK ** -0.5) for t in range(nt)]
    v = [v_ref[t] for t in range(nt)]
    intra = []
    for t in range(nt):
        acc = None
        for s in range(t + 1):
            term = (jnp.sum(q[t] * k[s], axis=-1, keepdims=True) * pw[t - s]) * v[s]
            acc = term if acc is None else acc + term
        intra.append(acc)
    q_dec = jnp.concatenate([q[t] * pw[t + 1] for t in range(nt)], axis=0).astype(BF16)
    k_end = jnp.concatenate([k[t] * pw[nt - 1 - t] for t in range(nt)], axis=0)
    v_all = jnp.concatenate(v, axis=0).astype(BF16)
    return _Decode(intra, q_dec, k_end, v_all, lambda i, st, upd: pw[nt] * st + upd, s_ref, so_ref, nt, bt)


def kernel(x_prompt, x_sample, state_gla, state_ret, ffn1_norm, ffn1_w1, ffn1_w3, ffn1_w2, mix_norm, w_in, w_gate_up, b_gate, gla_norm, w_gla_o, ret_norm, w_ret_o, w_out, ffn2_norm, ffn2_w1, ffn2_w3, ffn2_w2, final_norm):
    depth = w_in.shape[0]
    batch, seq, d = x_prompt.shape
    nb, nt, _ = x_sample.shape

    hp = x_prompt.reshape(batch * seq, d)
    hs = x_sample.transpose(1, 0, 2).reshape(nt * nb, d)
    gla_p, ret_p, gla_s, ret_s = [], [], [], []
    for l in range(depth):
        last = l == depth - 1
        wl = w_in[l].T
        w_ga = w_in[l][:, WA_WIDTH:WB_START].astype(BF16)
        wgu = w_gate_up[l].astype(BF16)
        bg = b_gate[l].reshape(1, GLA_QK)
        ng = gla_norm[l].reshape(GLA_HEADS, 1, GLA_DV)
        nr = ret_norm[l].reshape(RET_HEADS, 1, RET_DV)
        g_next = final_norm if last else ffn1_norm[l + 1]

        hs, us, *f1 = _ffn(hs, ffn1_norm[l], ffn1_w1[l], ffn1_w3[l], ffn1_w2[l], mix_norm[l], final=False)
        hp, up = _ffn(hp, ffn1_norm[l], *f1, mix_norm[l], final=False)

        pg, w_a = _proj(us, wl, 0, WA_WIDTH)
        pr, w_b = _proj(us, wl, WB_START, WB_ZA)
        la = _gate(us, w_ga, wgu, bg)
        la_p = _gate(up, w_ga, wgu, bg)
        og, sg, og_s, sg_s = _gla_mixer(up, la_p, batch, seq, w_a, ng,
                                        pg.reshape(nt, nb, -1), la.reshape(nt, nb, -1), state_gla[l])
        orr, sr, or_s, sr_s = _ret_mixer(up, batch, seq, w_b, nr, pr.reshape(nt, nb, -1), state_ret[l])
        gla_s.append(sg_s)
        ret_s.append(sr_s)
        gla_p.append(sg)
        ret_p.append(sr)

        ms, wza, wzb, wgo, wro = _merge_gate(us, og_s.reshape(nt * nb, GLA_V), or_s.reshape(nt * nb, RET_V),
                                             wl, w_gla_o[l], w_ret_o[l])
        mp, = _merge_gate(up, og, orr, (wza, wzb), wgo, wro)
        hs, wout = _out_proj(hs, ms, w_out[l])
        hp, = _out_proj(hp, mp, wout)

        hs, *f2 = _ffn(hs, ffn2_norm[l], ffn2_w1[l], ffn2_w3[l], ffn2_w2[l], g_next, final=last)
        hp, *_ = _ffn(hp, ffn2_norm[l], *f2[-3:], g_next, final=last)

    y_prompt = hp.reshape(batch, seq, d)
    y_sample = hs.reshape(nt, nb, d).transpose(1, 0, 2)
    return (y_prompt, y_sample, jnp.stack(gla_p), jnp.stack(ret_p), jnp.stack(gla_s), jnp.stack(ret_s))
```

```python
import functools

import jax
import jax.numpy as jnp
from jax import lax
from jax.experimental import pallas as pl
from jax.experimental.pallas import tpu as pltpu

F32, BF16 = jnp.float32, jnp.bfloat16

D_MODEL = 2048
PAST_LEN = 16384
GLA_HEADS = 4
GLA_DK = D_MODEL // (2 * GLA_HEADS)
GLA_DV = D_MODEL // GLA_HEADS
GLA_GATE_RANK = 16
GLA_GATE_TAU = 16.0
RET_HEADS = 8
RET_DK = D_MODEL // RET_HEADS
RET_DV = 2 * D_MODEL // RET_HEADS
ROPE_BASE = 10000.0
EPS = 1e-6
GLA_QK = GLA_HEADS * GLA_DK
GLA_V = GLA_HEADS * GLA_DV
RET_QK = RET_HEADS * RET_DK
RET_V = RET_HEADS * RET_DV

WA_Q, WA_K, WA_V, WA_G = 0, GLA_QK, 2 * GLA_QK, 2 * GLA_QK + GLA_V
WA_WIDTH = 2 * GLA_QK + 2 * GLA_V
WB_Q, WB_K, WB_V, WB_G = 0, RET_QK, 2 * RET_QK, 2 * RET_QK + RET_V
WB_ZA = 2 * RET_QK + 2 * RET_V
WB_ZB = WB_ZA + D_MODEL
WB_START = WA_WIDTH + GLA_GATE_RANK

LANES = 128
SUBLANES = 8
MXU_WIDTH = 256
VMEM_LIMIT = 58 * 2**20

GLA_CHUNK = 64
RET_CHUNK = 256
SEQ_TILE = 512


def _cparams(*sem):
    return pltpu.CompilerParams(dimension_semantics=sem, vmem_limit_bytes=VMEM_LIMIT)


def _dot(a, b):
    return jnp.dot(a, b, preferred_element_type=F32)


def _dot_tb(a, b):
    return lax.dot_general(a, b, (((1,), (1,)), ((), ())), preferred_element_type=F32)


def _dot_ta(a, b):
    return lax.dot_general(a, b, (((0,), (0,)), ((), ())), preferred_element_type=F32)


def _rms(x, g):
    return x * lax.rsqrt(jnp.mean(x * x, axis=-1, keepdims=True) + EPS) * g


def _sigmoid(x):
    return 1.0 / (1.0 + jnp.exp(-x))


def _silu(x):
    return x * _sigmoid(x)


def _log_sigmoid(x):
    return jnp.minimum(x, 0.0) - jnp.log(1.0 + jnp.exp(-jnp.abs(x)))


def _lane_replicated_column(row):
    return jnp.broadcast_to(row, (LANES, row.shape[-1])).T


def _ffn_body(x_ref, g_ref, w1_ref, w3_ref, w2_ref, g2_ref, *refs, tm, rows, final, emit_w):
    refs = list(refs)
    u_scr = refs.pop()
    wb_refs = [refs.pop() for _ in range(3)][::-1] if emit_w else None
    if final:
        y_ref, = refs
        acc_ref = y_ref
    else:
        h_ref, un_ref = refs
        acc_ref = h_ref
    j = pl.program_id(1)

    @pl.when(j == 0)
    def _():
        for r in range(0, tm, rows):
            u_scr[r:r + rows, :] = _rms(x_ref[r:r + rows, :], g_ref[...]).astype(BF16)
            acc_ref[r:r + rows, :] = jnp.zeros((rows, acc_ref.shape[1]), F32)

    w1, w3, w2 = w1_ref[...].astype(BF16), w3_ref[...].astype(BF16), w2_ref[...].astype(BF16)
    if emit_w:
        for ref, w in zip(wb_refs, (w1, w3, w2)):
            ref[...] = w
    u = u_scr[...]
    part = None
    for c0 in range(0, w1.shape[1], MXU_WIDTH):
        cols = slice(c0, c0 + MXU_WIDTH)
        a = _dot(u, w1[:, cols])
        b = _dot(u, w3[:, cols])
        p = _dot((_silu(a) * b).astype(BF16), w2[cols, :])
        part = p if part is None else part + p
    acc_ref[...] += part

    @pl.when(j == pl.num_programs(1) - 1)
    def _():
        step = 2 * SUBLANES
        for r in range(0, tm, step):
            h = x_ref[r:r + step, :] + 0.5 * acc_ref[r:r + step, :]
            if final:
                y_ref[r:r + step, :] = _rms(h, g2_ref[...])
            else:
                h_ref[r:r + step, :] = h
                un_ref[r:r + step, :] = _rms(h, g2_ref[...]).astype(BF16)


def _ffn(x, g, w1, w3, w2, g2, *, final, tm=512):
    t, d = x.shape
    dff = w1.shape[1]
    tm = min(tm, t)
    emit_w = w1.dtype == F32
    assert not emit_w or t == tm, "weights are emitted by a single-token-tile call"
    tf = 256 if emit_w else 512
    grid = (t // tm, dff // tf)
    row = pl.BlockSpec((tm, d), lambda i, j: (i, 0))
    vec = pl.BlockSpec((1, d), lambda i, j: (0, 0))
    w_up = pl.BlockSpec((d, tf), lambda i, j: (0, j))
    w_dn = pl.BlockSpec((tf, d), lambda i, j: (j, 0))
    out_shape = [jax.ShapeDtypeStruct((t, d), F32)] + ([] if final else [jax.ShapeDtypeStruct((t, d), BF16)])
    out_specs = [row] * len(out_shape)
    if emit_w:
        out_shape += [jax.ShapeDtypeStruct(w.shape, BF16) for w in (w1, w3, w2)]
        out_specs += [w_up, w_up, w_dn]
    return pl.pallas_call(
        functools.partial(_ffn_body, tm=tm, rows=min(256, tm), final=final, emit_w=emit_w),
        grid=grid,
        in_specs=[row, vec, w_up, w_up, w_dn, vec],
        out_specs=out_specs,
        out_shape=out_shape,
        scratch_shapes=[pltpu.VMEM((tm, d), BF16)],
        compiler_params=_cparams("parallel", "arbitrary"),
        name=("ffn_final" if final else "ffn_mid") + ("_castw" if emit_w else ""),
    )(x, g.reshape(1, d), w1, w3, w2, g2.reshape(1, d))


def _w_cols(a_ref, b_ref):
    rows = a_ref[...] if b_ref is None else jnp.concatenate([a_ref[b_ref.shape[0]:, :], b_ref[...]], axis=0)
    return rows.T.astype(BF16)


def _w_cols_specs(width, tn, start, col_of):
    shift = start % tn
    base = start - shift
    if shift == 0:
        return [pl.BlockSpec((tn, width), lambda *g: (base // tn + col_of(*g), 0))]
    assert shift % SUBLANES == 0 and tn % shift == 0 and base % shift == 0
    return [pl.BlockSpec((tn, width), lambda *g: (base // tn + col_of(*g), 0)),
            pl.BlockSpec((shift, width), lambda *g: ((base + tn * (col_of(*g) + 1)) // shift, 0))]


def _merge_body(u_ref, og_ref, or_ref, *refs, emit_w):
    u = u_ref[...]
    if not emit_w:
        wza_ref, wzb_ref, wgo_ref, wro_ref, m_ref = refs
        wza, wzb, wgo, wro = wza_ref[...], wzb_ref[...], wgo_ref[...], wro_ref[...]
    else:
        za_a, za_b, zb_a, zb_b, wgo_ref, wro_ref, m_ref, wza_o, wzb_o, wgo_o, wro_o = refs
        wza = _w_cols(za_a, za_b)
        wzb = _w_cols(zb_a, zb_b)
        wgo, wro = wgo_ref[...].astype(BF16), wro_ref[...].astype(BF16)
        wza_o[...], wzb_o[...], wgo_o[...], wro_o[...] = wza, wzb, wgo, wro
    za = _dot(u, wza)
    zb = _dot(u, wzb)
    branch_a = _dot(og_ref[...].astype(BF16), wgo)
    branch_b = _dot(or_ref[...].astype(BF16), wro)
    m_ref[...] = (_sigmoid(za) * branch_a + _sigmoid(zb) * branch_b).astype(BF16)


def _merge_gate(u, og, orr, wz, wgo, wro, *, tm=512):
    t, d = u.shape
    tm = min(tm, t)
    emit_w = wgo.dtype == F32
    assert not emit_w or t == tm, "weights are emitted by a single-token-tile call"
    tn = 256 if emit_w else 512
    col = lambda i, j: j
    acts = [pl.BlockSpec((tm, d), lambda i, j: (i, 0)),
            pl.BlockSpec((tm, GLA_V), lambda i, j: (i, 0)),
            pl.BlockSpec((tm, RET_V), lambda i, j: (i, 0))]
    w_col = lambda rows: pl.BlockSpec((rows, tn), lambda i, j: (0, j))
    out_shape = [jax.ShapeDtypeStruct((t, d), BF16)]
    out_specs = [pl.BlockSpec((tm, tn), lambda i, j: (i, j))]
    if emit_w:
        w_specs = (_w_cols_specs(d, tn, WB_START + WB_ZA, col) + _w_cols_specs(d, tn, WB_START + WB_ZB, col)
                   + [w_col(GLA_V), w_col(RET_V)])
        w_args = (wz, wz, wz, wz, wgo, wro)
        out_shape += [jax.ShapeDtypeStruct((d, d), BF16), jax.ShapeDtypeStruct((d, d), BF16),
                      jax.ShapeDtypeStruct(wgo.shape, BF16), jax.ShapeDtypeStruct(wro.shape, BF16)]
        out_specs += [w_col(d), w_col(d), w_col(GLA_V), w_col(RET_V)]
    else:
        w_specs, w_args = [w_col(d), w_col(d), w_col(GLA_V), w_col(RET_V)], (*wz, wgo, wro)
    return pl.pallas_call(
        functools.partial(_merge_body, emit_w=emit_w),
        grid=(t // tm, d // tn),
        in_specs=acts + w_specs,
        out_specs=out_specs,
        out_shape=out_shape,
        compiler_params=_cparams("parallel", "parallel"),
        name="merge_gate" + ("_castw" if emit_w else ""),
    )(u, og, orr, *w_args)


def _out_body(h_ref, m_ref, wout_ref, o_ref, *wb_ref):
    wout = wout_ref[...].astype(BF16)
    if wb_ref:
        wb_ref[0][...] = wout
    o_ref[...] = h_ref[...] + _dot(m_ref[...], wout)


def _out_proj(h, merged, wout, *, tm=512):
    t, d = h.shape
    tm = min(tm, t)
    emit_w = wout.dtype == F32
    assert not emit_w or t == tm, "weights are emitted by a single-token-tile call"
    tn = 512 if emit_w else d
    out_shape = [jax.ShapeDtypeStruct((t, d), F32)] + ([jax.ShapeDtypeStruct((d, d), BF16)] if emit_w else [])
    out_specs = [pl.BlockSpec((tm, tn), lambda i, j: (i, j))] + ([pl.BlockSpec((d, tn), lambda i, j: (0, j))] if emit_w else [])
    return pl.pallas_call(
        _out_body,
        grid=(t // tm, d // tn),
        in_specs=[pl.BlockSpec((tm, tn), lambda i, j: (i, j)),
                  pl.BlockSpec((tm, d), lambda i, j: (i, 0)),
                  pl.BlockSpec((d, tn), lambda i, j: (0, j))],
        out_specs=out_specs,
        out_shape=out_shape,
        compiler_params=_cparams("parallel", "parallel"),
        name="out_proj" + ("_castw" if emit_w else ""),
    )(h, merged, wout)


def _gla_body(u_ref, wq_ref, wk_ref, wv_ref, wg_ref, la_ref, ng_ref,
              dq_ref, dk_ref, dv_ref, dg_ref, dla_ref, ds_ref,
              og_ref, st_ref, dog_ref, dso_ref, s_scr, *, ts, c, nt, bt):
    s = pl.program_id(2)

    @pl.when(s == 0)
    def _():
        s_scr[...] = jnp.zeros_like(s_scr)

    u = u_ref[...]
    q_all = _dot(u, wq_ref[...]) * (GLA_DK ** -0.5)
    k_all = _dot(u, wk_ref[...])
    v_all = _dot(u, wv_ref[...]).astype(BF16)
    g_all = _dot(u, wg_ref[...])
    la_all = la_ref[...]
    la_hi = la_all.astype(BF16)
    la_lo = (la_all - la_hi.astype(F32)).astype(BF16)

    rr = lax.broadcasted_iota(jnp.int32, (c, c), 0)
    cc = lax.broadcasted_iota(jnp.int32, (c, c), 1)
    causal = rr >= cc
    tril = causal.astype(BF16)
    mid = c // 2
    rows = [slice(ci * c, (ci + 1) * c) for ci in range(ts // c)]

    b = [_dot(tril, la_hi[r]) + _dot(tril, la_lo[r]) for r in rows]
    q_rel, k_rel, q_dec, k_end, decay = [], [], [], [], []
    for r, bc in zip(rows, b):
        q, k = q_all[r], k_all[r]
        b_mid, b_last = bc[mid:mid + 1, :], bc[c - 1:c, :]
        q_rel.append((q * jnp.exp(bc - b_mid)).astype(BF16))
        k_rel.append((k * jnp.exp(b_mid - bc)).astype(BF16))
        q_dec.append((q * jnp.exp(bc)).astype(BF16))
        k_end.append((k * jnp.exp(b_last - bc)).astype(BF16))
        dcol = _lane_replicated_column(jnp.exp(b_last))
        decay.append(jnp.concatenate([dcol] * (GLA_DV // LANES), axis=1))
    att = [jnp.where(causal, _dot_tb(qr, kr), 0.0).astype(BF16) for qr, kr in zip(q_rel, k_rel)]
    o_intra = [_dot(a, v_all[r]) for a, r in zip(att, rows)]
    upd = [_dot_ta(ke, v_all[r]) for ke, r in zip(k_end, rows)]

    dec = _gla_decode_prep(dq_ref, dk_ref, dv_ref, dla_ref, ds_ref, dso_ref, nt=nt, bt=bt)
    dec.units(0, bt)
    dec.finish(dg_ref, ng_ref, dog_ref)

    st = s_scr[...]
    for ci, r in enumerate(rows):
        o = o_intra[ci] + _dot(q_dec[ci], st.astype(BF16))
        og_ref[r, :] = (_rms(o, ng_ref[...]) * _silu(g_all[r])).astype(BF16)
        st = decay[ci] * st + upd[ci]
    s_scr[...] = st

    @pl.when(s == pl.num_programs(2) - 1)
    def _():
        st_ref[...] = st


def _decode_tile(batch, ns, nb):
    bt = nb // (batch * ns)
    assert bt * batch * ns == nb and bt % SUBLANES == 0
    return bt


def _gla_mixer(u, la_p, batch, seq, w_a, ng, pg, la, state):
    ts, c = SEQ_TILE, GLA_CHUNK
    ns = seq // ts
    d = u.shape[1]
    nt, nb, _ = pg.shape
    bt = _decode_tile(batch, ns, nb)
    kq, kk, kv, kg = WA_Q // GLA_DK, WA_K // GLA_DK, WA_V // GLA_DV, WA_G // GLA_DV
    dec = lambda width, col0: pl.BlockSpec((nt, bt, width), lambda h, b, s: (0, b * ns + s, col0 + h))
    st_spec = pl.BlockSpec((bt, None, GLA_DK, GLA_DV), lambda h, b, s: (b * ns + s, h, 0, 0))
    return pl.pallas_call(
        functools.partial(_gla_body, ts=ts, c=c, nt=nt, bt=bt),
        grid=(GLA_HEADS, batch, ns),
        in_specs=[pl.BlockSpec((ts, d), lambda h, b, s: (b * ns + s, 0)),
                  pl.BlockSpec((d, GLA_DK), lambda h, b, s: (0, kq + h)),
                  pl.BlockSpec((d, GLA_DK), lambda h, b, s: (0, kk + h)),
                  pl.BlockSpec((d, GLA_DV), lambda h, b, s: (0, kv + h)),
                  pl.BlockSpec((d, GLA_DV), lambda h, b, s: (0, kg + h)),
                  pl.BlockSpec((ts, GLA_DK), lambda h, b, s: (b * ns + s, h)),
                  pl.BlockSpec((None, 1, GLA_DV), lambda h, b, s: (h, 0, 0)),
                  dec(GLA_DK, kq), dec(GLA_DK, kk), dec(GLA_DV, kv), dec(GLA_DV, kg), dec(GLA_DK, 0), st_spec],
        out_specs=(pl.BlockSpec((ts, GLA_DV), lambda h, b, s: (b * ns + s, h)),
                   pl.BlockSpec((None, None, GLA_DK, GLA_DV), lambda h, b, s: (b, h, 0, 0)),
                   dec(GLA_DV, 0), st_spec),
        out_shape=(jax.ShapeDtypeStruct((batch * seq, GLA_V), BF16),
                   jax.ShapeDtypeStruct((batch, GLA_HEADS, GLA_DK, GLA_DV), F32),
                   jax.ShapeDtypeStruct((nt, nb, GLA_V), F32),
                   jax.ShapeDtypeStruct(state.shape, F32)),
        scratch_shapes=[pltpu.VMEM((GLA_DK, GLA_DV), F32)],
        compiler_params=_cparams("parallel", "parallel", "arbitrary"),
        name="gla_mixer",
    )(u, w_a, w_a, w_a, w_a, la_p, ng, pg, pg, pg, pg, la, state)


def _rope(x, cos, sin):
    half = x.shape[-1] // 2
    x1, x2 = x[:, :half], x[:, half:]
    return jnp.concatenate([x1 * cos - x2 * sin, x2 * cos + x1 * sin], axis=-1)


def _ret_body(u_ref, wq_ref, wk_ref, wv_ref, wg_ref, cos_ref, sin_ref, dm_ref, qd_ref, kd_ref, cd_ref, ng_ref,
              pw_ref, dq_ref, dk_ref, dv_ref, dg_ref, dcos_ref, dsin_ref, ds_ref,
              or_ref, st_ref, dor_ref, dso_ref, s_scr, *, ts, c, nt, bt):
    s = pl.program_id(2)

    @pl.when(s == 0)
    def _():
        s_scr[...] = jnp.zeros_like(s_scr)

    dec = _ret_decode_prep(pw_ref, dq_ref, dk_ref, dv_ref, dcos_ref, dsin_ref, ds_ref, dso_ref, nt=nt, bt=bt)
    cut = [bt * n // 4 for n in range(5)]

    u = u_ref[...]
    tok = pl.ds(pl.multiple_of(s * ts, ts), ts)
    cos, sin = cos_ref[tok, :], sin_ref[tok, :]
    q_all = _rope(_dot(u, wq_ref[...]), cos, sin)
    dec.units(cut[0], cut[1])
    k_all = _rope(_dot(u, wk_ref[...]), cos, sin) * (RET_DK ** -0.5)
    dec.units(cut[1], cut[2])
    v_all = _dot(u, wv_ref[...]).astype(BF16)
    dec.units(cut[2], cut[3])
    g_all = _dot(u, wg_ref[...])
    dec.units(cut[3], cut[4])
    dec.finish(dg_ref, ng_ref, dor_ref)

    rows = [slice(ci * c, (ci + 1) * c) for ci in range(ts // c)]
    att = [(_dot_tb(q_all[r].astype(BF16), k_all[r].astype(BF16)) * dm_ref[...]).astype(BF16) for r in rows]
    o_intra = [_dot(a, v_all[r]) for a, r in zip(att, rows)]
    upd = [_dot_ta((k_all[r] * kd_ref[...]).astype(BF16), v_all[r]) for r in rows]
    q_dec = [(q_all[r] * qd_ref[...]).astype(BF16) for r in rows]
    st = s_scr[...]
    for ci, r in enumerate(rows):
        o = o_intra[ci] + _dot(q_dec[ci], st.astype(BF16))
        or_ref[r, :] = (_rms(o, ng_ref[...]) * _silu(g_all[r])).astype(BF16)
        st = cd_ref[...] * st + upd[ci]
    s_scr[...] = st

    @pl.when(s == pl.num_programs(2) - 1)
    def _():
        st_ref[...] = st


def _ret_log_gamma():
    return jnp.log1p(-jnp.exp2(-5.0 - jnp.arange(RET_HEADS, dtype=F32)))


def _rope_tables(pos):
    half = RET_DK // 2
    freqs = ROPE_BASE ** (-jnp.arange(half, dtype=F32) / half)
    ang = pos[:, None] * freqs[None, :]
    return jnp.cos(ang), jnp.sin(ang)


def _ret_mixer(u, batch, seq, w_b, ng, pr, state):
    ts, c = SEQ_TILE, RET_CHUNK
    ns = seq // ts
    d = u.shape[1]
    nt, nb, _ = pr.shape
    bt = _decode_tile(batch, ns, nb)
    kq, kk, kv, kg = WB_Q // RET_DK, WB_K // RET_DK, WB_V // RET_DV, WB_G // RET_DV
    cos, sin = _rope_tables(jnp.arange(seq, dtype=F32))
    dcos, dsin = _rope_tables(PAST_LEN + jnp.arange(nt, dtype=F32))
    pw = jnp.exp(_ret_log_gamma()[:, None] * jnp.arange(nt + 1, dtype=F32)[None, :])
    dec = lambda width, col0: pl.BlockSpec((nt, bt, width), lambda h, b, s: (0, b * ns + s, col0 + h))
    rope_spec = pl.BlockSpec((nt, RET_DK // 2), lambda h, b, s: (0, 0))
    st_spec = pl.BlockSpec((bt, None, RET_DK, RET_DV), lambda h, b, s: (b * ns + s, h, 0, 0))
    lg = _ret_log_gamma()
    idx = jnp.arange(c, dtype=F32)
    diff = idx[:, None] - idx[None, :]
    dmat = jnp.where(diff >= 0, jnp.exp(lg[:, None, None] * jnp.maximum(diff, 0.0)), 0.0)
    qdec = jnp.broadcast_to(jnp.exp(lg[:, None] * (idx + 1.0)[None, :])[:, :, None], (RET_HEADS, c, RET_DK))
    kdec = jnp.broadcast_to(jnp.exp(lg[:, None] * (c - 1.0 - idx)[None, :])[:, :, None], (RET_HEADS, c, RET_DK))
    cdec = jnp.broadcast_to(jnp.exp(lg * c)[:, None, None], (RET_HEADS, 1, RET_DV))
    return pl.pallas_call(
        functools.partial(_ret_body, ts=ts, c=c, nt=nt, bt=bt),
        grid=(RET_HEADS, batch, ns),
        in_specs=[pl.BlockSpec((ts, d), lambda h, b, s: (b * ns + s, 0)),
                  pl.BlockSpec((d, RET_DK), lambda h, b, s: (0, kq + h)),
                  pl.BlockSpec((d, RET_DK), lambda h, b, s: (0, kk + h)),
                  pl.BlockSpec((d, RET_DV), lambda h, b, s: (0, kv + h)),
                  pl.BlockSpec((d, RET_DV), lambda h, b, s: (0, kg + h)),
                  pl.BlockSpec((seq, RET_DK // 2), lambda h, b, s: (0, 0)),
                  pl.BlockSpec((seq, RET_DK // 2), lambda h, b, s: (0, 0)),
                  pl.BlockSpec((None, c, c), lambda h, b, s: (h, 0, 0)),
                  pl.BlockSpec((None, c, RET_DK), lambda h, b, s: (h, 0, 0)),
                  pl.BlockSpec((None, c, RET_DK), lambda h, b, s: (h, 0, 0)),
                  pl.BlockSpec((None, 1, RET_DV), lambda h, b, s: (h, 0, 0)),
                  pl.BlockSpec((None, 1, RET_DV), lambda h, b, s: (h, 0, 0)),
                  pl.BlockSpec(memory_space=pltpu.SMEM),
                  dec(RET_DK, kq), dec(RET_DK, kk), dec(RET_DV, kv), dec(RET_DV, kg), rope_spec, rope_spec, st_spec],
        out_specs=(pl.BlockSpec((ts, RET_DV), lambda h, b, s: (b * ns + s, h)),
                   pl.BlockSpec((None, None, RET_DK, RET_DV), lambda h, b, s: (b, h, 0, 0)),
                   dec(RET_DV, 0), st_spec),
        out_shape=(jax.ShapeDtypeStruct((batch * seq, RET_V), BF16),
                   jax.ShapeDtypeStruct((batch, RET_HEADS, RET_DK, RET_DV), F32),
                   jax.ShapeDtypeStruct((nt, nb, RET_V), F32),
                   jax.ShapeDtypeStruct(state.shape, F32)),
        scratch_shapes=[pltpu.VMEM((RET_DK, RET_DV), F32)],
        compiler_params=_cparams("parallel", "parallel", "arbitrary"),
        name="ret_mixer",
    )(u, w_b, w_b, w_b, w_b, cos, sin, dmat, qdec, kdec, cdec, ng, pw, pr, pr, pr, pr, dcos, dsin, state)


def _proj_body(u_ref, *refs):
    *w_refs, o_ref, wb_ref = refs
    w = _w_cols(w_refs[0], w_refs[1] if len(w_refs) > 1 else None)
    wb_ref[...] = w
    o_ref[...] = _dot(u_ref[...], w)


def _proj(u, w_t, start, n, *, tn=1024):
    t, d = u.shape
    w_specs = _w_cols_specs(d, tn, start, lambda j: j)
    return pl.pallas_call(
        _proj_body,
        grid=(n // tn,),
        in_specs=[pl.BlockSpec((t, d), lambda j: (0, 0))] + w_specs,
        out_specs=[pl.BlockSpec((t, tn), lambda j: (0, j)), pl.BlockSpec((d, tn), lambda j: (0, j))],
        out_shape=[jax.ShapeDtypeStruct((t, n), F32), jax.ShapeDtypeStruct((d, n), BF16)],
        compiler_params=_cparams("parallel"),
        name="sample_proj_castw",
    )(u, *([w_t] * len(w_specs)))


def _gate_body(u_ref, wga_ref, wgu_ref, bg_ref, o_ref):
    ga = _dot(u_ref[...], wga_ref[...])
    logit = _dot(ga.astype(BF16), wgu_ref[...]) + bg_ref[...]
    o_ref[...] = _log_sigmoid(logit) / GLA_GATE_TAU


def _gate(u, wga, wgu, bg, *, tm=1024):
    t, d = u.shape
    tm = min(tm, t)
    return pl.pallas_call(
        _gate_body,
        grid=(t // tm,),
        in_specs=[pl.BlockSpec((tm, d), lambda i: (i, 0)),
                  pl.BlockSpec(wga.shape, lambda i: (0, 0)),
                  pl.BlockSpec(wgu.shape, lambda i: (0, 0)),
                  pl.BlockSpec(bg.shape, lambda i: (0, 0))],
        out_specs=pl.BlockSpec((tm, GLA_QK), lambda i: (i, 0)),
        out_shape=jax.ShapeDtypeStruct((t, GLA_QK), F32),
        compiler_params=_cparams("parallel"),
        name="log_gate",
    )(u, wga, wgu, bg)


def _row_group_mask(rows, cols, bt, i):
    return (lax.broadcasted_iota(jnp.int32, (rows, cols), 0) % bt) == i


class _Decode:
    def __init__(self, intra, q_dec, k_end, v_all, new_state, s_ref, so_ref, nt, bt):
        self.intra, self.q_dec, self.k_end, self.v_all = intra, q_dec, k_end, v_all
        self.new_state, self.s_ref, self.so_ref, self.nt, self.bt = new_state, s_ref, so_ref, nt, bt
        self.inter = jnp.zeros((nt * bt, v_all.shape[1]), F32)

    def units(self, lo, hi):
        rows, bt = self.nt * self.bt, self.bt
        for i in range(lo, hi):
            st = self.s_ref[i]
            qs = _dot(self.q_dec, st.astype(BF16))
            self.inter = self.inter + jnp.where(_row_group_mask(rows, qs.shape[1], bt, i), qs, 0.0)
            k_i = jnp.where(_row_group_mask(rows, self.k_end.shape[1], bt, i), self.k_end, 0.0).astype(BF16)
            self.so_ref[i] = self.new_state(i, st, _dot_ta(k_i, self.v_all))

    def finish(self, g_ref, ng_ref, o_ref):
        bt = self.bt
        for t in range(self.nt):
            o = self.intra[t] + self.inter[t * bt:(t + 1) * bt, :]
            o_ref[t] = _rms(o, ng_ref[...]) * _silu(g_ref[t])


def _gla_decode_prep(q_ref, k_ref, v_ref, la_ref, s_ref, so_ref, *, nt, bt):
    q = [q_ref[t] * (GLA_DK ** -0.5) for t in range(nt)]
    k = [k_ref[t] for t in range(nt)]
    v = [v_ref[t] for t in range(nt)]
    b = [la_ref[0]]
    for t in range(1, nt):
        b.append(b[-1] + la_ref[t])
    b_mid, b_last = b[nt // 2], b[nt - 1]
    q_rel = [q[t] * jnp.exp(b[t] - b_mid) for t in range(nt)]
    k_rel = [k[t] * jnp.exp(b_mid - b[t]) for t in range(nt)]
    intra = []
    for t in range(nt):
        acc = None
        for s in range(t + 1):
            term = jnp.sum(q_rel[t] * k_rel[s], axis=-1, keepdims=True) * v[s]
            acc = term if acc is None else acc + term
        intra.append(acc)
    q_dec = jnp.concatenate([q[t] * jnp.exp(b[t]) for t in range(nt)], axis=0).astype(BF16)
    k_end = jnp.concatenate([k[t] * jnp.exp(b_last - b[t]) for t in range(nt)], axis=0)
    v_all = jnp.concatenate(v, axis=0).astype(BF16)
    decay = jnp.exp(b_last)

    def new_state(i, st, upd):
        dcol = _lane_replicated_column(decay[i:i + 1, :])
        return jnp.concatenate([dcol] * (GLA_DV // LANES), axis=1) * st + upd

    return _Decode(intra, q_dec, k_end, v_all, new_state, s_ref, so_ref, nt, bt)


def _ret_decode_prep(pw_ref, q_ref, k_ref, v_ref, cos_ref, sin_ref, s_ref, so_ref, *, nt, bt):
    h = pl.program_id(0)
    pw = [pw_ref[h, n] for n in range(nt + 1)]
    q = [_rope(q_ref[t], cos_ref[t:t + 1, :], sin_ref[t:t + 1, :]) for t in range(nt)]
    k = [_rope(k_ref[t], cos_ref[t:t + 1, :], sin_ref[t:t + 1, :]) * (RET_DK ** -0.5) for t in range(nt)]
    v = [v_ref[t] for t in range(nt)]
    intra = []
    for t in range(nt):
        acc = None
        for s in range(t + 1):
            term = (jnp.sum(q[t] * k[s], axis=-1, keepdims=True) * pw[t - s]) * v[s]
            acc = term if acc is None else acc + term
        intra.append(acc)
    q_dec = jnp.concatenate([q[t] * pw[t + 1] for t in range(nt)], axis=0).astype(BF16)
    k_end = jnp.concatenate([k[t] * pw[nt - 1 - t] for t in range(nt)], axis=0)
    v_all = jnp.concatenate(v, axis=0).astype(BF16)
    return _Decode(intra, q_dec, k_end, v_all, lambda i, st, upd: pw[nt] * st + upd, s_ref, so_ref, nt, bt)


def kernel(x_prompt, x_sample, state_gla, state_ret, ffn1_norm, ffn1_w1, ffn1_w3, ffn1_w2, mix_norm, w_in, w_gate_up, b_gate, gla_norm, w_gla_o, ret_norm, w_ret_o, w_out, ffn2_norm, ffn2_w1, ffn2_w3, ffn2_w2, final_norm):
    depth = w_in.shape[0]
    batch, seq, d = x_prompt.shape
    nb, nt, _ = x_sample.shape

    hp = x_prompt.reshape(batch * seq, d)
    hs = x_sample.transpose(1, 0, 2).reshape(nt * nb, d)
    gla_p, ret_p, gla_s, ret_s = [], [], [], []
    for l in range(depth):
        last = l == depth - 1
        wl = w_in[l].T
        w_ga = w_in[l][:, WA_WIDTH:WB_START].astype(BF16)
        wgu = w_gate_up[l].astype(BF16)
        bg = b_gate[l].reshape(1, GLA_QK)
        ng = gla_norm[l].reshape(GLA_HEADS, 1, GLA_DV)
        nr = ret_norm[l].reshape(RET_HEADS, 1, RET_DV)
        g_next = final_norm if last else ffn1_norm[l + 1]

        hs, us, *f1 = _ffn(hs, ffn1_norm[l], ffn1_w1[l], ffn1_w3[l], ffn1_w2[l], mix_norm[l], final=False)
        hp, up = _ffn(hp, ffn1_norm[l], *f1, mix_norm[l], final=False)

        pg, w_a = _proj(us, wl, 0, WA_WIDTH)
        pr, w_b = _proj(us, wl, WB_START, WB_ZA)
        la = _gate(us, w_ga, wgu, bg)
        la_p = _gate(up, w_ga, wgu, bg)
        og, sg, og_s, sg_s = _gla_mixer(up, la_p, batch, seq, w_a, ng,
                                        pg.reshape(nt, nb, -1), la.reshape(nt, nb, -1), state_gla[l])
        orr, sr, or_s, sr_s = _ret_mixer(up, batch, seq, w_b, nr, pr.reshape(nt, nb, -1), state_ret[l])
        gla_s.append(sg_s)
        ret_s.append(sr_s)
        gla_p.append(sg)
        ret_p.append(sr)

        ms, wza, wzb, wgo, wro = _merge_gate(us, og_s.reshape(nt * nb, GLA_V), or_s.reshape(nt * nb, RET_V),
                                             wl, w_gla_o[l], w_ret_o[l])
        mp, = _merge_gate(up, og, orr, (wza, wzb), wgo, wro)
        hs, wout = _out_proj(hs, ms, w_out[l])
        hp, = _out_proj(hp, mp, wout)

        hs, *f2 = _ffn(hs, ffn2_norm[l], ffn2_w1[l], ffn2_w3[l], ffn2_w2[l], g_next, final=last)
        hp, *_ = _ffn(hp, ffn2_norm[l], *f2[-3:], g_next, final=last)

    y_prompt = hp.reshape(batch, seq, d)
    y_sample = hs.reshape(nt, nb, d).transpose(1, 0, 2)
    return (y_prompt, y_sample, jnp.stack(gla_p), jnp.stack(ret_p), jnp.stack(gla_s), jnp.stack(ret_s))
```
